```python
import math
import jax, jax.numpy as jnp
from jax import lax
import numpy as np

D_MODEL = 2048
BATCH = 1
SEQ = 8192
DEPTH = 2

N_MIXERS = 2
N_ATTN = (DEPTH + 1) // 2
N_HGRN = DEPTH // 2

ATTN_HEAD_DIM = 64
ATTN_HEADS = D_MODEL // ATTN_HEAD_DIM
ATTN_KV_HEADS = ATTN_HEADS // 8
ATTN_GROUP = ATTN_HEADS // ATTN_KV_HEADS
WINDOW = 128
BLOCK = 128
ATTN_Q_DIM = ATTN_HEADS * ATTN_HEAD_DIM
ATTN_KV_DIM = ATTN_KV_HEADS * ATTN_HEAD_DIM
ATTN_IN_DIM = ATTN_Q_DIM + 2 * ATTN_KV_DIM
ATTN_SCALE = 1.0 / math.sqrt(ATTN_HEAD_DIM)

HGRN_EXPAND = 128
HGRN_HEADS = D_MODEL // HGRN_EXPAND
HGRN_DK = HGRN_EXPAND
HGRN_DV = D_MODEL // HGRN_HEADS
HGRN_FDIM = HGRN_HEADS * HGRN_DK
HGRN_IDIM = HGRN_HEADS * HGRN_DV
HGRN_IN_DIM = 2 * HGRN_FDIM + 2 * HGRN_IDIM
HGRN_SCALE = 1.0 / math.sqrt(HGRN_DK)
CHUNK = 64

D_FF = 4 * D_MODEL
N_MOD = 6
EPS = 1e-6

kernel_name = "hybrid_swa_hgrn2_block"


def rms_norm(x, gain):
    xf = x.astype(jnp.float32)
    y = xf * lax.rsqrt(jnp.mean(xf * xf, axis=-1, keepdims=True) + EPS)
    return (y * gain.astype(jnp.float32)).astype(x.dtype)


def modulate(h, shift, scale):
    return h * (1.0 + scale[:, None, :]) + shift[:, None, :]


def alibi_slopes(n_heads):
    return jnp.exp2(-8.0 * jnp.arange(1, n_heads + 1, dtype=jnp.float32) / n_heads)


def swa_sink_attention(h, w_in, w_out, q_gain, k_gain, sinks):
    B, T, _ = h.shape
    nb = T // BLOCK
    proj = h @ w_in
    q, k, v = jnp.split(proj, [ATTN_Q_DIM, ATTN_Q_DIM + ATTN_KV_DIM], axis=-1)
    q = rms_norm(q.reshape(B, T, ATTN_HEADS, ATTN_HEAD_DIM), q_gain)
    k = rms_norm(k.reshape(B, T, ATTN_KV_HEADS, ATTN_HEAD_DIM), k_gain)
    v = v.reshape(B, T, ATTN_KV_HEADS, ATTN_HEAD_DIM)
    q = q.reshape(B, nb, BLOCK, ATTN_KV_HEADS, ATTN_GROUP, ATTN_HEAD_DIM)

    def band(a):
        ap = jnp.pad(a, ((0, 0), (BLOCK, 0), (0, 0), (0, 0)))
        ap = ap.reshape(B, nb + 1, BLOCK, ATTN_KV_HEADS, ATTN_HEAD_DIM)
        return jnp.concatenate([ap[:, :-1], ap[:, 1:]], axis=2)

    kb, vb = band(k), band(v)
    logits = jnp.einsum('bnqkgd,bnskd->bnkgqs', q, kb,
                        preferred_element_type=jnp.float32) * ATTN_SCALE

    kpos = jnp.arange(2 * BLOCK)
    dist = (jnp.arange(BLOCK) + BLOCK)[:, None] - kpos[None, :]
    in_band = (dist >= 0) & (dist < WINDOW)
    s_abs = (jnp.arange(nb) * BLOCK - BLOCK)[:, None, None] + kpos[None, None, :]
    valid = in_band[None] & (s_abs >= 0)
    slopes = alibi_slopes(ATTN_HEADS).reshape(ATTN_KV_HEADS, ATTN_GROUP)
    alibi = -slopes[:, :, None, None] * jnp.abs(dist).astype(jnp.float32)
    logits = jnp.where(valid[None, :, None, None], logits + alibi, -jnp.inf)

    sink = jnp.broadcast_to(
        sinks.astype(jnp.float32).reshape(1, 1, ATTN_KV_HEADS, ATTN_GROUP, 1, 1),
        logits.shape[:-1] + (1,))
    probs = jax.nn.softmax(jnp.concatenate([logits, sink], axis=-1), axis=-1)[..., :-1]
    out = jnp.einsum('bnkgqs,bnskd->bnqkgd', probs.astype(vb.dtype), vb)
    return out.reshape(B, T, ATTN_Q_DIM) @ w_out


def hgrn2_mixer(h, w_in, w_out, o_gain, lower_bound):
    B, T, _ = h.shape
    nc = T // CHUNK
    proj = (h @ w_in).astype(jnp.float32)
    q, f, v, g = jnp.split(proj, [HGRN_FDIM, 2 * HGRN_FDIM, 2 * HGRN_FDIM + HGRN_IDIM], axis=-1)
    q = jax.nn.silu(q) * HGRN_SCALE
    forget = lower_bound + (1.0 - lower_bound) * jax.nn.sigmoid(f)
    k = 1.0 - forget
    logf = jnp.log(forget)

    def to_chunks(a, d):
        return a.reshape(B, nc, CHUNK, HGRN_HEADS, d)

    q, k, logf = to_chunks(q, HGRN_DK), to_chunks(k, HGRN_DK), to_chunks(logf, HGRN_DK)
    v = to_chunks(v, HGRN_DV)
    b = jnp.cumsum(logf, axis=2)
    piv = b[:, :, CHUNK // 2 - 1:CHUNK // 2]

    causal = jnp.tril(jnp.ones((CHUNK, CHUNK), dtype=bool))
    a = jnp.einsum('bnchd,bnshd->bnhcs', q * jnp.exp(b - piv), k * jnp.exp(piv - b))
    a = jnp.where(causal, a, 0.0)
    o_intra = jnp.einsum('bnhcs,bnshv->bnchv', a, v)

    b_last = b[:, :, -1]
    upd = jnp.einsum('bnshd,bnshv->nbhdv', k * jnp.exp(b_last[:, :, None] - b), v)
    decay = jnp.exp(b_last).transpose(1, 0, 2, 3)

    def step(state, inp):
        dec, u = inp
        return dec[..., None] * state + u, state

    s0 = jnp.zeros((B, HGRN_HEADS, HGRN_DK, HGRN_DV), jnp.float32)
    _, s_before = lax.scan(step, s0, (decay, upd))
    o_inter = jnp.einsum('bnchd,nbhdv->bnchv', q * jnp.exp(b), s_before)

    o = (o_intra + o_inter).reshape(B, T, HGRN_HEADS, HGRN_DV)
    o = rms_norm(o, o_gain) * jax.nn.silu(g).reshape(B, T, HGRN_HEADS, HGRN_DV)
    return o.reshape(B, T, HGRN_IDIM).astype(h.dtype) @ w_out


def squared_relu_mlp(h, w1, w2):
    a = jax.nn.relu(h @ w1)
    return (a * a) @ w2


def setup_inputs(seed: int = 0) -> dict:
    key = jax.random.key(seed)
    ks = jax.random.split(key, 17)
    nrm = jax.random.normal
    f32 = jnp.float32
    x = nrm(ks[0], (BATCH, SEQ, D_MODEL), f32)
    c = nrm(ks[1], (BATCH, D_MODEL), f32)
    mod_w = nrm(ks[2], (DEPTH, D_MODEL, N_MOD * D_MODEL), f32) * (0.5 * D_MODEL ** -0.5)
    mod_b = nrm(ks[3], (DEPTH, N_MOD * D_MODEL), f32) * 0.02
    norm_mix = 1.0 + 0.05 * nrm(ks[4], (DEPTH, D_MODEL), f32)
    norm_mlp = 1.0 + 0.05 * nrm(ks[5], (DEPTH, D_MODEL), f32)
    attn_w_in = nrm(ks[6], (N_ATTN, D_MODEL, ATTN_IN_DIM), f32) * D_MODEL ** -0.5
    attn_w_out = nrm(ks[7], (N_ATTN, ATTN_Q_DIM, D_MODEL), f32) * ATTN_Q_DIM ** -0.5
    attn_q_gain = 1.0 + 0.05 * nrm(ks[8], (N_ATTN, ATTN_HEAD_DIM), f32)
    attn_k_gain = 1.0 + 0.05 * nrm(ks[9], (N_ATTN, ATTN_HEAD_DIM), f32)
    attn_sinks = nrm(ks[10], (N_ATTN, ATTN_HEADS), f32)
    hgrn_w_in = nrm(ks[11], (N_HGRN, D_MODEL, HGRN_IN_DIM), f32) * D_MODEL ** -0.5
    hgrn_w_out = nrm(ks[12], (N_HGRN, HGRN_IDIM, D_MODEL), f32) * HGRN_IDIM ** -0.5
    hgrn_o_gain = 1.0 + 0.05 * nrm(ks[13], (N_HGRN, HGRN_HEADS, HGRN_DV), f32)
    hgrn_lb_logits = 0.5 * nrm(ks[14], (DEPTH, HGRN_FDIM), f32)
    mlp_w1 = nrm(ks[15], (DEPTH, D_MODEL, D_FF), f32) * D_MODEL ** -0.5
    mlp_w2 = nrm(ks[16], (DEPTH, D_FF, D_MODEL), f32) * D_FF ** -0.5
    return {"x": x, "c": c, "mod_w": mod_w, "mod_b": mod_b,
            "norm_mix": norm_mix, "norm_mlp": norm_mlp,
            "attn_w_in": attn_w_in, "attn_w_out": attn_w_out,
            "attn_q_gain": attn_q_gain, "attn_k_gain": attn_k_gain, "attn_sinks": attn_sinks,
            "hgrn_w_in": hgrn_w_in, "hgrn_w_out": hgrn_w_out, "hgrn_o_gain": hgrn_o_gain,
            "hgrn_lb_logits": hgrn_lb_logits, "mlp_w1": mlp_w1, "mlp_w2": mlp_w2}


def reference(x, c, mod_w, mod_b, norm_mix, norm_mlp, attn_w_in, attn_w_out,
              attn_q_gain, attn_k_gain, attn_sinks, hgrn_w_in, hgrn_w_out, hgrn_o_gain,
              hgrn_lb_logits, mlp_w1, mlp_w2):
    lb_p = jax.nn.softmax(hgrn_lb_logits.astype(jnp.float32), axis=0)
    lower_bounds = jnp.cumsum(lb_p, axis=0) - lb_p[0]
    cond = jax.nn.silu(c)
    for i in range(DEPTH):
        mod = cond @ mod_w[i] + mod_b[i]
        sh1, sc1, g1, sh2, sc2, g2 = jnp.split(mod, N_MOD, axis=-1)
        h = modulate(rms_norm(x, norm_mix[i]), sh1, sc1)
        j = i // N_MIXERS
        if i % N_MIXERS == 0:
            y = swa_sink_attention(h, attn_w_in[j], attn_w_out[j], attn_q_gain[j],
                                   attn_k_gain[j], attn_sinks[j])
        else:
            y = hgrn2_mixer(h, hgrn_w_in[j], hgrn_w_out[j], hgrn_o_gain[j], lower_bounds[i])
        x = x + g1[:, None, :] * y
        h = modulate(rms_norm(x, norm_mlp[i]), sh2, sc2)
        x = x + g2[:, None, :] * squared_relu_mlp(h, mlp_w1[i], mlp_w2[i])
    return x
```

```python
import functools
import math

import numpy as np
import jax
import jax.numpy as jnp
from jax import lax
from jax.experimental import pallas as pl
from jax.experimental.pallas import tpu as pltpu

F32 = jnp.float32
BF16 = jnp.bfloat16

EPS = 1e-6
N_MOD = 6

HEAD_DIM = 64
GROUP = 8
WINDOW = 128
LANES = 128

HG_DK = 128
CHUNK = 64

VMEM_LIMIT = 56 * 1024 * 1024


def _cparams(*sem):
    return pltpu.CompilerParams(dimension_semantics=sem, vmem_limit_bytes=VMEM_LIMIT)


def _resident(shape):
    nd = len(shape)
    return pl.BlockSpec(shape, lambda *_: (0,) * nd, pipeline_mode=pl.Buffered(1))


def _silu(v):
    return v * jax.nn.sigmoid(v)


def _norm_mod(x, gain, shift, scale):
    ms = jnp.mean(x * x, axis=-1, keepdims=True)
    y = x * lax.rsqrt(ms + EPS) * gain
    return y * (1.0 + scale) + shift


def _mod_kernel(c_ref, w_ref, b_ref, o_ref):
    c = c_ref[...]
    cond = _silu(c)
    acc = jnp.sum(cond * w_ref[0], axis=0, keepdims=True)
    o_ref[0] = acc + b_ref[0]


def _mod_call(c, mod_w, mod_b, tn=1024):
    depth, d, n = mod_w.shape
    out = pl.pallas_call(
        _mod_kernel,
        grid=(depth, n // tn),
        in_specs=[
            pl.BlockSpec((d, 1), lambda l, j: (0, 0)),
            pl.BlockSpec((1, d, tn), lambda l, j: (l, 0, j)),
            pl.BlockSpec((1, 1, tn), lambda l, j: (l, 0, j)),
        ],
        out_specs=pl.BlockSpec((1, 1, tn), lambda l, j: (l, 0, j)),
        out_shape=jax.ShapeDtypeStruct((depth, 1, n), F32),
        compiler_params=_cparams("arbitrary", "arbitrary"),
        name="mod_proj",
    )(c.reshape(d, 1), mod_w, mod_b.reshape(depth, 1, n))
    return out.reshape(depth, N_MOD, d)


def _attn_inproj_kernel(x_ref, mod_ref, gain_ref, w_ref, qkg_ref, p_ref, pt_ref,
                        q_ref, k_ref, v_ref, *, q_dim, kv_dim, scale):
    h = _norm_mod(x_ref[...], gain_ref[...], mod_ref[0:1, :], mod_ref[1:2, :]).astype(BF16)
    proj = jnp.dot(h, w_ref[...], preferred_element_type=F32)
    qk = proj[:, :q_dim + kv_dim]
    ss = jnp.dot((qk * qk).astype(BF16), p_ref[...], preferred_element_type=F32)
    inv = lax.rsqrt(ss * (1.0 / HEAD_DIM) + EPS)
    inv_hi = inv.astype(BF16)
    inv_lo = (inv - inv_hi.astype(F32)).astype(BF16)
    inv_b = (jnp.dot(inv_hi, pt_ref[...], preferred_element_type=F32)
             + jnp.dot(inv_lo, pt_ref[...], preferred_element_type=F32))
    qkn = qk * inv_b * qkg_ref[...]
    q_ref[...] = (qkn[:, :q_dim] * scale).astype(BF16)
    k_ref[...] = qkn[:, q_dim:].astype(BF16)
    v_ref[...] = proj[:, q_dim + kv_dim:].astype(BF16)


def _attn_inproj_call(x, mod_l, gain, w, qk_gain, tm=512):
    t, d = x.shape
    n = w.shape[1]
    kv_dim = (n - d) // 2
    q_dim = d
    n_heads = (q_dim + kv_dim) // HEAD_DIM
    ind = (np.arange(q_dim + kv_dim)[:, None] // HEAD_DIM == np.arange(LANES)[None, :])
    p = jnp.asarray(ind, dtype=BF16)
    pt = jnp.asarray(ind.T, dtype=BF16)
    assert n_heads <= LANES
    kern = functools.partial(_attn_inproj_kernel, q_dim=q_dim, kv_dim=kv_dim,
                             scale=1.0 / math.sqrt(HEAD_DIM))
    return pl.pallas_call(
        kern,
        grid=(t // tm,),
        in_specs=[
            pl.BlockSpec((tm, d), lambda i: (i, 0)),
            _resident((N_MOD, d)),
            _resident((1, d)),
            _resident((d, n)),
            _resident((1, q_dim + kv_dim)),
            _resident((q_dim + kv_dim, LANES)),
            _resident((LANES, q_dim + kv_dim)),
        ],
        out_specs=[
            pl.BlockSpec((tm, q_dim), lambda i: (i, 0)),
            pl.BlockSpec((tm, kv_dim), lambda i: (i, 0)),
            pl.BlockSpec((tm, kv_dim), lambda i: (i, 0)),
        ],
        out_shape=[
            jax.ShapeDtypeStruct((t, q_dim), BF16),
            jax.ShapeDtypeStruct((t, kv_dim), BF16),
            jax.ShapeDtypeStruct((t, kv_dim), BF16),
        ],
        compiler_params=_cparams("arbitrary"),
        name="attn_inproj",
    )(x, mod_l, gain, w, qk_gain, p, pt)


def _attn_kernel(sink_ref, q_ref, kc_ref, kp_ref, vc_ref, vp_ref, o_ref, bias_ref, *, n_kv):
    n = pl.program_id(0)
    n_heads = n_kv * GROUP
    row = lax.broadcasted_iota(jnp.int32, (WINDOW, WINDOW), 0)
    col = lax.broadcasted_iota(jnp.int32, (WINDOW, WINDOW), 1)
    in_cur = col <= row

    @pl.when(n == 0)
    def _():
        dist = jnp.where(in_cur, row - col, row - col + WINDOW).astype(F32)
        for hq in range(n_heads):
            slope = 2.0 ** (-8.0 * (hq + 1) / n_heads)
            bias_ref[hq] = -slope * dist

    prev_pen = jnp.where(n == 0, -jnp.inf, 0.0).astype(F32)
    lane = lax.broadcasted_iota(jnp.int32, (WINDOW, LANES), 1)
    low = lane < HEAD_DIM

    for g in range(n_kv):
        pair = g // 2
        sl = slice(pair * LANES, (pair + 1) * LANES)
        zero = jnp.zeros((WINDOW, LANES), BF16)

        def halves(ref):
            t = ref[:, sl]
            r = pltpu.roll(t, HEAD_DIM, axis=1)
            if g % 2 == 0:
                return jnp.where(low, t, zero), jnp.where(low, zero, r)
            return jnp.where(low, r, zero), jnp.where(low, zero, t)

        kc_lo, kc_hi = halves(kc_ref)
        kp_lo, kp_hi = halves(kp_ref)
        vc_lo, vc_hi = halves(vc_ref)
        vp_lo, vp_hi = halves(vp_ref)
        k_rhs = jnp.concatenate([kc_lo, kc_hi, kp_lo, kp_hi], axis=0)
        v_rhs = jnp.concatenate([vc_lo, vp_lo, vc_hi, vp_hi], axis=0)

        n_pairs = GROUP // 2
        q_base = g * GROUP * HEAD_DIM
        qs = jnp.concatenate(
            [q_ref[:, q_base + t * LANES: q_base + (t + 1) * LANES] for t in range(n_pairs)], axis=0)
        logits = lax.dot_general(qs, k_rhs, (((1,), (1,)), ((), ())), preferred_element_type=F32)

        for t in range(n_pairs):
            rows = slice(t * WINDOW, (t + 1) * WINDOW)
            probs = []
            for e in range(2):
                hq = g * GROUP + 2 * t + e
                l_cur = logits[rows, e * WINDOW:(e + 1) * WINDOW]
                l_prev = logits[rows, (2 + e) * WINDOW:(3 + e) * WINDOW]
                s = jnp.where(in_cur, l_cur, l_prev + prev_pen) + bias_ref[hq]
                sink = sink_ref[0, hq]
                mx = jnp.maximum(jnp.max(s, axis=1, keepdims=True), sink)
                pe = jnp.exp(s - mx)
                den = jnp.sum(pe, axis=1, keepdims=True) + jnp.exp(sink - mx)
                pn = pe / den
                probs.append(jnp.where(in_cur, pn, 0.0).astype(BF16))
                probs.append(jnp.where(in_cur, 0.0, pn).astype(BF16))
            pm = jnp.concatenate(probs, axis=1)
            o_pair = jnp.dot(pm, v_rhs, preferred_element_type=F32)
            o_ref[:, q_base + t * LANES: q_base + (t + 1) * LANES] = o_pair.astype(BF16)


def _attn_call(q, k, v, sinks):
    t, q_dim = q.shape
    kv_dim = k.shape[1]
    n_kv = kv_dim // HEAD_DIM
    n_heads = n_kv * GROUP
    assert n_kv % 2 == 0 and q_dim == n_heads * HEAD_DIM
    nb = t // WINDOW
    cur = lambda n: (n, 0)
    prev = lambda n: (jnp.maximum(n - 1, 0), 0)
    return pl.pallas_call(
        functools.partial(_attn_kernel, n_kv=n_kv),
        grid=(nb,),
        in_specs=[
            pl.BlockSpec(memory_space=pltpu.SMEM),
            pl.BlockSpec((WINDOW, q_dim), cur),
            pl.BlockSpec((WINDOW, kv_dim), cur),
            pl.BlockSpec((WINDOW, kv_dim), prev),
            pl.BlockSpec((WINDOW, kv_dim), cur),
            pl.BlockSpec((WINDOW, kv_dim), prev),
        ],
        out_specs=pl.BlockSpec((WINDOW, q_dim), cur),
        out_shape=jax.ShapeDtypeStruct((t, q_dim), BF16),
        scratch_shapes=[pltpu.VMEM((n_heads, WINDOW, WINDOW), F32)],
        compiler_params=_cparams("arbitrary"),
        name="swa_attention",
    )(sinks.reshape(1, n_heads), q, k, k, v, v)


def _outproj_kernel(a_ref, x_ref, mod_ref, w_ref, o_ref, *, gate_row):
    y = jnp.dot(a_ref[...], w_ref[...], preferred_element_type=F32)
    o_ref[...] = x_ref[...] + mod_ref[gate_row:gate_row + 1, :] * y


def _outproj_call(a, x, mod_l, w, gate_row, tm=512):
    t, d = x.shape
    kdim = a.shape[1]
    return pl.pallas_call(
        functools.partial(_outproj_kernel, gate_row=gate_row),
        grid=(t // tm,),
        in_specs=[
            pl.BlockSpec((tm, kdim), lambda i: (i, 0)),
            pl.BlockSpec((tm, d), lambda i: (i, 0)),
            _resident((N_MOD, d)),
            _resident((kdim, d)),
        ],
        out_specs=pl.BlockSpec((tm, d), lambda i: (i, 0)),
        out_shape=jax.ShapeDtypeStruct((t, d), F32),
        compiler_params=_cparams("arbitrary"),
        name="outproj_residual",
    )(a, x, mod_l, w)


def _mlp_kernel(x_ref, mod_ref, gain_ref, w1_ref, w2_ref, o_ref, h_ref):
    f = pl.program_id(1)
    nf = pl.num_programs(1)

    @pl.when(f == 0)
    def _():
        h_ref[...] = _norm_mod(x_ref[...], gain_ref[...], mod_ref[3:4, :], mod_ref[4:5, :]).astype(BF16)

    a = jnp.dot(h_ref[...], w1_ref[...], preferred_element_type=F32)
    a = jnp.maximum(a, 0.0)
    part = jnp.dot((a * a).astype(BF16), w2_ref[...], preferred_element_type=F32)

    @pl.when(f == 0)
    def _():
        o_ref[...] = part

    @pl.when(jnp.logical_and(f > 0, f < nf - 1))
    def _():
        o_ref[...] += part

    @pl.when(f == nf - 1)
    def _():
        o_ref[...] = x_ref[...] + mod_ref[5:6, :] * (o_ref[...] + part)


def _mlp_call(x, mod_l, gain, w1, w2, tm=512, tf=1024):
    t, d = x.shape
    dff = w1.shape[1]
    assert dff // tf >= 2
    return pl.pallas_call(
        _mlp_kernel,
        grid=(t // tm, dff // tf),
        in_specs=[
            pl.BlockSpec((tm, d), lambda i, f: (i, 0)),
            _resident((N_MOD, d)),
            _resident((1, d)),
            pl.BlockSpec((d, tf), lambda i, f: (0, f)),
            pl.BlockSpec((tf, d), lambda i, f: (f, 0)),
        ],
        out_specs=pl.BlockSpec((tm, d), lambda i, f: (i, 0)),
        out_shape=jax.ShapeDtypeStruct((t, d), F32),
        scratch_shapes=[pltpu.VMEM((tm, d), BF16)],
        compiler_params=_cparams("arbitrary", "arbitrary"),
        name="relu2_mlp",
    )(x, mod_l, gain, w1, w2)


def _hgrn_inproj_kernel(x_ref, mod_ref, gain_ref, w_ref, q_ref, f_ref, i_ref, g_ref, h_ref, *, hpt):
    j = pl.program_id(1)

    @pl.when(j == 0)
    def _():
        h_ref[...] = _norm_mod(x_ref[...], gain_ref[...], mod_ref[0:1, :], mod_ref[1:2, :]).astype(BF16)

    res = jnp.dot(h_ref[...], w_ref[...], preferred_element_type=F32)
    width = hpt * HG_DK
    for s, ref in enumerate((q_ref, f_ref, i_ref, g_ref)):
        for hh in range(hpt):
            lo = s * width + hh * HG_DK
            ref[hh] = res[:, lo:lo + HG_DK].astype(ref.dtype)


def _hgrn_inproj_call(x, mod_l, gain, w_perm, n_heads, tm=1024, hpt=2):
    t, d = x.shape
    tn = 4 * hpt * HG_DK
    head_major = lambda dt: jax.ShapeDtypeStruct((n_heads, t, HG_DK), dt)
    ospec = pl.BlockSpec((hpt, tm, HG_DK), lambda i, j: (j, i, 0))
    return pl.pallas_call(
        functools.partial(_hgrn_inproj_kernel, hpt=hpt),
        grid=(t // tm, n_heads // hpt),
        in_specs=[
            pl.BlockSpec((tm, d), lambda i, j: (i, 0)),
            _resident((N_MOD, d)),
            _resident((1, d)),
            pl.BlockSpec((d, tn), lambda i, j: (0, j)),
        ],
        out_specs=[ospec, ospec, ospec, ospec],
        out_shape=[head_major(BF16), head_major(F32), head_major(BF16), head_major(BF16)],
        scratch_shapes=[pltpu.VMEM((tm, d), BF16)],
        compiler_params=_cparams("arbitrary", "arbitrary"),
        name="hgrn_inproj",
    )(x, mod_l, gain, w_perm)


def _cum_matrices(tr):
    r = np.arange(tr)[:, None]
    c = np.arange(tr)[None, :]
    same = (r // CHUNK) == (c // CHUNK)
    incl = same & (c <= r)
    pivot = same & (c % CHUNK <= CHUNK // 2 - 1)
    after = same & (c > r)
    mats = np.concatenate([incl.astype(np.float32),
                           incl.astype(np.float32) - pivot.astype(np.float32),
                           after.astype(np.float32)], axis=0)
    return jnp.asarray(mats, dtype=BF16)


def _hgrn_kernel(lbl_ref, gain_ref, cm_ref, q_ref, f_ref, i_ref, g_ref, o_ref, st_ref, *, hb, tr, scale):
    tstep = pl.program_id(1)

    @pl.when(tstep == 0)
    def _():
        st_ref[...] = jnp.zeros_like(st_ref)

    l0 = lbl_ref[0]
    l1 = lbl_ref[1]
    lm = jnp.maximum(l0, l1)
    e0 = jnp.exp(l0 - lm)
    e1 = jnp.exp(l1 - lm)
    p0 = e0 / (e0 + e1)
    p1 = e1 / (e0 + e1)
    lb = (p0 + p1) - p0

    rowc = lax.broadcasted_iota(jnp.int32, (CHUNK, CHUNK), 0)
    colc = lax.broadcasted_iota(jnp.int32, (CHUNK, CHUNK), 1)
    causal = colc <= rowc
    n_chunks = tr // CHUNK

    for hh in range(hb):
        forget = lb[hh] + (1.0 - lb[hh]) * jax.nn.sigmoid(f_ref[hh])
        kk = 1.0 - forget
        logf = jnp.log(forget)
        p_hi = logf.astype(BF16)
        r1 = logf - p_hi.astype(F32)
        p_mid = r1.astype(BF16)
        p_lo = (r1 - p_mid.astype(F32)).astype(BF16)
        parts = jnp.concatenate([p_hi, p_mid, p_lo], axis=1)
        cums = jnp.dot(cm_ref[...], parts, preferred_element_type=F32)
        cums = (cums[:, 2 * HG_DK:] + cums[:, HG_DK:2 * HG_DK]) + cums[:, :HG_DK]
        b = cums[0:tr]
        bmp = cums[tr:2 * tr]
        blb = cums[2 * tr:3 * tr]

        qv = _silu(q_ref[hh].astype(F32)) * scale
        qe = (qv * jnp.exp(bmp)).astype(BF16)
        ke = (kk * jnp.exp(-bmp)).astype(BF16)
        qb = (qv * jnp.exp(b)).astype(BF16)
        ku = (kk * jnp.exp(blb)).astype(BF16)
        v = i_ref[hh]
        gate = _silu(g_ref[hh].astype(F32))

        for c in range(n_chunks):
            rs = slice(c * CHUNK, (c + 1) * CHUNK)
            a = lax.dot_general(qe[rs], ke[rs], (((1,), (1,)), ((), ())), preferred_element_type=F32)
            a = jnp.where(causal, a, 0.0).astype(BF16)
            st = st_ref[hh]
            o = (jnp.dot(a, v[rs], preferred_element_type=F32)
                 + lax.dot_general(qb[rs], st.astype(BF16), (((1,), (1,)), ((), ())),
                                   preferred_element_type=F32))
            upd = lax.dot_general(v[rs], ku[rs], (((0,), (0,)), ((), ())), preferred_element_type=F32)
            dec = jnp.exp(b[(c + 1) * CHUNK - 1:(c + 1) * CHUNK, :])
            st_ref[hh] = st * dec + upd
            on = o * lax.rsqrt(jnp.mean(o * o, axis=-1, keepdims=True) + EPS) * gain_ref[hh]
            o_ref[rs, hh * HG_DK:(hh + 1) * HG_DK] = (on * gate[rs]).astype(BF16)


def _hgrn_call(q, f, i, g, lb_logits, o_gain, hb=4, tr=256):
    n_heads, t, dk = q.shape
    depth = lb_logits.shape[0]
    assert depth == 2 and dk == HG_DK
    cm = _cum_matrices(tr)
    blk = pl.BlockSpec((hb, tr, dk), lambda h, s: (h, s, 0))
    return pl.pallas_call(
        functools.partial(_hgrn_kernel, hb=hb, tr=tr, scale=1.0 / math.sqrt(HG_DK)),
        grid=(n_heads // hb, t // tr),
        in_specs=[
            pl.BlockSpec((depth, hb, 1, dk), lambda h, s: (0, h, 0, 0)),
            pl.BlockSpec((hb, 1, dk), lambda h, s: (h, 0, 0)),
            _resident((3 * tr, tr)),
            blk, blk, blk, blk,
        ],
        out_specs=pl.BlockSpec((tr, hb * dk), lambda h, s: (s, h)),
        out_shape=jax.ShapeDtypeStruct((t, n_heads * dk), BF16),
        scratch_shapes=[pltpu.VMEM((hb, dk, dk), F32)],
        compiler_params=_cparams("arbitrary", "arbitrary"),
        name="hgrn_recurrence",
    )(lb_logits.reshape(depth, n_heads, 1, dk), o_gain.reshape(n_heads, 1, dk), cm, q, f, i, g)


def kernel(x, c, mod_w, mod_b, norm_mix, norm_mlp, attn_w_in, attn_w_out, attn_q_gain, attn_k_gain,
           attn_sinks, hgrn_w_in, hgrn_w_out, hgrn_o_gain, hgrn_lb_logits, mlp_w1, mlp_w2):
    b, t, d = x.shape
    assert b == 1 and mod_w.shape[0] == 2
    xs = x.reshape(t, d)

    mod = _mod_call(c, mod_w, mod_b)

    w_in = attn_w_in[0].astype(BF16)
    q_dim = attn_w_out.shape[1]
    kv_dim = (w_in.shape[1] - q_dim) // 2
    qk_gain = jnp.concatenate([jnp.tile(attn_q_gain[0], q_dim // HEAD_DIM),
                               jnp.tile(attn_k_gain[0], kv_dim // HEAD_DIM)]).reshape(1, q_dim + kv_dim)
    q, k, v = _attn_inproj_call(xs, mod[0], norm_mix[0].reshape(1, d), w_in, qk_gain)
    att = _attn_call(q, k, v, attn_sinks[0])
    xs = _outproj_call(att, xs, mod[0], attn_w_out[0].astype(BF16), gate_row=2)
    xs = _mlp_call(xs, mod[0], norm_mlp[0].reshape(1, d), mlp_w1[0].astype(BF16), mlp_w2[0].astype(BF16))

    n_heads = hgrn_o_gain.shape[1]
    hpt = 2
    w = hgrn_w_in[0].reshape(d, 4, n_heads // hpt, hpt * HG_DK)
    w_perm = jnp.transpose(w, (0, 2, 1, 3)).reshape(d, 4 * n_heads * HG_DK).astype(BF16)
    hq, hf, hi, hg = _hgrn_inproj_call(xs, mod[1], norm_mix[1].reshape(1, d), w_perm, n_heads, hpt=hpt)
    ho = _hgrn_call(hq, hf, hi, hg, hgrn_lb_logits, hgrn_o_gain[0])
    xs = _outproj_call(ho, xs, mod[1], hgrn_w_out[0].astype(BF16), gate_row=2)
    xs = _mlp_call(xs, mod[1], norm_mlp[1].reshape(1, d), mlp_w1[1].astype(BF16), mlp_w2[1].astype(BF16))
    return xs.reshape(b, t, d)
```

```python
import functools
import math

import numpy as np
import jax
import jax.numpy as jnp
from jax import lax
from jax.experimental import pallas as pl
from jax.experimental.pallas import tpu as pltpu

F32 = jnp.float32
BF16 = jnp.bfloat16

EPS = 1e-6
N_MOD = 6

HEAD_DIM = 64
GROUP = 8
WINDOW = 128
LANES = 128

HG_DK = 128
CHUNK = 64

VMEM_LIMIT = 56 * 1024 * 1024


def _cparams(*sem):
    return pltpu.CompilerParams(dimension_semantics=sem, vmem_limit_bytes=VMEM_LIMIT)


def _resident(shape):
    nd = len(shape)
    return pl.BlockSpec(shape, lambda *_: (0,) * nd, pipeline_mode=pl.Buffered(1))


def _sigmoid(v):
    return 0.5 * jnp.tanh(0.5 * v) + 0.5


def _silu(v):
    return v * _sigmoid(v)


def _norm_mod(x, gain, shift, scale):
    ms = jnp.mean(x * x, axis=-1, keepdims=True)
    y = x * lax.rsqrt(ms + EPS) * gain
    return y * (1.0 + scale) + shift


def _mod_kernel(c_ref, w_ref, b_ref, o_ref):
    c = c_ref[...]
    cond = _silu(c)
    acc = jnp.sum(cond * w_ref[0], axis=0, keepdims=True)
    o_ref[0] = acc + b_ref[0]


def _mod_call(c, mod_w, mod_b, tn=1024):
    depth, d, n = mod_w.shape
    out = pl.pallas_call(
        _mod_kernel,
        grid=(depth, n // tn),
        in_specs=[
            pl.BlockSpec((d, 1), lambda l, j: (0, 0)),
            pl.BlockSpec((1, d, tn), lambda l, j: (l, 0, j)),
            pl.BlockSpec((1, 1, tn), lambda l, j: (l, 0, j)),
        ],
        out_specs=pl.BlockSpec((1, 1, tn), lambda l, j: (l, 0, j)),
        out_shape=jax.ShapeDtypeStruct((depth, 1, n), F32),
        compiler_params=_cparams("arbitrary", "arbitrary"),
        name="mod_proj",
    )(c.reshape(d, 1), mod_w, mod_b.reshape(depth, 1, n))
    return out.reshape(depth, N_MOD, d)


def _attn_inproj_kernel(x_ref, mod_ref, gain_ref, w_ref, qkg_ref, p_ref, pt_ref,
                        q_ref, k_ref, v_ref, *, q_dim, kv_dim, scale):
    h = _norm_mod(x_ref[...], gain_ref[...], mod_ref[0:1, :], mod_ref[1:2, :]).astype(BF16)
    proj = jnp.dot(h, w_ref[...], preferred_element_type=F32)
    qk = proj[:, :q_dim + kv_dim]
    ss = jnp.dot((qk * qk).astype(BF16), p_ref[...], preferred_element_type=F32)
    inv = lax.rsqrt(ss * (1.0 / HEAD_DIM) + EPS)
    inv_hi = inv.astype(BF16)
    inv_lo = (inv - inv_hi.astype(F32)).astype(BF16)
    inv_b = (jnp.dot(inv_hi, pt_ref[...], preferred_element_type=F32)
             + jnp.dot(inv_lo, pt_ref[...], preferred_element_type=F32))
    qkn = qk * inv_b * qkg_ref[...]
    q_ref[...] = (qkn[:, :q_dim] * scale).astype(BF16)
    k_ref[...] = qkn[:, q_dim:].astype(BF16)
    v_ref[...] = proj[:, q_dim + kv_dim:].astype(BF16)


def _attn_inproj_call(x, mod_l, gain, w, qk_gain, tm=512):
    t, d = x.shape
    n = w.shape[1]
    kv_dim = (n - d) // 2
    q_dim = d
    n_heads = (q_dim + kv_dim) // HEAD_DIM
    ind = (np.arange(q_dim + kv_dim)[:, None] // HEAD_DIM == np.arange(LANES)[None, :])
    p = jnp.asarray(ind, dtype=BF16)
    pt = jnp.asarray(ind.T, dtype=BF16)
    assert n_heads <= LANES
    kern = functools.partial(_attn_inproj_kernel, q_dim=q_dim, kv_dim=kv_dim,
                             scale=1.0 / math.sqrt(HEAD_DIM))
    return pl.pallas_call(
        kern,
        grid=(t // tm,),
        in_specs=[
            pl.BlockSpec((tm, d), lambda i: (i, 0)),
            _resident((N_MOD, d)),
            _resident((1, d)),
            _resident((d, n)),
            _resident((1, q_dim + kv_dim)),
            _resident((q_dim + kv_dim, LANES)),
            _resident((LANES, q_dim + kv_dim)),
        ],
        out_specs=[
            pl.BlockSpec((tm, q_dim), lambda i: (i, 0)),
            pl.BlockSpec((tm, kv_dim), lambda i: (i, 0)),
            pl.BlockSpec((tm, kv_dim), lambda i: (i, 0)),
        ],
        out_shape=[
            jax.ShapeDtypeStruct((t, q_dim), BF16),
            jax.ShapeDtypeStruct((t, kv_dim), BF16),
            jax.ShapeDtypeStruct((t, kv_dim), BF16),
        ],
        compiler_params=_cparams("arbitrary"),
        name="attn_inproj",
    )(x, mod_l, gain, w, qk_gain, p, pt)


def _attn_kernel(sink_ref, q_ref, kc_ref, kp_ref, vc_ref, vp_ref, o_ref, bias_ref, *, n_kv):
    n = pl.program_id(0)
    n_heads = n_kv * GROUP
    row = lax.broadcasted_iota(jnp.int32, (WINDOW, WINDOW), 0)
    col = lax.broadcasted_iota(jnp.int32, (WINDOW, WINDOW), 1)
    in_cur = col <= row

    @pl.when(n == 0)
    def _():
        dist = jnp.where(in_cur, row - col, row - col + WINDOW).astype(F32)
        for hq in range(n_heads):
            slope = 2.0 ** (-8.0 * (hq + 1) / n_heads)
            bias_ref[hq] = -slope * dist

    prev_pen = jnp.where(n == 0, -jnp.inf, 0.0).astype(F32)
    lane = lax.broadcasted_iota(jnp.int32, (WINDOW, LANES), 1)
    low = lane < HEAD_DIM

    for g in range(n_kv):
        pair = g // 2
        sl = slice(pair * LANES, (pair + 1) * LANES)
        zero = jnp.zeros((WINDOW, LANES), BF16)

        def halves(ref):
            t = ref[:, sl]
            r = pltpu.roll(t, HEAD_DIM, axis=1)
            if g % 2 == 0:
                return jnp.where(low, t, zero), jnp.where(low, zero, r)
            return jnp.where(low, r, zero), jnp.where(low, zero, t)

        kc_lo, kc_hi = halves(kc_ref)
        kp_lo, kp_hi = halves(kp_ref)
        vc_lo, vc_hi = halves(vc_ref)
        vp_lo, vp_hi = halves(vp_ref)
        k_rhs = jnp.concatenate([kc_lo, kc_hi, kp_lo, kp_hi], axis=0)
        v_rhs = jnp.concatenate([vc_lo, vp_lo, vc_hi, vp_hi], axis=0)

        n_pairs = GROUP // 2
        q_base = g * GROUP * HEAD_DIM
        qs = jnp.concatenate(
            [q_ref[:, q_base + t * LANES: q_base + (t + 1) * LANES] for t in range(n_pairs)], axis=0)
        logits = lax.dot_general(qs, k_rhs, (((1,), (1,)), ((), ())), preferred_element_type=F32)

        for t in range(n_pairs):
            rows = slice(t * WINDOW, (t + 1) * WINDOW)
            probs = []
            for e in range(2):
                hq = g * GROUP + 2 * t + e
                l_cur = logits[rows, e * WINDOW:(e + 1) * WINDOW]
                l_prev = logits[rows, (2 + e) * WINDOW:(3 + e) * WINDOW]
                s = jnp.where(in_cur, l_cur, l_prev + prev_pen) + bias_ref[hq]
                sink = sink_ref[0, hq]
                mx = jnp.maximum(jnp.max(s, axis=1, keepdims=True), sink)
                pe = jnp.exp(s - mx)
                den = jnp.sum(pe, axis=1, keepdims=True) + jnp.exp(sink - mx)
                pn = pe / den
                probs.append(jnp.where(in_cur, pn, 0.0).astype(BF16))
                probs.append(jnp.where(in_cur, 0.0, pn).astype(BF16))
            pm = jnp.concatenate(probs, axis=1)
            o_pair = jnp.dot(pm, v_rhs, preferred_element_type=F32)
            o_ref[:, q_base + t * LANES: q_base + (t + 1) * LANES] = o_pair.astype(BF16)


def _attn_call(q, k, v, sinks):
    t, q_dim = q.shape
    kv_dim = k.shape[1]
    n_kv = kv_dim // HEAD_DIM
    n_heads = n_kv * GROUP
    assert n_kv % 2 == 0 and q_dim == n_heads * HEAD_DIM
    nb = t // WINDOW
    cur = lambda n: (n, 0)
    prev = lambda n: (jnp.maximum(n - 1, 0), 0)
    return pl.pallas_call(
        functools.partial(_attn_kernel, n_kv=n_kv),
        grid=(nb,),
        in_specs=[
            pl.BlockSpec(memory_space=pltpu.SMEM),
            pl.BlockSpec((WINDOW, q_dim), cur),
            pl.BlockSpec((WINDOW, kv_dim), cur),
            pl.BlockSpec((WINDOW, kv_dim), prev),
            pl.BlockSpec((WINDOW, kv_dim), cur),
            pl.BlockSpec((WINDOW, kv_dim), prev),
        ],
        out_specs=pl.BlockSpec((WINDOW, q_dim), cur),
        out_shape=jax.ShapeDtypeStruct((t, q_dim), BF16),
        scratch_shapes=[pltpu.VMEM((n_heads, WINDOW, WINDOW), F32)],
        compiler_params=_cparams("arbitrary"),
        name="swa_attention",
    )(sinks.reshape(1, n_heads), q, k, k, v, v)


def _outproj_kernel(a_ref, x_ref, mod_ref, w_ref, o_ref, *, gate_row):
    y = jnp.dot(a_ref[...], w_ref[...], preferred_element_type=F32)
    o_ref[...] = x_ref[...] + mod_ref[gate_row:gate_row + 1, :] * y


def _outproj_call(a, x, mod_l, w, gate_row, tm=512):
    t, d = x.shape
    kdim = a.shape[1]
    return pl.pallas_call(
        functools.partial(_outproj_kernel, gate_row=gate_row),
        grid=(t // tm,),
        in_specs=[
            pl.BlockSpec((tm, kdim), lambda i: (i, 0)),
            pl.BlockSpec((tm, d), lambda i: (i, 0)),
            _resident((N_MOD, d)),
            _resident((kdim, d)),
        ],
        out_specs=pl.BlockSpec((tm, d), lambda i: (i, 0)),
        out_shape=jax.ShapeDtypeStruct((t, d), F32),
        compiler_params=_cparams("arbitrary"),
        name="outproj_residual",
    )(a, x, mod_l, w)


def _mlp_kernel(x_ref, mod_ref, gain_ref, w1_ref, w2_ref, o_ref, h_ref):
    f = pl.program_id(1)
    nf = pl.num_programs(1)

    @pl.when(f == 0)
    def _():
        h_ref[...] = _norm_mod(x_ref[...], gain_ref[...], mod_ref[3:4, :], mod_ref[4:5, :]).astype(BF16)

    a = jnp.dot(h_ref[...], w1_ref[...], preferred_element_type=F32)
    a = jnp.maximum(a, 0.0)
    part = jnp.dot((a * a).astype(BF16), w2_ref[...], preferred_element_type=F32)

    @pl.when(f == 0)
    def _():
        o_ref[...] = part

    @pl.when(jnp.logical_and(f > 0, f < nf - 1))
    def _():
        o_ref[...] += part

    @pl.when(f == nf - 1)
    def _():
        o_ref[...] = x_ref[...] + mod_ref[5:6, :] * (o_ref[...] + part)


def _mlp_call(x, mod_l, gain, w1, w2, tm=512, tf=1024):
    t, d = x.shape
    dff = w1.shape[1]
    assert dff // tf >= 2
    return pl.pallas_call(
        _mlp_kernel,
        grid=(t // tm, dff // tf),
        in_specs=[
            pl.BlockSpec((tm, d), lambda i, f: (i, 0)),
            _resident((N_MOD, d)),
            _resident((1, d)),
            pl.BlockSpec((d, tf), lambda i, f: (0, f)),
            pl.BlockSpec((tf, d), lambda i, f: (f, 0)),
        ],
        out_specs=pl.BlockSpec((tm, d), lambda i, f: (i, 0)),
        out_shape=jax.ShapeDtypeStruct((t, d), F32),
        scratch_shapes=[pltpu.VMEM((tm, d), BF16)],
        compiler_params=_cparams("arbitrary", "arbitrary"),
        name="relu2_mlp",
    )(x, mod_l, gain, w1, w2)


def _layer_lower_bound(lbl_ref):
    l0 = lbl_ref[0]
    l1 = lbl_ref[1]
    lm = jnp.maximum(l0, l1)
    e0 = jnp.exp(l0 - lm)
    e1 = jnp.exp(l1 - lm)
    p0 = e0 / (e0 + e1)
    p1 = e1 / (e0 + e1)
    return (p0 + p1) - p0


def _hgrn_inproj_kernel(x_ref, mod_ref, gain_ref, lbl_ref, w_ref, q_ref, lf_ref, kk_ref, v_ref, g_ref, h_ref,
                        *, hpt, scale):
    j = pl.program_id(1)

    @pl.when(j == 0)
    def _():
        h_ref[...] = _norm_mod(x_ref[...], gain_ref[...], mod_ref[0:1, :], mod_ref[1:2, :]).astype(BF16)

    res = jnp.dot(h_ref[...], w_ref[...], preferred_element_type=F32)
    lb = _layer_lower_bound(lbl_ref)
    width = hpt * HG_DK
    for hh in range(hpt):
        seg = lambda s: res[:, s * width + hh * HG_DK: s * width + (hh + 1) * HG_DK]
        q_ref[hh] = (_silu(seg(0)) * scale).astype(BF16)
        forget = lb[hh] + (1.0 - lb[hh]) * _sigmoid(seg(1))
        kk_ref[hh] = 1.0 - forget
        lf_ref[hh] = jnp.log(forget)
        v_ref[hh] = seg(2).astype(BF16)
        g_ref[hh] = _silu(seg(3)).astype(BF16)


def _hgrn_inproj_call(x, mod_l, gain, lb_logits, w_perm, n_heads, tm=1024, hpt=2):
    t, d = x.shape
    depth = lb_logits.shape[0]
    assert depth == 2
    tn = 4 * hpt * HG_DK
    head_major = lambda dt: jax.ShapeDtypeStruct((n_heads, t, HG_DK), dt)
    ospec = pl.BlockSpec((hpt, tm, HG_DK), lambda i, j: (j, i, 0))
    return pl.pallas_call(
        functools.partial(_hgrn_inproj_kernel, hpt=hpt, scale=1.0 / math.sqrt(HG_DK)),
        grid=(t // tm, n_heads // hpt),
        in_specs=[
            pl.BlockSpec((tm, d), lambda i, j: (i, 0)),
            _resident((N_MOD, d)),
            _resident((1, d)),
            pl.BlockSpec((depth, hpt, 1, HG_DK), lambda i, j: (0, j, 0, 0)),
            pl.BlockSpec((d, tn), lambda i, j: (0, j)),
        ],
        out_specs=[ospec] * 5,
        out_shape=[head_major(BF16), head_major(F32), head_major(F32), head_major(BF16), head_major(BF16)],
        scratch_shapes=[pltpu.VMEM((tm, d), BF16)],
        compiler_params=_cparams("arbitrary", "arbitrary"),
        name="hgrn_inproj",
    )(x, mod_l, gain, lb_logits.reshape(depth, n_heads, 1, HG_DK), w_perm)


def _cum_matrix(tr):
    n_chunks = tr // CHUNK
    r = np.arange(tr)[:, None]
    c = np.arange(tr)[None, :]
    same = (r // CHUNK) == (c // CHUNK)
    incl = same & (c <= r)
    pivot = same & (c % CHUNK <= CHUNK // 2 - 1)
    rows = np.zeros((2 * n_chunks, tr), np.float32)
    for ch in range(n_chunks):
        rows[2 * ch, ch * CHUNK: ch * CHUNK + CHUNK // 2] = 1.0
        rows[2 * ch + 1, ch * CHUNK: (ch + 1) * CHUNK] = 1.0
    mats = np.concatenate([incl.astype(np.float32) - pivot.astype(np.float32), rows], axis=0)
    return jnp.asarray(mats, dtype=BF16)


def _hgrn_kernel(gain_ref, cm_ref, q_ref, lf_ref, kk_ref, v_ref, g_ref, o_ref, st_ref, *, hb, tr):
    tstep = pl.program_id(1)

    @pl.when(tstep == 0)
    def _():
        st_ref[...] = jnp.zeros_like(st_ref)

    rowc = lax.broadcasted_iota(jnp.int32, (CHUNK, 2 * CHUNK), 0)
    colc = lax.broadcasted_iota(jnp.int32, (CHUNK, 2 * CHUNK), 1)
    causal2 = (colc % CHUNK) <= rowc
    n_chunks = tr // CHUNK
    zc = jnp.zeros((CHUNK, HG_DK), BF16)
    zs = jnp.zeros((HG_DK, HG_DK), BF16)

    def block_diag(a, b, z):
        return jnp.concatenate([jnp.concatenate([a, z], axis=1), jnp.concatenate([z, b], axis=1)], axis=0)

    for pp in range(hb // 2):
        heads = (2 * pp, 2 * pp + 1)
        qe, ke, qb, ku, dec = [], [], [], [], []
        for hh in heads:
            logf = lf_ref[hh]
            p_hi = logf.astype(BF16)
            p_lo = (logf - p_hi.astype(F32)).astype(BF16)
            cums = jnp.dot(cm_ref[...], jnp.concatenate([p_hi, p_lo], axis=1), preferred_element_type=F32)
            cums = cums[:, :HG_DK] + cums[:, HG_DK:]
            bmp = cums[:tr]
            qe_f = q_ref[hh].astype(F32) * jnp.exp(bmp)
            ke_f = kk_ref[hh] * jnp.exp(-bmp)
            qb_h, ku_h, dec_h = [], [], []
            for c in range(n_chunks):
                rs = slice(c * CHUNK, (c + 1) * CHUNK)
                piv = cums[tr + 2 * c: tr + 2 * c + 1]
                blast = cums[tr + 2 * c + 1: tr + 2 * c + 2]
                qb_h.append((qe_f[rs] * jnp.exp(piv)).astype(BF16))
                ku_h.append((ke_f[rs] * jnp.exp(blast - piv)).astype(BF16))
                dec_h.append(jnp.exp(blast))
            qe.append(qe_f.astype(BF16))
            ke.append(ke_f.astype(BF16))
            qb.append(qb_h)
            ku.append(ku_h)
            dec.append(dec_h)

        for c in range(n_chunks):
            rs = slice(c * CHUNK, (c + 1) * CHUNK)
            v0 = v_ref[heads[0], rs, :]
            v1 = v_ref[heads[1], rs, :]
            a = lax.dot_general(jnp.concatenate([qe[0][rs], qe[1][rs]], axis=1),
                                block_diag(ke[0][rs], ke[1][rs], zc),
                                (((1,), (1,)), ((), ())), preferred_element_type=F32)
            a = jnp.where(causal2, a, 0.0).astype(BF16)
            st = st_ref[pp]
            st_b = st.astype(BF16)
            o = (jnp.dot(a, block_diag(v0, v1, zc), preferred_element_type=F32)
                 + lax.dot_general(jnp.concatenate([qb[0][c], qb[1][c]], axis=1),
                                   block_diag(st_b[:, :HG_DK], st_b[:, HG_DK:], zs),
                                   (((1,), (1,)), ((), ())), preferred_element_type=F32))
            upd = lax.dot_general(jnp.concatenate([v0, v1], axis=0), block_diag(ku[0][c], ku[1][c], zc),
                                  (((0,), (0,)), ((), ())), preferred_element_type=F32)
            st_ref[pp] = st * jnp.concatenate([dec[0][c], dec[1][c]], axis=1) + upd
            for e, hh in enumerate(heads):
                oh = o[:, e * HG_DK:(e + 1) * HG_DK]
                on = oh * lax.rsqrt(jnp.mean(oh * oh, axis=-1, keepdims=True) + EPS) * gain_ref[hh]
                o_ref[rs, hh * HG_DK:(hh + 1) * HG_DK] = (on * g_ref[hh, rs, :].astype(F32)).astype(BF16)


def _hgrn_call(q, lf, kk, v, g, o_gain, hb=8, tr=256):
    n_heads, t, dk = q.shape
    assert dk == HG_DK and hb % 2 == 0
    cm = _cum_matrix(tr)
    blk = pl.BlockSpec((hb, tr, dk), lambda h, s: (h, s, 0))
    return pl.pallas_call(
        functools.partial(_hgrn_kernel, hb=hb, tr=tr),
        grid=(n_heads // hb, t // tr),
        in_specs=[
            pl.BlockSpec((hb, 1, dk), lambda h, s: (h, 0, 0)),
            _resident(cm.shape),
            blk, blk, blk, blk, blk,
        ],
        out_specs=pl.BlockSpec((tr, hb * dk), lambda h, s: (s, h)),
        out_shape=jax.ShapeDtypeStruct((t, n_heads * dk), BF16),
        scratch_shapes=[pltpu.VMEM((hb // 2, dk, 2 * dk), F32)],
        compiler_params=_cparams("arbitrary", "arbitrary"),
        name="hgrn_recurrence",
    )(o_gain.reshape(n_heads, 1, dk), cm, q, lf, kk, v, g)


def kernel(x, c, mod_w, mod_b, norm_mix, norm_mlp, attn_w_in, attn_w_out, attn_q_gain, attn_k_gain,
           attn_sinks, hgrn_w_in, hgrn_w_out, hgrn_o_gain, hgrn_lb_logits, mlp_w1, mlp_w2):
    b, t, d = x.shape
    assert b == 1 and mod_w.shape[0] == 2
    xs = x.reshape(t, d)

    mod = _mod_call(c, mod_w, mod_b)

    w_in = attn_w_in[0].astype(BF16)
    q_dim = attn_w_out.shape[1]
    kv_dim = (w_in.shape[1] - q_dim) // 2
    qk_gain = jnp.concatenate([jnp.tile(attn_q_gain[0], q_dim // HEAD_DIM),
                               jnp.tile(attn_k_gain[0], kv_dim // HEAD_DIM)]).reshape(1, q_dim + kv_dim)
    q, k, v = _attn_inproj_call(xs, mod[0], norm_mix[0].reshape(1, d), w_in, qk_gain)
    att = _attn_call(q, k, v, attn_sinks[0])
    xs = _outproj_call(att, xs, mod[0], attn_w_out[0].astype(BF16), gate_row=2)
    xs = _mlp_call(xs, mod[0], norm_mlp[0].reshape(1, d), mlp_w1[0].astype(BF16), mlp_w2[0].astype(BF16))

    n_heads = hgrn_o_gain.shape[1]
    hpt = 2
    w = hgrn_w_in[0].reshape(d, 4, n_heads // hpt, hpt * HG_DK)
    w_perm = jnp.transpose(w, (0, 2, 1, 3)).reshape(d, 4 * n_heads * HG_DK).astype(BF16)
    hq, hlf, hkk, hv, hg = _hgrn_inproj_call(xs, mod[1], norm_mix[1].reshape(1, d), hgrn_lb_logits, w_perm,
                                             n_heads, hpt=hpt)
    ho = _hgrn_call(hq, hlf, hkk, hv, hg, hgrn_o_gain[0])
    xs = _outproj_call(ho, xs, mod[1], hgrn_w_out[0].astype(BF16), gate_row=2)
    xs = _mlp_call(xs, mod[1], norm_mlp[1].reshape(1, d), mlp_w1[1].astype(BF16), mlp_w2[1].astype(BF16))
    return xs.reshape(b, t, d)
```

```python
import functools
import math

import numpy as np
import jax
import jax.numpy as jnp
from jax import lax
from jax.experimental import pallas as pl
from jax.experimental.pallas import tpu as pltpu

F32 = jnp.float32
BF16 = jnp.bfloat16

EPS = 1e-6
N_MOD = 6

HEAD_DIM = 64
GROUP = 8
WINDOW = 128
LANES = 128

HG_DK = 128
CHUNK = 64

VMEM_LIMIT = 56 * 1024 * 1024


def _cparams(*sem):
    return pltpu.CompilerParams(dimension_semantics=sem, vmem_limit_bytes=VMEM_LIMIT)


def _resident(shape):
    nd = len(shape)
    return pl.BlockSpec(shape, lambda *_: (0,) * nd, pipeline_mode=pl.Buffered(1))


def _cast_job(w, layer, n_steps, step_of):
    _, r, c = w.shape
    rb = r // n_steps
    assert rb * n_steps == r and rb % 16 == 0
    in_spec = pl.BlockSpec((None, rb, c), lambda *g: (layer, step_of(*g), 0))
    out_spec = pl.BlockSpec((rb, c), lambda *g: (step_of(*g), 0))
    return in_spec, out_spec, jax.ShapeDtypeStruct((r, c), BF16)


def _sigmoid(v):
    return 0.5 * jnp.tanh(0.5 * v) + 0.5


def _silu(v):
    return v * _sigmoid(v)


def _norm_mod(x, gain, shift, scale):
    ms = jnp.mean(x * x, axis=-1, keepdims=True)
    y = x * lax.rsqrt(ms + EPS) * gain
    return y * (1.0 + scale) + shift


def _mod_kernel(c_ref, w_ref, b_ref, o_ref):
    c = c_ref[...]
    cond = _silu(c)
    acc = jnp.sum(cond * w_ref[0], axis=0, keepdims=True)
    o_ref[0] = acc + b_ref[0]


def _mod_call(c, mod_w, mod_b, tn=1024):
    depth, d, n = mod_w.shape
    out = pl.pallas_call(
        _mod_kernel,
        grid=(depth, n // tn),
        in_specs=[
            pl.BlockSpec((d, 1), lambda l, j: (0, 0)),
            pl.BlockSpec((1, d, tn), lambda l, j: (l, 0, j)),
            pl.BlockSpec((1, 1, tn), lambda l, j: (l, 0, j)),
        ],
        out_specs=pl.BlockSpec((1, 1, tn), lambda l, j: (l, 0, j)),
        out_shape=jax.ShapeDtypeStruct((depth, 1, n), F32),
        compiler_params=_cparams("arbitrary", "arbitrary"),
        name="mod_proj",
    )(c.reshape(d, 1), mod_w, mod_b.reshape(depth, 1, n))
    return out.reshape(depth, N_MOD, d)


def _attn_inproj_kernel(x_ref, mod_ref, gain_ref, w_ref, qkg_ref, p_ref, pt_ref,
                        q_ref, k_ref, v_ref, wb_ref, *, q_dim, kv_dim, scale, n_sub):
    @pl.when(pl.program_id(0) == 0)
    def _():
        wb_ref[...] = w_ref[...].astype(BF16)

    sub = x_ref.shape[0] // n_sub
    for s in range(n_sub):
        rs = slice(s * sub, (s + 1) * sub)
        h = _norm_mod(x_ref[rs, :], gain_ref[...], mod_ref[0:1, :], mod_ref[1:2, :]).astype(BF16)
        proj = jnp.dot(h, wb_ref[...], preferred_element_type=F32)
        qk = proj[:, :q_dim + kv_dim]
        ss = jnp.dot((qk * qk).astype(BF16), p_ref[...], preferred_element_type=F32)
        inv = lax.rsqrt(ss * (1.0 / HEAD_DIM) + EPS)
        inv_hi = inv.astype(BF16)
        inv_lo = (inv - inv_hi.astype(F32)).astype(BF16)
        inv_b = jnp.dot(jnp.concatenate([inv_hi, inv_lo], axis=1), pt_ref[...],
                        preferred_element_type=F32)
        qkn = qk * inv_b * qkg_ref[...]
        q_ref[rs, :] = (qkn[:, :q_dim] * scale).astype(BF16)
        k_ref[rs, :] = qkn[:, q_dim:].astype(BF16)
        v_ref[rs, :] = proj[:, q_dim + kv_dim:].astype(BF16)


def _attn_inproj_call(x, mod_l, gain, w_all, qk_gain, tm=512, n_sub=2):
    t, d = x.shape
    n = w_all.shape[2]
    kv_dim = (n - d) // 2
    q_dim = d
    n_heads = (q_dim + kv_dim) // HEAD_DIM
    ind = (np.arange(q_dim + kv_dim)[:, None] // HEAD_DIM == np.arange(LANES)[None, :])
    p = jnp.asarray(ind, dtype=BF16)
    pt = jnp.asarray(np.concatenate([ind.T, ind.T], axis=0), dtype=BF16)
    assert n_heads <= LANES
    kern = functools.partial(_attn_inproj_kernel, q_dim=q_dim, kv_dim=kv_dim,
                             scale=1.0 / math.sqrt(HEAD_DIM), n_sub=n_sub)
    return pl.pallas_call(
        kern,
        grid=(t // tm,),
        in_specs=[
            pl.BlockSpec((tm, d), lambda i: (i, 0)),
            _resident((N_MOD, d)),
            _resident((1, d)),
            pl.BlockSpec((None, d, n), lambda i: (0, 0, 0), pipeline_mode=pl.Buffered(1)),
            _resident((1, q_dim + kv_dim)),
            _resident((q_dim + kv_dim, LANES)),
            _resident((2 * LANES, q_dim + kv_dim)),
        ],
        out_specs=[
            pl.BlockSpec((tm, q_dim), lambda i: (i, 0)),
            pl.BlockSpec((tm, kv_dim), lambda i: (i, 0)),
            pl.BlockSpec((tm, kv_dim), lambda i: (i, 0)),
        ],
        out_shape=[
            jax.ShapeDtypeStruct((t, q_dim), BF16),
            jax.ShapeDtypeStruct((t, kv_dim), BF16),
            jax.ShapeDtypeStruct((t, kv_dim), BF16),
        ],
        scratch_shapes=[pltpu.VMEM((d, n), BF16)],
        compiler_params=_cparams("arbitrary"),
        name="attn_inproj",
    )(x, mod_l, gain, w_all, qk_gain, p, pt)


def _attn_kernel(sink_ref, q_ref, kc_ref, kp_ref, vc_ref, vp_ref, w1_ref, w2_ref,
                 o_ref, w1b_ref, w2b_ref, bias_ref, *, n_kv):
    n = pl.program_id(0)
    w1b_ref[...] = w1_ref[...].astype(BF16)
    w2b_ref[...] = w2_ref[...].astype(BF16)
    n_heads = n_kv * GROUP
    row = lax.broadcasted_iota(jnp.int32, (WINDOW, WINDOW), 0)
    col = lax.broadcasted_iota(jnp.int32, (WINDOW, WINDOW), 1)
    in_cur = col <= row

    @pl.when(n == 0)
    def _():
        dist = jnp.where(in_cur, row - col, row - col + WINDOW).astype(F32)
        for hq in range(n_heads):
            slope = 2.0 ** (-8.0 * (hq + 1) / n_heads)
            bias_ref[hq] = -slope * dist

    prev_pen = jnp.where(n == 0, -jnp.inf, 0.0).astype(F32)
    lane = lax.broadcasted_iota(jnp.int32, (WINDOW, LANES), 1)
    low = lane < HEAD_DIM

    for g in range(n_kv):
        pair = g // 2
        sl = slice(pair * LANES, (pair + 1) * LANES)
        zero = jnp.zeros((WINDOW, LANES), BF16)

        def halves(ref):
            t = ref[:, sl]
            r = pltpu.roll(t, HEAD_DIM, axis=1)
            if g % 2 == 0:
                return jnp.where(low, t, zero), jnp.where(low, zero, r)
            return jnp.where(low, r, zero), jnp.where(low, zero, t)

        kc_lo, kc_hi = halves(kc_ref)
        kp_lo, kp_hi = halves(kp_ref)
        vc_lo, vc_hi = halves(vc_ref)
        vp_lo, vp_hi = halves(vp_ref)
        k_rhs = jnp.concatenate([kc_lo, kc_hi, kp_lo, kp_hi], axis=0)
        v_rhs = jnp.concatenate([vc_lo, vp_lo, vc_hi, vp_hi], axis=0)

        n_pairs = GROUP // 2
        q_base = g * GROUP * HEAD_DIM
        qs = jnp.concatenate(
            [q_ref[:, q_base + t * LANES: q_base + (t + 1) * LANES] for t in range(n_pairs)], axis=0)
        logits = lax.dot_general(qs, k_rhs, (((1,), (1,)), ((), ())), preferred_element_type=F32)

        for t in range(n_pairs):
            rows = slice(t * WINDOW, (t + 1) * WINDOW)
            probs = []
            for e in range(2):
                hq = g * GROUP + 2 * t + e
                l_cur = logits[rows, e * WINDOW:(e + 1) * WINDOW]
                l_prev = logits[rows, (2 + e) * WINDOW:(3 + e) * WINDOW]
                s = jnp.where(in_cur, l_cur, l_prev + prev_pen) + bias_ref[hq]
                sink = sink_ref[0, hq]
                mx = jnp.maximum(jnp.max(s, axis=1, keepdims=True), sink)
                pe = jnp.exp(s - mx)
                den = jnp.sum(pe, axis=1, keepdims=True) + jnp.exp(sink - mx)
                pn = pe / den
                probs.append(jnp.where(in_cur, pn, 0.0).astype(BF16))
                probs.append(jnp.where(in_cur, 0.0, pn).astype(BF16))
            pm = jnp.concatenate(probs, axis=1)
            o_pair = jnp.dot(pm, v_rhs, preferred_element_type=F32)
            o_ref[:, q_base + t * LANES: q_base + (t + 1) * LANES] = o_pair.astype(BF16)


def _attn_call(q, k, v, sinks, mlp_w1, mlp_w2, layer):
    t, q_dim = q.shape
    kv_dim = k.shape[1]
    n_kv = kv_dim // HEAD_DIM
    n_heads = n_kv * GROUP
    assert n_kv % 2 == 0 and q_dim == n_heads * HEAD_DIM
    nb = t // WINDOW
    cur = lambda n: (n, 0)
    prev = lambda n: (jnp.maximum(n - 1, 0), 0)
    w1_in, w1_out, w1_shape = _cast_job(mlp_w1, layer, nb, lambda n: n)
    w2_in, w2_out, w2_shape = _cast_job(mlp_w2, layer, nb, lambda n: n)
    return pl.pallas_call(
        functools.partial(_attn_kernel, n_kv=n_kv),
        grid=(nb,),
        in_specs=[
            pl.BlockSpec(memory_space=pltpu.SMEM),
            pl.BlockSpec((WINDOW, q_dim), cur),
            pl.BlockSpec((WINDOW, kv_dim), cur),
            pl.BlockSpec((WINDOW, kv_dim), prev),
            pl.BlockSpec((WINDOW, kv_dim), cur),
            pl.BlockSpec((WINDOW, kv_dim), prev),
            w1_in, w2_in,
        ],
        out_specs=[pl.BlockSpec((WINDOW, q_dim), cur), w1_out, w2_out],
        out_shape=[jax.ShapeDtypeStruct((t, q_dim), BF16), w1_shape, w2_shape],
        scratch_shapes=[pltpu.VMEM((n_heads, WINDOW, WINDOW), F32)],
        compiler_params=_cparams("arbitrary"),
        name="swa_attention",
    )(sinks.reshape(1, n_heads), q, k, k, v, v, mlp_w1, mlp_w2)


def _outproj_kernel(a_ref, x_ref, mod_ref, w_ref, o_ref, wb_ref, *, gate_row):
    @pl.when(pl.program_id(0) == 0)
    def _():
        wb_ref[...] = w_ref[...].astype(BF16)

    y = jnp.dot(a_ref[...], wb_ref[...], preferred_element_type=F32)
    o_ref[...] = x_ref[...] + mod_ref[gate_row:gate_row + 1, :] * y


def _outproj_call(a, x, mod_l, w_all, gate_row, tm=512):
    t, d = x.shape
    kdim = a.shape[1]
    return pl.pallas_call(
        functools.partial(_outproj_kernel, gate_row=gate_row),
        grid=(t // tm,),
        in_specs=[
            pl.BlockSpec((tm, kdim), lambda i: (i, 0)),
            pl.BlockSpec((tm, d), lambda i: (i, 0)),
            _resident((N_MOD, d)),
            pl.BlockSpec((None, kdim, d), lambda i: (0, 0, 0), pipeline_mode=pl.Buffered(1)),
        ],
        out_specs=pl.BlockSpec((tm, d), lambda i: (i, 0)),
        out_shape=jax.ShapeDtypeStruct((t, d), F32),
        scratch_shapes=[pltpu.VMEM((kdim, d), BF16)],
        compiler_params=_cparams("arbitrary"),
        name="outproj_residual",
    )(a, x, mod_l, w_all)


def _mlp_kernel(*refs, has_side):
    if has_side:
        x_ref, mod_ref, gain_ref, w1_ref, w2_ref, side_ref, o_ref, sideb_ref, h_ref = refs
        sideb_ref[...] = side_ref[...].astype(BF16)
    else:
        x_ref, mod_ref, gain_ref, w1_ref, w2_ref, o_ref, h_ref = refs
    f = pl.program_id(1)
    nf = pl.num_programs(1)

    @pl.when(f == 0)
    def _():
        h_ref[...] = _norm_mod(x_ref[...], gain_ref[...], mod_ref[3:4, :], mod_ref[4:5, :]).astype(BF16)

    a = jnp.dot(h_ref[...], w1_ref[...], preferred_element_type=F32)
    a = jnp.maximum(a, 0.0)
    part = jnp.dot((a * a).astype(BF16), w2_ref[...], preferred_element_type=F32)

    @pl.when(f == 0)
    def _():
        o_ref[...] = part

    @pl.when(jnp.logical_and(f > 0, f < nf - 1))
    def _():
        o_ref[...] += part

    @pl.when(f == nf - 1)
    def _():
        o_ref[...] = x_ref[...] + mod_ref[5:6, :] * (o_ref[...] + part)


def _mlp_call(x, mod_l, gain, w1, w2, side=None, tm=512, tf=1024):
    t, d = x.shape
    dff = w1.shape[1]
    nf = dff // tf
    assert nf >= 2
    in_specs = [
        pl.BlockSpec((tm, d), lambda i, f: (i, 0)),
        _resident((N_MOD, d)),
        _resident((1, d)),
        pl.BlockSpec((d, tf), lambda i, f: (0, f)),
        pl.BlockSpec((tf, d), lambda i, f: (f, 0)),
    ]
    out_specs = [pl.BlockSpec((tm, d), lambda i, f: (i, 0))]
    out_shape = [jax.ShapeDtypeStruct((t, d), F32)]
    args = [x, mod_l, gain, w1, w2]
    if side is not None:
        arr, in_fn, out_fn, shape = side
        in_specs.append(in_fn(t // tm, nf))
        out_specs.append(out_fn(t // tm, nf))
        out_shape.append(shape)
        args.append(arr)
    outs = pl.pallas_call(
        functools.partial(_mlp_kernel, has_side=side is not None),
        grid=(t // tm, nf),
        in_specs=in_specs,
        out_specs=out_specs,
        out_shape=out_shape,
        scratch_shapes=[pltpu.VMEM((tm, d), BF16)],
        compiler_params=_cparams("arbitrary", "arbitrary"),
        name="relu2_mlp",
    )(*args)
    return outs if side is not None else outs[0]


def _hgrn_w_in_side_job(hgrn_w_in, layer, n_heads, hpt):
    _, d, n = hgrn_w_in.shape
    cw = hpt * HG_DK
    n_tiles = n_heads // hpt
    n_cb = 4 * n_tiles

    def blocks(n_outer, n_inner):
        n_rb = (n_outer * n_inner) // n_cb
        assert n_rb * n_cb == n_outer * n_inner and d % n_rb == 0 and (d // n_rb) % 16 == 0
        return d // n_rb

    def in_fn(n_outer, n_inner):
        rb = blocks(n_outer, n_inner)

        def imap(i, f):
            step = i * n_inner + f
            out_cb = step % n_cb
            return (layer, step // n_cb, (out_cb % 4) * n_tiles + out_cb // 4)
        return pl.BlockSpec((None, rb, cw), imap)

    def out_fn(n_outer, n_inner):
        rb = blocks(n_outer, n_inner)

        def omap(i, f):
            step = i * n_inner + f
            return (step // n_cb, step % n_cb)
        return pl.BlockSpec((rb, cw), omap)

    return hgrn_w_in, in_fn, out_fn, jax.ShapeDtypeStruct((d, n), BF16)


def _layer_lower_bound(lbl_ref):
    l0 = lbl_ref[0]
    l1 = lbl_ref[1]
    lm = jnp.maximum(l0, l1)
    e0 = jnp.exp(l0 - lm)
    e1 = jnp.exp(l1 - lm)
    p0 = e0 / (e0 + e1)
    p1 = e1 / (e0 + e1)
    return (p0 + p1) - p0


def _hgrn_inproj_kernel(x_ref, mod_ref, gain_ref, lbl_ref, w_ref, side_ref,
                        q_ref, lf_ref, kk_ref, v_ref, g_ref, sideb_ref, h_ref, *, hpt, scale):
    j = pl.program_id(1)
    sideb_ref[...] = side_ref[...].astype(BF16)

    @pl.when(j == 0)
    def _():
        h_ref[...] = _norm_mod(x_ref[...], gain_ref[...], mod_ref[0:1, :], mod_ref[1:2, :]).astype(BF16)

    res = jnp.dot(h_ref[...], w_ref[...], preferred_element_type=F32)
    lb = _layer_lower_bound(lbl_ref)
    width = hpt * HG_DK
    for hh in range(hpt):
        seg = lambda s: res[:, s * width + hh * HG_DK: s * width + (hh + 1) * HG_DK]
        q_ref[hh] = (_silu(seg(0)) * scale).astype(BF16)
        forget = lb[hh] + (1.0 - lb[hh]) * _sigmoid(seg(1))
        kk_ref[hh] = 1.0 - forget
        lf_ref[hh] = jnp.log(forget)
        v_ref[hh] = seg(2).astype(BF16)
        g_ref[hh] = _silu(seg(3)).astype(BF16)


def _hgrn_inproj_call(x, mod_l, gain, lb_logits, w_perm, n_heads, side_w, side_layer, tm=1024, hpt=2):
    t, d = x.shape
    depth = lb_logits.shape[0]
    assert depth == 2
    tn = 4 * hpt * HG_DK
    n_inner = n_heads // hpt
    head_major = lambda dt: jax.ShapeDtypeStruct((n_heads, t, HG_DK), dt)
    ospec = pl.BlockSpec((hpt, tm, HG_DK), lambda i, j: (j, i, 0))
    s_in, s_out, s_shape = _cast_job(side_w, side_layer, (t // tm) * n_inner, lambda i, j: i * n_inner + j)
    return pl.pallas_call(
        functools.partial(_hgrn_inproj_kernel, hpt=hpt, scale=1.0 / math.sqrt(HG_DK)),
        grid=(t // tm, n_inner),
        in_specs=[
            pl.BlockSpec((tm, d), lambda i, j: (i, 0)),
            _resident((N_MOD, d)),
            _resident((1, d)),
            pl.BlockSpec((depth, hpt, 1, HG_DK), lambda i, j: (0, j, 0, 0)),
            pl.BlockSpec((d, tn), lambda i, j: (0, j)),
            s_in,
        ],
        out_specs=[ospec] * 5 + [s_out],
        out_shape=[head_major(BF16), head_major(F32), head_major(F32), head_major(BF16), head_major(BF16),
                   s_shape],
        scratch_shapes=[pltpu.VMEM((tm, d), BF16)],
        compiler_params=_cparams("arbitrary", "arbitrary"),
        name="hgrn_inproj",
    )(x, mod_l, gain, lb_logits.reshape(depth, n_heads, 1, HG_DK), w_perm, side_w)


def _cum_matrix(tr):
    n_chunks = tr // CHUNK
    r = np.arange(tr)[:, None]
    c = np.arange(tr)[None, :]
    same = (r // CHUNK) == (c // CHUNK)
    incl = same & (c <= r)
    pivot = same & (c % CHUNK <= CHUNK // 2 - 1)
    rows = np.zeros((2 * n_chunks, tr), np.float32)
    for ch in range(n_chunks):
        rows[2 * ch, ch * CHUNK: ch * CHUNK + CHUNK // 2] = 1.0
        rows[2 * ch + 1, ch * CHUNK: (ch + 1) * CHUNK] = 1.0
    mats = np.concatenate([incl.astype(np.float32) - pivot.astype(np.float32), rows], axis=0)
    return jnp.asarray(mats, dtype=BF16)


def _hgrn_kernel(gain_ref, cm_ref, q_ref, lf_ref, kk_ref, v_ref, g_ref, side_ref,
                 o_ref, sideb_ref, st_ref, *, hb, tr):
    tstep = pl.program_id(1)
    sideb_ref[...] = side_ref[...].astype(BF16)

    @pl.when(tstep == 0)
    def _():
        st_ref[...] = jnp.zeros_like(st_ref)

    rowc = lax.broadcasted_iota(jnp.int32, (CHUNK, 2 * CHUNK), 0)
    colc = lax.broadcasted_iota(jnp.int32, (CHUNK, 2 * CHUNK), 1)
    causal2 = (colc % CHUNK) <= rowc
    n_chunks = tr // CHUNK
    zc = jnp.zeros((CHUNK, HG_DK), BF16)
    zs = jnp.zeros((HG_DK, HG_DK), BF16)

    def block_diag(a, b, z):
        return jnp.concatenate([jnp.concatenate([a, z], axis=1), jnp.concatenate([z, b], axis=1)], axis=0)

    for pp in range(hb // 2):
        heads = (2 * pp, 2 * pp + 1)
        qe, ke, qb, ku, dec = [], [], [], [], []
        for hh in heads:
            logf = lf_ref[hh]
            p_hi = logf.astype(BF16)
            p_lo = (logf - p_hi.astype(F32)).astype(BF16)
            cums = jnp.dot(cm_ref[...], jnp.concatenate([p_hi, p_lo], axis=1), preferred_element_type=F32)
            cums = cums[:, :HG_DK] + cums[:, HG_DK:]
            bmp = cums[:tr]
            qe_f = q_ref[hh].astype(F32) * jnp.exp(bmp)
            ke_f = kk_ref[hh] * jnp.exp(-bmp)
            qb_h, ku_h, dec_h = [], [], []
            for c in range(n_chunks):
                rs = slice(c * CHUNK, (c + 1) * CHUNK)
                piv = cums[tr + 2 * c: tr + 2 * c + 1]
                blast = cums[tr + 2 * c + 1: tr + 2 * c + 2]
                qb_h.append((qe_f[rs] * jnp.exp(piv)).astype(BF16))
                ku_h.append((ke_f[rs] * jnp.exp(blast - piv)).astype(BF16))
                dec_h.append(jnp.exp(blast))
            qe.append(qe_f.astype(BF16))
            ke.append(ke_f.astype(BF16))
            qb.append(qb_h)
            ku.append(ku_h)
            dec.append(dec_h)

        for c in range(n_chunks):
            rs = slice(c * CHUNK, (c + 1) * CHUNK)
            v0 = v_ref[heads[0], rs, :]
            v1 = v_ref[heads[1], rs, :]
            a = lax.dot_general(jnp.concatenate([qe[0][rs], qe[1][rs]], axis=1),
                                block_diag(ke[0][rs], ke[1][rs], zc),
                                (((1,), (1,)), ((), ())), preferred_element_type=F32)
            a = jnp.where(causal2, a, 0.0).astype(BF16)
            st = st_ref[pp]
            st_b = st.astype(BF16)
            o = (jnp.dot(a, block_diag(v0, v1, zc), preferred_element_type=F32)
                 + lax.dot_general(jnp.concatenate([qb[0][c], qb[1][c]], axis=1),
                                   block_diag(st_b[:, :HG_DK], st_b[:, HG_DK:], zs),
                                   (((1,), (1,)), ((), ())), preferred_element_type=F32))
            upd = lax.dot_general(jnp.concatenate([v0, v1], axis=0), block_diag(ku[0][c], ku[1][c], zc),
                                  (((0,), (0,)), ((), ())), preferred_element_type=F32)
            st_ref[pp] = st * jnp.concatenate([dec[0][c], dec[1][c]], axis=1) + upd
            for e, hh in enumerate(heads):
                oh = o[:, e * HG_DK:(e + 1) * HG_DK]
                on = oh * lax.rsqrt(jnp.mean(oh * oh, axis=-1, keepdims=True) + EPS) * gain_ref[hh]
                o_ref[rs, hh * HG_DK:(hh + 1) * HG_DK] = (on * g_ref[hh, rs, :].astype(F32)).astype(BF16)


def _hgrn_call(q, lf, kk, v, g, o_gain, side_w, side_layer, hb=8, tr=256):
    n_heads, t, dk = q.shape
    assert dk == HG_DK and hb % 2 == 0
    cm = _cum_matrix(tr)
    n_inner = t // tr
    blk = pl.BlockSpec((hb, tr, dk), lambda h, s: (h, s, 0))
    s_in, s_out, s_shape = _cast_job(side_w, side_layer, (n_heads // hb) * n_inner,
                                     lambda h, s: h * n_inner + s)
    return pl.pallas_call(
        functools.partial(_hgrn_kernel, hb=hb, tr=tr),
        grid=(n_heads // hb, n_inner),
        in_specs=[
            pl.BlockSpec((hb, 1, dk), lambda h, s: (h, 0, 0)),
            _resident(cm.shape),
            blk, blk, blk, blk, blk,
            s_in,
        ],
        out_specs=[pl.BlockSpec((tr, hb * dk), lambda h, s: (s, h)), s_out],
        out_shape=[jax.ShapeDtypeStruct((t, n_heads * dk), BF16), s_shape],
        scratch_shapes=[pltpu.VMEM((hb // 2, dk, 2 * dk), F32)],
        compiler_params=_cparams("arbitrary", "arbitrary"),
        name="hgrn_recurrence",
    )(o_gain.reshape(n_heads, 1, dk), cm, q, lf, kk, v, g, side_w)


def kernel(x, c, mod_w, mod_b, norm_mix, norm_mlp, attn_w_in, attn_w_out, attn_q_gain, attn_k_gain,
           attn_sinks, hgrn_w_in, hgrn_w_out, hgrn_o_gain, hgrn_lb_logits, mlp_w1, mlp_w2):
    b, t, d = x.shape
    assert b == 1 and mod_w.shape[0] == 2
    xs = x.reshape(t, d)

    mod = _mod_call(c, mod_w, mod_b)

    q_dim = attn_w_out.shape[1]
    kv_dim = (attn_w_in.shape[2] - q_dim) // 2
    qk_gain = jnp.concatenate([jnp.tile(attn_q_gain[0], q_dim // HEAD_DIM),
                               jnp.tile(attn_k_gain[0], kv_dim // HEAD_DIM)]).reshape(1, q_dim + kv_dim)
    q, k, v = _attn_inproj_call(xs, mod[0], norm_mix[0].reshape(1, d), attn_w_in, qk_gain)
    att, w1b, w2b = _attn_call(q, k, v, attn_sinks[0], mlp_w1, mlp_w2, layer=0)
    xs = _outproj_call(att, xs, mod[0], attn_w_out, gate_row=2)
    n_heads = hgrn_o_gain.shape[1]
    hpt = 2
    xs, hw_perm = _mlp_call(xs, mod[0], norm_mlp[0].reshape(1, d), w1b, w2b,
                            side=_hgrn_w_in_side_job(hgrn_w_in, 0, n_heads, hpt))

    hq, hlf, hkk, hv, hg, w1b = _hgrn_inproj_call(xs, mod[1], norm_mix[1].reshape(1, d), hgrn_lb_logits,
                                                  hw_perm, n_heads, mlp_w1, 1, hpt=hpt)
    ho, w2b = _hgrn_call(hq, hlf, hkk, hv, hg, hgrn_o_gain[0], mlp_w2, 1)
    xs = _outproj_call(ho, xs, mod[1], hgrn_w_out, gate_row=2)
    xs = _mlp_call(xs, mod[1], norm_mlp[1].reshape(1, d), w1b, w2b)
    return xs.reshape(b, t, d)
```

```python
import functools
import math

import numpy as np
import jax
import jax.numpy as jnp
from jax import lax
from jax.experimental import pallas as pl
from jax.experimental.pallas import tpu as pltpu

F32 = jnp.float32
BF16 = jnp.bfloat16

EPS = 1e-6
N_MOD = 6

HEAD_DIM = 64
GROUP = 8
WINDOW = 128
LANES = 128

HG_DK = 128
CHUNK = 64

VMEM_LIMIT = 56 * 1024 * 1024


def _cparams(*sem):
    return pltpu.CompilerParams(dimension_semantics=sem, vmem_limit_bytes=VMEM_LIMIT)


def _resident(shape):
    nd = len(shape)
    return pl.BlockSpec(shape, lambda *_: (0,) * nd, pipeline_mode=pl.Buffered(1))


def _cast_job(w, layer, n_steps, step_of):
    _, r, c = w.shape
    rb = r // n_steps
    assert rb * n_steps == r and rb % 16 == 0
    in_spec = pl.BlockSpec((None, rb, c), lambda *g: (layer, step_of(*g), 0))
    out_spec = pl.BlockSpec((rb, c), lambda *g: (step_of(*g), 0))
    return in_spec, out_spec, jax.ShapeDtypeStruct((r, c), BF16)


def _sigmoid(v):
    return 0.5 * jnp.tanh(0.5 * v) + 0.5


def _silu(v):
    return v * _sigmoid(v)


def _norm_mod(x, gain, shift, scale):
    ms = jnp.mean(x * x, axis=-1, keepdims=True)
    y = x * lax.rsqrt(ms + EPS) * gain
    return y * (1.0 + scale) + shift


def _mod_kernel(c_ref, w_ref, b_ref, o_ref):
    c = c_ref[...]
    cond = _silu(c)
    acc = jnp.sum(cond * w_ref[0], axis=0, keepdims=True)
    o_ref[0] = acc + b_ref[0]


def _mod_call(c, mod_w, mod_b, tn=1024):
    depth, d, n = mod_w.shape
    out = pl.pallas_call(
        _mod_kernel,
        grid=(depth, n // tn),
        in_specs=[
            pl.BlockSpec((d, 1), lambda l, j: (0, 0)),
            pl.BlockSpec((1, d, tn), lambda l, j: (l, 0, j)),
            pl.BlockSpec((1, 1, tn), lambda l, j: (l, 0, j)),
        ],
        out_specs=pl.BlockSpec((1, 1, tn), lambda l, j: (l, 0, j)),
        out_shape=jax.ShapeDtypeStruct((depth, 1, n), F32),
        compiler_params=_cparams("arbitrary", "arbitrary"),
        name="mod_proj",
    )(c.reshape(d, 1), mod_w, mod_b.reshape(depth, 1, n))
    return out.reshape(depth, N_MOD, d)


def _attn_inproj_kernel(x_ref, mod_ref, gain_ref, w_ref, qkg_ref, p_ref, pt_ref,
                        q_ref, k_ref, v_ref, wb_ref, *, q_dim, kv_dim, scale, n_sub):
    @pl.when(pl.program_id(0) == 0)
    def _():
        wb_ref[...] = w_ref[...].astype(BF16)

    sub = x_ref.shape[0] // n_sub
    for s in range(n_sub):
        rs = slice(s * sub, (s + 1) * sub)
        h = _norm_mod(x_ref[rs, :], gain_ref[...], mod_ref[0:1, :], mod_ref[1:2, :]).astype(BF16)
        proj = jnp.dot(h, wb_ref[...], preferred_element_type=F32)
        qk = proj[:, :q_dim + kv_dim]
        ss = jnp.dot((qk * qk).astype(BF16), p_ref[...], preferred_element_type=F32)
        inv = lax.rsqrt(ss * (1.0 / HEAD_DIM) + EPS)
        inv_hi = inv.astype(BF16)
        inv_lo = (inv - inv_hi.astype(F32)).astype(BF16)
        inv_b = jnp.dot(jnp.concatenate([inv_hi, inv_lo], axis=1), pt_ref[...],
                        preferred_element_type=F32)
        qkn = qk * inv_b * qkg_ref[...]
        q_ref[rs, :] = (qkn[:, :q_dim] * scale).astype(BF16)
        k_ref[rs, :] = qkn[:, q_dim:].astype(BF16)
        v_ref[rs, :] = proj[:, q_dim + kv_dim:].astype(BF16)


def _attn_inproj_call(x, mod_l, gain, w_all, qk_gain, tm=512, n_sub=2):
    t, d = x.shape
    n = w_all.shape[2]
    kv_dim = (n - d) // 2
    q_dim = d
    n_heads = (q_dim + kv_dim) // HEAD_DIM
    ind = (np.arange(q_dim + kv_dim)[:, None] // HEAD_DIM == np.arange(LANES)[None, :])
    p = jnp.asarray(ind, dtype=BF16)
    pt = jnp.asarray(np.concatenate([ind.T, ind.T], axis=0), dtype=BF16)
    assert n_heads <= LANES
    kern = functools.partial(_attn_inproj_kernel, q_dim=q_dim, kv_dim=kv_dim,
                             scale=1.0 / math.sqrt(HEAD_DIM), n_sub=n_sub)
    return pl.pallas_call(
        kern,
        grid=(t // tm,),
        in_specs=[
            pl.BlockSpec((tm, d), lambda i: (i, 0)),
            _resident((N_MOD, d)),
            _resident((1, d)),
            pl.BlockSpec((None, d, n), lambda i: (0, 0, 0), pipeline_mode=pl.Buffered(1)),
            _resident((1, q_dim + kv_dim)),
            _resident((q_dim + kv_dim, LANES)),
            _resident((2 * LANES, q_dim + kv_dim)),
        ],
        out_specs=[
            pl.BlockSpec((tm, q_dim), lambda i: (i, 0)),
            pl.BlockSpec((tm, kv_dim), lambda i: (i, 0)),
            pl.BlockSpec((tm, kv_dim), lambda i: (i, 0)),
        ],
        out_shape=[
            jax.ShapeDtypeStruct((t, q_dim), BF16),
            jax.ShapeDtypeStruct((t, kv_dim), BF16),
            jax.ShapeDtypeStruct((t, kv_dim), BF16),
        ],
        scratch_shapes=[pltpu.VMEM((d, n), BF16)],
        compiler_params=_cparams("arbitrary"),
        name="attn_inproj",
    )(x, mod_l, gain, w_all, qk_gain, p, pt)


def _attn_kernel(sink_ref, q_ref, kc_ref, kp_ref, vc_ref, vp_ref, w1_ref, w2_ref,
                 o_ref, w1b_ref, w2b_ref, bias_ref, *, n_kv):
    n = pl.program_id(0)
    w1b_ref[...] = w1_ref[...].astype(BF16)
    w2b_ref[...] = w2_ref[...].astype(BF16)
    n_heads = n_kv * GROUP
    row = lax.broadcasted_iota(jnp.int32, (WINDOW, WINDOW), 0)
    col = lax.broadcasted_iota(jnp.int32, (WINDOW, WINDOW), 1)
    in_cur = col <= row

    @pl.when(n == 0)
    def _():
        dist = jnp.where(in_cur, row - col, row - col + WINDOW).astype(F32)
        for hq in range(n_heads):
            slope = 2.0 ** (-8.0 * (hq + 1) / n_heads)
            bias_ref[hq] = -slope * dist

    prev_pen = jnp.where(n == 0, -jnp.inf, 0.0).astype(F32)
    lane = lax.broadcasted_iota(jnp.int32, (WINDOW, LANES), 1)
    low = lane < HEAD_DIM

    for g in range(n_kv):
        pair = g // 2
        sl = slice(pair * LANES, (pair + 1) * LANES)
        zero = jnp.zeros((WINDOW, LANES), BF16)

        def halves(ref):
            t = ref[:, sl]
            r = pltpu.roll(t, HEAD_DIM, axis=1)
            if g % 2 == 0:
                return jnp.where(low, t, zero), jnp.where(low, zero, r)
            return jnp.where(low, r, zero), jnp.where(low, zero, t)

        kc_lo, kc_hi = halves(kc_ref)
        kp_lo, kp_hi = halves(kp_ref)
        vc_lo, vc_hi = halves(vc_ref)
        vp_lo, vp_hi = halves(vp_ref)
        k_rhs = jnp.concatenate([kc_lo, kc_hi, kp_lo, kp_hi], axis=0)
        v_rhs = jnp.concatenate([vc_lo, vp_lo, vc_hi, vp_hi], axis=0)

        n_pairs = GROUP // 2
        q_base = g * GROUP * HEAD_DIM
        qs = jnp.concatenate(
            [q_ref[:, q_base + t * LANES: q_base + (t + 1) * LANES] for t in range(n_pairs)], axis=0)
        logits = lax.dot_general(qs, k_rhs, (((1,), (1,)), ((), ())), preferred_element_type=F32)

        for t in range(n_pairs):
            rows = slice(t * WINDOW, (t + 1) * WINDOW)
            probs = []
            for e in range(2):
                hq = g * GROUP + 2 * t + e
                l_cur = logits[rows, e * WINDOW:(e + 1) * WINDOW]
                l_prev = logits[rows, (2 + e) * WINDOW:(3 + e) * WINDOW]
                s = jnp.where(in_cur, l_cur, l_prev + prev_pen) + bias_ref[hq]
                sink = sink_ref[0, hq]
                mx = jnp.maximum(jnp.max(s, axis=1, keepdims=True), sink)
                pe = jnp.exp(s - mx)
                den = jnp.sum(pe, axis=1, keepdims=True) + jnp.exp(sink - mx)
                pn = pe / den
                probs.append(jnp.where(in_cur, pn, 0.0).astype(BF16))
                probs.append(jnp.where(in_cur, 0.0, pn).astype(BF16))
            pm = jnp.concatenate(probs, axis=1)
            o_pair = jnp.dot(pm, v_rhs, preferred_element_type=F32)
            o_ref[:, q_base + t * LANES: q_base + (t + 1) * LANES] = o_pair.astype(BF16)


def _attn_call(q, k, v, sinks, mlp_w1, mlp_w2, layer):
    t, q_dim = q.shape
    kv_dim = k.shape[1]
    n_kv = kv_dim // HEAD_DIM
    n_heads = n_kv * GROUP
    assert n_kv % 2 == 0 and q_dim == n_heads * HEAD_DIM
    nb = t // WINDOW
    cur = lambda n: (n, 0)
    prev = lambda n: (jnp.maximum(n - 1, 0), 0)
    w1_in, w1_out, w1_shape = _cast_job(mlp_w1, layer, nb, lambda n: n)
    w2_in, w2_out, w2_shape = _cast_job(mlp_w2, layer, nb, lambda n: n)
    return pl.pallas_call(
        functools.partial(_attn_kernel, n_kv=n_kv),
        grid=(nb,),
        in_specs=[
            pl.BlockSpec(memory_space=pltpu.SMEM),
            pl.BlockSpec((WINDOW, q_dim), cur),
            pl.BlockSpec((WINDOW, kv_dim), cur),
            pl.BlockSpec((WINDOW, kv_dim), prev),
            pl.BlockSpec((WINDOW, kv_dim), cur),
            pl.BlockSpec((WINDOW, kv_dim), prev),
            w1_in, w2_in,
        ],
        out_specs=[pl.BlockSpec((WINDOW, q_dim), cur), w1_out, w2_out],
        out_shape=[jax.ShapeDtypeStruct((t, q_dim), BF16), w1_shape, w2_shape],
        scratch_shapes=[pltpu.VMEM((n_heads, WINDOW, WINDOW), F32)],
        compiler_params=_cparams("arbitrary"),
        name="swa_attention",
    )(sinks.reshape(1, n_heads), q, k, k, v, v, mlp_w1, mlp_w2)


def _outproj_kernel(a_ref, x_ref, mod_ref, w_ref, o_ref, wb_ref, *, gate_row):
    @pl.when(pl.program_id(0) == 0)
    def _():
        wb_ref[...] = w_ref[...].astype(BF16)

    y = jnp.dot(a_ref[...], wb_ref[...], preferred_element_type=F32)
    o_ref[...] = x_ref[...] + mod_ref[gate_row:gate_row + 1, :] * y


def _outproj_call(a, x, mod_l, w_all, gate_row, tm=512):
    t, d = x.shape
    kdim = a.shape[1]
    return pl.pallas_call(
        functools.partial(_outproj_kernel, gate_row=gate_row),
        grid=(t // tm,),
        in_specs=[
            pl.BlockSpec((tm, kdim), lambda i: (i, 0)),
            pl.BlockSpec((tm, d), lambda i: (i, 0)),
            _resident((N_MOD, d)),
            pl.BlockSpec((None, kdim, d), lambda i: (0, 0, 0), pipeline_mode=pl.Buffered(1)),
        ],
        out_specs=pl.BlockSpec((tm, d), lambda i: (i, 0)),
        out_shape=jax.ShapeDtypeStruct((t, d), F32),
        scratch_shapes=[pltpu.VMEM((kdim, d), BF16)],
        compiler_params=_cparams("arbitrary"),
        name="outproj_residual",
    )(a, x, mod_l, w_all)


def _mlp_kernel(*refs, has_side, n_sub):
    if has_side:
        x_ref, mod_ref, gain_ref, w1_ref, w2_ref, side_ref, o_ref, sideb_ref, h_ref = refs
        sideb_ref[...] = side_ref[...].astype(BF16)
    else:
        x_ref, mod_ref, gain_ref, w1_ref, w2_ref, o_ref, h_ref = refs
    f = pl.program_id(1)
    gate = mod_ref[5:6, :]

    def ffn(h):
        a = jnp.maximum(jnp.dot(h, w1_ref[...], preferred_element_type=F32), 0.0)
        return gate * jnp.dot((a * a).astype(BF16), w2_ref[...], preferred_element_type=F32)

    @pl.when(f == 0)
    def _():
        sub = x_ref.shape[0] // n_sub
        for s in range(n_sub):
            rs = slice(s * sub, (s + 1) * sub)
            xs = x_ref[rs, :]
            h = _norm_mod(xs, gain_ref[...], mod_ref[3:4, :], mod_ref[4:5, :]).astype(BF16)
            h_ref[rs, :] = h
            o_ref[rs, :] = xs + ffn(h)

    @pl.when(f > 0)
    def _():
        o_ref[...] += ffn(h_ref[...])


def _mlp_call(x, mod_l, gain, w1, w2, side=None, tm=512, tf=1024, n_sub=2):
    t, d = x.shape
    dff = w1.shape[1]
    nf = dff // tf
    in_specs = [
        pl.BlockSpec((tm, d), lambda i, f: (i, 0)),
        _resident((N_MOD, d)),
        _resident((1, d)),
        pl.BlockSpec((d, tf), lambda i, f: (0, f)),
        pl.BlockSpec((tf, d), lambda i, f: (f, 0)),
    ]
    out_specs = [pl.BlockSpec((tm, d), lambda i, f: (i, 0))]
    out_shape = [jax.ShapeDtypeStruct((t, d), F32)]
    args = [x, mod_l, gain, w1, w2]
    if side is not None:
        arr, in_fn, out_fn, shape = side
        in_specs.append(in_fn(t // tm, nf))
        out_specs.append(out_fn(t // tm, nf))
        out_shape.append(shape)
        args.append(arr)
    outs = pl.pallas_call(
        functools.partial(_mlp_kernel, has_side=side is not None, n_sub=n_sub),
        grid=(t // tm, nf),
        in_specs=in_specs,
        out_specs=out_specs,
        out_shape=out_shape,
        scratch_shapes=[pltpu.VMEM((tm, d), BF16)],
        compiler_params=_cparams("arbitrary", "arbitrary"),
        name="relu2_mlp",
    )(*args)
    return outs if side is not None else outs[0]


def _hgrn_w_in_side_job(hgrn_w_in, layer, n_heads, hpt):
    _, d, n = hgrn_w_in.shape
    cw = hpt * HG_DK
    n_tiles = n_heads // hpt
    n_cb = 4 * n_tiles

    def blocks(n_outer, n_inner):
        n_rb = (n_outer * n_inner) // n_cb
        assert n_rb * n_cb == n_outer * n_inner and d % n_rb == 0 and (d // n_rb) % 16 == 0
        return d // n_rb

    def in_fn(n_outer, n_inner):
        rb = blocks(n_outer, n_inner)

        def imap(i, f):
            step = i * n_inner + f
            out_cb = step % n_cb
            return (layer, step // n_cb, (out_cb % 4) * n_tiles + out_cb // 4)
        return pl.BlockSpec((None, rb, cw), imap)

    def out_fn(n_outer, n_inner):
        rb = blocks(n_outer, n_inner)

        def omap(i, f):
            step = i * n_inner + f
            return (step // n_cb, step % n_cb)
        return pl.BlockSpec((rb, cw), omap)

    return hgrn_w_in, in_fn, out_fn, jax.ShapeDtypeStruct((d, n), BF16)


def _layer_lower_bound(lbl_ref):
    l0 = lbl_ref[0]
    l1 = lbl_ref[1]
    lm = jnp.maximum(l0, l1)
    e0 = jnp.exp(l0 - lm)
    e1 = jnp.exp(l1 - lm)
    p0 = e0 / (e0 + e1)
    p1 = e1 / (e0 + e1)
    return (p0 + p1) - p0


def _hgrn_inproj_kernel(x_ref, mod_ref, gain_ref, lbl_ref, w_ref, side_ref,
                        q_ref, lf_ref, kk_ref, v_ref, g_ref, sideb_ref, h_ref, *, hpt, scale, n_sub):
    j = pl.program_id(1)
    sideb_ref[...] = side_ref[...].astype(BF16)
    lb = _layer_lower_bound(lbl_ref)
    width = hpt * HG_DK

    def project(h, rs):
        res = jnp.dot(h, w_ref[...], preferred_element_type=F32)
        for hh in range(hpt):
            seg = lambda s: res[:, s * width + hh * HG_DK: s * width + (hh + 1) * HG_DK]
            q_ref[hh, rs, :] = (_silu(seg(0)) * scale).astype(BF16)
            forget = lb[hh] + (1.0 - lb[hh]) * _sigmoid(seg(1))
            kk_ref[hh, rs, :] = 1.0 - forget
            lf_ref[hh, rs, :] = jnp.log(forget)
            v_ref[hh, rs, :] = seg(2).astype(BF16)
            g_ref[hh, rs, :] = _silu(seg(3)).astype(BF16)

    @pl.when(j == 0)
    def _():
        sub = x_ref.shape[0] // n_sub
        for s in range(n_sub):
            rs = slice(s * sub, (s + 1) * sub)
            h = _norm_mod(x_ref[rs, :], gain_ref[...], mod_ref[0:1, :], mod_ref[1:2, :]).astype(BF16)
            h_ref[rs, :] = h
            project(h, rs)

    @pl.when(j > 0)
    def _():
        project(h_ref[...], slice(None))


def _hgrn_inproj_call(x, mod_l, gain, lb_logits, w_perm, n_heads, side_w, side_layer, tm=1024, hpt=2):
    t, d = x.shape
    depth = lb_logits.shape[0]
    assert depth == 2
    tn = 4 * hpt * HG_DK
    n_inner = n_heads // hpt
    head_major = lambda dt: jax.ShapeDtypeStruct((n_heads, t, HG_DK), dt)
    ospec = pl.BlockSpec((hpt, tm, HG_DK), lambda i, j: (j, i, 0))
    s_in, s_out, s_shape = _cast_job(side_w, side_layer, (t // tm) * n_inner, lambda i, j: i * n_inner + j)
    return pl.pallas_call(
        functools.partial(_hgrn_inproj_kernel, hpt=hpt, scale=1.0 / math.sqrt(HG_DK), n_sub=4),
        grid=(t // tm, n_inner),
        in_specs=[
            pl.BlockSpec((tm, d), lambda i, j: (i, 0)),
            _resident((N_MOD, d)),
            _resident((1, d)),
            pl.BlockSpec((depth, hpt, 1, HG_DK), lambda i, j: (0, j, 0, 0)),
            pl.BlockSpec((d, tn), lambda i, j: (0, j)),
            s_in,
        ],
        out_specs=[ospec] * 5 + [s_out],
        out_shape=[head_major(BF16), head_major(F32), head_major(F32), head_major(BF16), head_major(BF16),
                   s_shape],
        scratch_shapes=[pltpu.VMEM((tm, d), BF16)],
        compiler_params=_cparams("arbitrary", "arbitrary"),
        name="hgrn_inproj",
    )(x, mod_l, gain, lb_logits.reshape(depth, n_heads, 1, HG_DK), w_perm, side_w)


def _cum_matrix(tr):
    n_chunks = tr // CHUNK
    r = np.arange(tr)[:, None]
    c = np.arange(tr)[None, :]
    same = (r // CHUNK) == (c // CHUNK)
    incl = same & (c <= r)
    pivot = same & (c % CHUNK <= CHUNK // 2 - 1)
    rows = np.zeros((2 * n_chunks, tr), np.float32)
    for ch in range(n_chunks):
        rows[2 * ch, ch * CHUNK: ch * CHUNK + CHUNK // 2] = 1.0
        rows[2 * ch + 1, ch * CHUNK: (ch + 1) * CHUNK] = 1.0
    mats = np.concatenate([incl.astype(np.float32) - pivot.astype(np.float32), rows], axis=0)
    return jnp.asarray(mats, dtype=BF16)


def _hgrn_kernel(gain_ref, cm_ref, q_ref, lf_ref, kk_ref, v_ref, g_ref, side_ref,
                 o_ref, sideb_ref, st_ref, *, hb, tr):
    tstep = pl.program_id(1)
    sideb_ref[...] = side_ref[...].astype(BF16)

    @pl.when(tstep == 0)
    def _():
        st_ref[...] = jnp.zeros_like(st_ref)

    rowc = lax.broadcasted_iota(jnp.int32, (CHUNK, 2 * CHUNK), 0)
    colc = lax.broadcasted_iota(jnp.int32, (CHUNK, 2 * CHUNK), 1)
    causal2 = (colc % CHUNK) <= rowc
    n_chunks = tr // CHUNK
    zc = jnp.zeros((CHUNK, HG_DK), BF16)
    zs = jnp.zeros((HG_DK, HG_DK), BF16)

    def block_diag(a, b, z):
        return jnp.concatenate([jnp.concatenate([a, z], axis=1), jnp.concatenate([z, b], axis=1)], axis=0)

    for pp in range(hb // 2):
        heads = (2 * pp, 2 * pp + 1)
        qe, ke, qb, ku, dec = [], [], [], [], []
        for hh in heads:
            logf = lf_ref[hh]
            p_hi = logf.astype(BF16)
            p_lo = (logf - p_hi.astype(F32)).astype(BF16)
            cums = jnp.dot(cm_ref[...], jnp.concatenate([p_hi, p_lo], axis=1), preferred_element_type=F32)
            cums = cums[:, :HG_DK] + cums[:, HG_DK:]
            bmp = cums[:tr]
            qe_f = q_ref[hh].astype(F32) * jnp.exp(bmp)
            ke_f = kk_ref[hh] * jnp.exp(-bmp)
            qb_h, ku_h, dec_h = [], [], []
            for c in range(n_chunks):
                rs = slice(c * CHUNK, (c + 1) * CHUNK)
                piv = cums[tr + 2 * c: tr + 2 * c + 1]
                blast = cums[tr + 2 * c + 1: tr + 2 * c + 2]
                qb_h.append((qe_f[rs] * jnp.exp(piv)).astype(BF16))
                ku_h.append((ke_f[rs] * jnp.exp(blast - piv)).astype(BF16))
                dec_h.append(jnp.exp(blast))
            qe.append(qe_f.astype(BF16))
            ke.append(ke_f.astype(BF16))
            qb.append(qb_h)
            ku.append(ku_h)
            dec.append(dec_h)

        for c in range(n_chunks):
            rs = slice(c * CHUNK, (c + 1) * CHUNK)
            v0 = v_ref[heads[0], rs, :]
            v1 = v_ref[heads[1], rs, :]
            a = lax.dot_general(jnp.concatenate([qe[0][rs], qe[1][rs]], axis=1),
                                block_diag(ke[0][rs], ke[1][rs], zc),
                                (((1,), (1,)), ((), ())), preferred_element_type=F32)
            a = jnp.where(causal2, a, 0.0).astype(BF16)
            st = st_ref[pp]
            st_b = st.astype(BF16)
            o = (jnp.dot(a, block_diag(v0, v1, zc), preferred_element_type=F32)
                 + lax.dot_general(jnp.concatenate([qb[0][c], qb[1][c]], axis=1),
                                   block_diag(st_b[:, :HG_DK], st_b[:, HG_DK:], zs),
                                   (((1,), (1,)), ((), ())), preferred_element_type=F32))
            upd = lax.dot_general(jnp.concatenate([v0, v1], axis=0), block_diag(ku[0][c], ku[1][c], zc),
                                  (((0,), (0,)), ((), ())), preferred_element_type=F32)
            st_ref[pp] = st * jnp.concatenate([dec[0][c], dec[1][c]], axis=1) + upd
            for e, hh in enumerate(heads):
                oh = o[:, e * HG_DK:(e + 1) * HG_DK]
                on = oh * lax.rsqrt(jnp.mean(oh * oh, axis=-1, keepdims=True) + EPS) * gain_ref[hh]
                o_ref[rs, hh * HG_DK:(hh + 1) * HG_DK] = (on * g_ref[hh, rs, :].astype(F32)).astype(BF16)


def _hgrn_call(q, lf, kk, v, g, o_gain, side_w, side_layer, hb=8, tr=256):
    n_heads, t, dk = q.shape
    assert dk == HG_DK and hb % 2 == 0
    cm = _cum_matrix(tr)
    n_inner = t // tr
    blk = pl.BlockSpec((hb, tr, dk), lambda h, s: (h, s, 0))
    s_in, s_out, s_shape = _cast_job(side_w, side_layer, (n_heads // hb) * n_inner,
                                     lambda h, s: h * n_inner + s)
    return pl.pallas_call(
        functools.partial(_hgrn_kernel, hb=hb, tr=tr),
        grid=(n_heads // hb, n_inner),
        in_specs=[
            pl.BlockSpec((hb, 1, dk), lambda h, s: (h, 0, 0)),
            _resident(cm.shape),
            blk, blk, blk, blk, blk,
            s_in,
        ],
        out_specs=[pl.BlockSpec((tr, hb * dk), lambda h, s: (s, h)), s_out],
        out_shape=[jax.ShapeDtypeStruct((t, n_heads * dk), BF16), s_shape],
        scratch_shapes=[pltpu.VMEM((hb // 2, dk, 2 * dk), F32)],
        compiler_params=_cparams("arbitrary", "arbitrary"),
        name="hgrn_recurrence",
    )(o_gain.reshape(n_heads, 1, dk), cm, q, lf, kk, v, g, side_w)


def kernel(x, c, mod_w, mod_b, norm_mix, norm_mlp, attn_w_in, attn_w_out, attn_q_gain, attn_k_gain,
           attn_sinks, hgrn_w_in, hgrn_w_out, hgrn_o_gain, hgrn_lb_logits, mlp_w1, mlp_w2):
    b, t, d = x.shape
    assert b == 1 and mod_w.shape[0] == 2
    xs = x.reshape(t, d)

    mod = _mod_call(c, mod_w, mod_b)

    q_dim = attn_w_out.shape[1]
    kv_dim = (attn_w_in.shape[2] - q_dim) // 2
    qk_gain = jnp.concatenate([jnp.tile(attn_q_gain[0], q_dim // HEAD_DIM),
                               jnp.tile(attn_k_gain[0], kv_dim // HEAD_DIM)]).reshape(1, q_dim + kv_dim)
    q, k, v = _attn_inproj_call(xs, mod[0], norm_mix[0].reshape(1, d), attn_w_in, qk_gain)
    att, w1b, w2b = _attn_call(q, k, v, attn_sinks[0], mlp_w1, mlp_w2, layer=0)
    xs = _outproj_call(att, xs, mod[0], attn_w_out, gate_row=2)
    n_heads = hgrn_o_gain.shape[1]
    hpt = 2
    xs, hw_perm = _mlp_call(xs, mod[0], norm_mlp[0].reshape(1, d), w1b, w2b,
                            side=_hgrn_w_in_side_job(hgrn_w_in, 0, n_heads, hpt))

    hq, hlf, hkk, hv, hg, w1b = _hgrn_inproj_call(xs, mod[1], norm_mix[1].reshape(1, d), hgrn_lb_logits,
                                                  hw_perm, n_heads, mlp_w1, 1, hpt=hpt)
    ho, w2b = _hgrn_call(hq, hlf, hkk, hv, hg, hgrn_o_gain[0], mlp_w2, 1)
    xs = _outproj_call(ho, xs, mod[1], hgrn_w_out, gate_row=2)
    xs = _mlp_call(xs, mod[1], norm_mlp[1].reshape(1, d), w1b, w2b)
    return xs.reshape(b, t, d)
```

```python
import functools
import math

import numpy as np
import jax
import jax.numpy as jnp
from jax import lax
from jax.experimental import pallas as pl
from jax.experimental.pallas import tpu as pltpu

F32 = jnp.float32
BF16 = jnp.bfloat16

EPS = 1e-6
N_MOD = 6

HEAD_DIM = 64
GROUP = 8
WINDOW = 128
LANES = 128
LOG2E = math.log2(math.e)

HG_DK = 128
CHUNK = 64

VMEM_LIMIT = 56 * 1024 * 1024


def _cparams(*sem):
    return pltpu.CompilerParams(dimension_semantics=sem, vmem_limit_bytes=VMEM_LIMIT)


def _resident(shape):
    nd = len(shape)
    return pl.BlockSpec(shape, lambda *_: (0,) * nd, pipeline_mode=pl.Buffered(1))


def _cast_job(w, layer, n_steps, step_of):
    _, r, c = w.shape
    rb = r // n_steps
    assert rb * n_steps == r and rb % 16 == 0
    in_spec = pl.BlockSpec((None, rb, c), lambda *g: (layer, step_of(*g), 0))
    out_spec = pl.BlockSpec((rb, c), lambda *g: (step_of(*g), 0))
    return in_spec, out_spec, jax.ShapeDtypeStruct((r, c), BF16)


def _sigmoid(v):
    return 0.5 * jnp.tanh(0.5 * v) + 0.5


def _silu(v):
    return v * _sigmoid(v)


def _norm_mod(x, gain, shift, scale):
    ms = jnp.mean(x * x, axis=-1, keepdims=True)
    y = x * lax.rsqrt(ms + EPS) * gain
    return y * (1.0 + scale) + shift


def _mod_kernel(c_ref, w_ref, b_ref, o_ref):
    c = c_ref[...]
    cond = _silu(c)
    acc = jnp.sum(cond * w_ref[0], axis=0, keepdims=True)
    o_ref[0] = acc + b_ref[0]


def _mod_call(c, mod_w, mod_b, tn=1024):
    depth, d, n = mod_w.shape
    out = pl.pallas_call(
        _mod_kernel,
        grid=(depth, n // tn),
        in_specs=[
            pl.BlockSpec((d, 1), lambda l, j: (0, 0)),
            pl.BlockSpec((1, d, tn), lambda l, j: (l, 0, j)),
            pl.BlockSpec((1, 1, tn), lambda l, j: (l, 0, j)),
        ],
        out_specs=pl.BlockSpec((1, 1, tn), lambda l, j: (l, 0, j)),
        out_shape=jax.ShapeDtypeStruct((depth, 1, n), F32),
        compiler_params=_cparams("arbitrary", "arbitrary"),
        name="mod_proj",
    )(c.reshape(d, 1), mod_w, mod_b.reshape(depth, 1, n))
    return out.reshape(depth, N_MOD, d)


def _attn_inproj_kernel(x_ref, mod_ref, gain_ref, w_ref, qkg_ref, p_ref, pt_ref,
                        q_ref, k_ref, v_ref, wb_ref, *, q_dim, kv_dim, scale, n_sub):
    @pl.when(pl.program_id(0) == 0)
    def _():
        wb_ref[...] = w_ref[...].astype(BF16)

    sub = x_ref.shape[0] // n_sub
    for s in range(n_sub):
        rs = slice(s * sub, (s + 1) * sub)
        h = _norm_mod(x_ref[rs, :], gain_ref[...], mod_ref[0:1, :], mod_ref[1:2, :]).astype(BF16)
        proj = jnp.dot(h, wb_ref[...], preferred_element_type=F32)
        qk = proj[:, :q_dim + kv_dim]
        ss = jnp.dot((qk * qk).astype(BF16), p_ref[...], preferred_element_type=F32)
        inv = lax.rsqrt(ss * (1.0 / HEAD_DIM) + EPS)
        inv_hi = inv.astype(BF16)
        inv_lo = (inv - inv_hi.astype(F32)).astype(BF16)
        inv_b = jnp.dot(jnp.concatenate([inv_hi, inv_lo], axis=1), pt_ref[...],
                        preferred_element_type=F32)
        qkn = qk * inv_b * qkg_ref[...]
        q_ref[rs, :] = (qkn[:, :q_dim] * scale).astype(BF16)
        k_ref[rs, :] = qkn[:, q_dim:].astype(BF16)
        v_ref[rs, :] = proj[:, q_dim + kv_dim:].astype(BF16)


def _attn_inproj_call(x, mod_l, gain, w_all, qk_gain, tm=512, n_sub=2):
    t, d = x.shape
    n = w_all.shape[2]
    kv_dim = (n - d) // 2
    q_dim = d
    n_heads = (q_dim + kv_dim) // HEAD_DIM
    ind = (np.arange(q_dim + kv_dim)[:, None] // HEAD_DIM == np.arange(LANES)[None, :])
    p = jnp.asarray(ind, dtype=BF16)
    pt = jnp.asarray(np.concatenate([ind.T, ind.T], axis=0), dtype=BF16)
    assert n_heads <= LANES
    kern = functools.partial(_attn_inproj_kernel, q_dim=q_dim, kv_dim=kv_dim,
                             scale=LOG2E / math.sqrt(HEAD_DIM), n_sub=n_sub)
    return pl.pallas_call(
        kern,
        grid=(t // tm,),
        in_specs=[
            pl.BlockSpec((tm, d), lambda i: (i, 0)),
            _resident((N_MOD, d)),
            _resident((1, d)),
            pl.BlockSpec((None, d, n), lambda i: (0, 0, 0), pipeline_mode=pl.Buffered(1)),
            _resident((1, q_dim + kv_dim)),
            _resident((q_dim + kv_dim, LANES)),
            _resident((2 * LANES, q_dim + kv_dim)),
        ],
        out_specs=[
            pl.BlockSpec((tm, q_dim), lambda i: (i, 0)),
            pl.BlockSpec((tm, kv_dim), lambda i: (i, 0)),
            pl.BlockSpec((tm, kv_dim), lambda i: (i, 0)),
        ],
        out_shape=[
            jax.ShapeDtypeStruct((t, q_dim), BF16),
            jax.ShapeDtypeStruct((t, kv_dim), BF16),
            jax.ShapeDtypeStruct((t, kv_dim), BF16),
        ],
        scratch_shapes=[pltpu.VMEM((d, n), BF16)],
        compiler_params=_cparams("arbitrary"),
        name="attn_inproj",
    )(x, mod_l, gain, w_all, qk_gain, p, pt)


def _attn_kernel(sink_ref, q_ref, kc_ref, kp_ref, vc_ref, vp_ref, w1_ref, w2_ref,
                 o_ref, w1b_ref, w2b_ref, bias_ref, *, n_kv, n_blk):
    n = pl.program_id(0)
    w1b_ref[...] = w1_ref[...].astype(BF16)
    w2b_ref[...] = w2_ref[...].astype(BF16)
    n_heads = n_kv * GROUP
    row = lax.broadcasted_iota(jnp.int32, (WINDOW, WINDOW), 0)
    col = lax.broadcasted_iota(jnp.int32, (WINDOW, WINDOW), 1)
    in_cur = col <= row

    @pl.when(n == 0)
    def _():
        dist = jnp.where(in_cur, row - col, row - col + WINDOW).astype(F32)
        for hq in range(n_heads):
            slope = LOG2E * 2.0 ** (-8.0 * (hq + 1) / n_heads)
            bias_ref[hq] = -slope * dist
            bias_ref[n_heads + hq] = jnp.where(in_cur, -slope * dist, -jnp.inf)

    lane = lax.broadcasted_iota(jnp.int32, (WINDOW, LANES), 1)
    low = lane < HEAD_DIM
    one_lo = jnp.where(low, 1.0, 0.0).astype(BF16)
    one_hi = jnp.where(low, 0.0, 1.0).astype(BF16)
    den_rhs = jnp.concatenate([one_lo, one_lo, one_hi, one_hi], axis=0)
    zero = jnp.zeros((WINDOW, LANES), BF16)

    def halves(t, g):
        r = pltpu.roll(t, HEAD_DIM, axis=1)
        if g % 2 == 0:
            return jnp.where(low, t, zero), jnp.where(low, zero, r)
        return jnp.where(low, r, zero), jnp.where(low, zero, t)

    for blk, g in [(b_, g_) for b_ in range(n_blk) for g_ in range(n_kv)]:
        qrows = slice(blk * WINDOW, (blk + 1) * WINDOW)
        sl = slice((g // 2) * LANES, (g // 2 + 1) * LANES)
        if blk == 0:
            k_prev, v_prev = kp_ref[:, sl], vp_ref[:, sl]
            bias_off = jnp.where(n == 0, n_heads, 0)
        else:
            prows = slice((blk - 1) * WINDOW, blk * WINDOW)
            k_prev, v_prev = kc_ref[prows, sl], vc_ref[prows, sl]
            bias_off = 0
        kc_lo, kc_hi = halves(kc_ref[qrows, sl], g)
        kp_lo, kp_hi = halves(k_prev, g)
        vc_lo, vc_hi = halves(vc_ref[qrows, sl], g)
        vp_lo, vp_hi = halves(v_prev, g)
        k_rhs = jnp.concatenate([kc_lo, kc_hi, kp_lo, kp_hi], axis=0)
        v_rhs = jnp.concatenate([vc_lo, vp_lo, vc_hi, vp_hi], axis=0)

        n_pairs = GROUP // 2
        q_base = g * GROUP * HEAD_DIM
        qs = jnp.concatenate(
            [q_ref[qrows, q_base + t * LANES: q_base + (t + 1) * LANES] for t in range(n_pairs)], axis=0)
        logits = lax.dot_general(qs, k_rhs, (((1,), (1,)), ((), ())), preferred_element_type=F32)

        pms, sink_terms = [], []
        for t in range(n_pairs):
            rows = slice(t * WINDOW, (t + 1) * WINDOW)
            probs, mx_e = [], []
            for e in range(2):
                hq = g * GROUP + 2 * t + e
                l_cur = logits[rows, e * WINDOW:(e + 1) * WINDOW]
                l_prev = logits[rows, (2 + e) * WINDOW:(3 + e) * WINDOW]
                s = jnp.where(in_cur, l_cur, l_prev) + bias_ref[hq + bias_off]
                mx = jnp.max(s, axis=1, keepdims=True)
                pe = jnp.exp2(s - mx).astype(BF16)
                probs.append(jnp.where(in_cur, pe, zero))
                probs.append(jnp.where(in_cur, zero, pe))
                mx_e.append(mx)
            pms.append(jnp.concatenate(probs, axis=1))
            hq0 = g * GROUP + 2 * t
            sink_l = jnp.where(low[0:1], sink_ref[0, hq0] * LOG2E, sink_ref[0, hq0 + 1] * LOG2E)
            sink_terms.append(jnp.exp2(sink_l - jnp.where(low, mx_e[0], mx_e[1])))
        res = jnp.dot(jnp.concatenate(pms, axis=0), jnp.concatenate([v_rhs, den_rhs], axis=1),
                      preferred_element_type=F32)
        for t in range(n_pairs):
            rows = slice(t * WINDOW, (t + 1) * WINDOW)
            den = res[rows, LANES:] + sink_terms[t]
            o_ref[qrows, q_base + t * LANES: q_base + (t + 1) * LANES] = (res[rows, :LANES] / den).astype(BF16)


def _attn_call(q, k, v, sinks, mlp_w1, mlp_w2, layer, n_blk=2):
    t, q_dim = q.shape
    kv_dim = k.shape[1]
    n_kv = kv_dim // HEAD_DIM
    n_heads = n_kv * GROUP
    assert n_kv % 2 == 0 and q_dim == n_heads * HEAD_DIM
    rows = n_blk * WINDOW
    nb = t // rows
    cur = lambda n: (n, 0)
    prev = lambda n: (jnp.maximum(n * n_blk - 1, 0), 0)
    w1_in, w1_out, w1_shape = _cast_job(mlp_w1, layer, nb, lambda n: n)
    w2_in, w2_out, w2_shape = _cast_job(mlp_w2, layer, nb, lambda n: n)
    return pl.pallas_call(
        functools.partial(_attn_kernel, n_kv=n_kv, n_blk=n_blk),
        grid=(nb,),
        in_specs=[
            pl.BlockSpec(memory_space=pltpu.SMEM),
            pl.BlockSpec((rows, q_dim), cur),
            pl.BlockSpec((rows, kv_dim), cur),
            pl.BlockSpec((WINDOW, kv_dim), prev),
            pl.BlockSpec((rows, kv_dim), cur),
            pl.BlockSpec((WINDOW, kv_dim), prev),
            w1_in, w2_in,
        ],
        out_specs=[pl.BlockSpec((rows, q_dim), cur), w1_out, w2_out],
        out_shape=[jax.ShapeDtypeStruct((t, q_dim), BF16), w1_shape, w2_shape],
        scratch_shapes=[pltpu.VMEM((2 * n_heads, WINDOW, WINDOW), F32)],
        compiler_params=_cparams("arbitrary"),
        name="swa_attention",
    )(sinks.reshape(1, n_heads), q, k, k, v, v, mlp_w1, mlp_w2)


def _outproj_kernel(a_ref, x_ref, mod_ref, w_ref, o_ref, wb_ref, *, gate_row):
    @pl.when(pl.program_id(0) == 0)
    def _():
        wb_ref[...] = w_ref[...].astype(BF16)

    y = jnp.dot(a_ref[...], wb_ref[...], preferred_element_type=F32)
    o_ref[...] = x_ref[...] + mod_ref[gate_row:gate_row + 1, :] * y


def _outproj_call(a, x, mod_l, w_all, gate_row, tm=512):
    t, d = x.shape
    kdim = a.shape[1]
    return pl.pallas_call(
        functools.partial(_outproj_kernel, gate_row=gate_row),
        grid=(t // tm,),
        in_specs=[
            pl.BlockSpec((tm, kdim), lambda i: (i, 0)),
            pl.BlockSpec((tm, d), lambda i: (i, 0)),
            _resident((N_MOD, d)),
            pl.BlockSpec((None, kdim, d), lambda i: (0, 0, 0), pipeline_mode=pl.Buffered(1)),
        ],
        out_specs=pl.BlockSpec((tm, d), lambda i: (i, 0)),
        out_shape=jax.ShapeDtypeStruct((t, d), F32),
        scratch_shapes=[pltpu.VMEM((kdim, d), BF16)],
        compiler_params=_cparams("arbitrary"),
        name="outproj_residual",
    )(a, x, mod_l, w_all)


def _mlp_kernel(*refs, has_side, n_sub):
    if has_side:
        x_ref, mod_ref, gain_ref, w1_ref, w2_ref, side_ref, o_ref, sideb_ref, h_ref = refs
        sideb_ref[...] = side_ref[...].astype(BF16)
    else:
        x_ref, mod_ref, gain_ref, w1_ref, w2_ref, o_ref, h_ref = refs
    f = pl.program_id(1)
    gate = mod_ref[5:6, :]

    def ffn(h):
        a = jnp.maximum(jnp.dot(h, w1_ref[...], preferred_element_type=F32), 0.0)
        return gate * jnp.dot((a * a).astype(BF16), w2_ref[...], preferred_element_type=F32)

    @pl.when(f == 0)
    def _():
        sub = x_ref.shape[0] // n_sub
        for s in range(n_sub):
            rs = slice(s * sub, (s + 1) * sub)
            xs = x_ref[rs, :]
            h = _norm_mod(xs, gain_ref[...], mod_ref[3:4, :], mod_ref[4:5, :]).astype(BF16)
            h_ref[rs, :] = h
            o_ref[rs, :] = xs + ffn(h)

    @pl.when(f > 0)
    def _():
        o_ref[...] += ffn(h_ref[...])


def _mlp_call(x, mod_l, gain, w1, w2, side=None, tm=512, tf=1024, n_sub=2):
    t, d = x.shape
    dff = w1.shape[1]
    nf = dff // tf
    in_specs = [
        pl.BlockSpec((tm, d), lambda i, f: (i, 0)),
        _resident((N_MOD, d)),
        _resident((1, d)),
        pl.BlockSpec((d, tf), lambda i, f: (0, f)),
        pl.BlockSpec((tf, d), lambda i, f: (f, 0)),
    ]
    out_specs = [pl.BlockSpec((tm, d), lambda i, f: (i, 0))]
    out_shape = [jax.ShapeDtypeStruct((t, d), F32)]
    args = [x, mod_l, gain, w1, w2]
    if side is not None:
        arr, in_fn, out_fn, shape = side
        in_specs.append(in_fn(t // tm, nf))
        out_specs.append(out_fn(t // tm, nf))
        out_shape.append(shape)
        args.append(arr)
    outs = pl.pallas_call(
        functools.partial(_mlp_kernel, has_side=side is not None, n_sub=n_sub),
        grid=(t // tm, nf),
        in_specs=in_specs,
        out_specs=out_specs,
        out_shape=out_shape,
        scratch_shapes=[pltpu.VMEM((tm, d), BF16)],
        compiler_params=_cparams("arbitrary", "arbitrary"),
        name="relu2_mlp",
    )(*args)
    return outs if side is not None else outs[0]


def _hgrn_w_in_side_job(hgrn_w_in, layer, n_heads, hpt):
    _, d, n = hgrn_w_in.shape
    cw = hpt * HG_DK
    n_tiles = n_heads // hpt
    n_cb = 4 * n_tiles

    def blocks(n_outer, n_inner):
        n_rb = (n_outer * n_inner) // n_cb
        assert n_rb * n_cb == n_outer * n_inner and d % n_rb == 0 and (d // n_rb) % 16 == 0
        return d // n_rb

    def in_fn(n_outer, n_inner):
        rb = blocks(n_outer, n_inner)

        def imap(i, f):
            step = i * n_inner + f
            out_cb = step % n_cb
            return (layer, step // n_cb, (out_cb % 4) * n_tiles + out_cb // 4)
        return pl.BlockSpec((None, rb, cw), imap)

    def out_fn(n_outer, n_inner):
        rb = blocks(n_outer, n_inner)

        def omap(i, f):
            step = i * n_inner + f
            return (step // n_cb, step % n_cb)
        return pl.BlockSpec((rb, cw), omap)

    return hgrn_w_in, in_fn, out_fn, jax.ShapeDtypeStruct((d, n), BF16)


def _layer_lower_bound(lbl_ref):
    l0 = lbl_ref[0]
    l1 = lbl_ref[1]
    lm = jnp.maximum(l0, l1)
    e0 = jnp.exp(l0 - lm)
    e1 = jnp.exp(l1 - lm)
    p0 = e0 / (e0 + e1)
    p1 = e1 / (e0 + e1)
    return (p0 + p1) - p0


def _hgrn_inproj_kernel(x_ref, mod_ref, gain_ref, lbl_ref, w_ref, side_ref,
                        q_ref, lf_ref, kk_ref, v_ref, g_ref, sideb_ref, h_ref, *, hpt, scale, n_sub):
    j = pl.program_id(1)
    sideb_ref[...] = side_ref[...].astype(BF16)
    lb = _layer_lower_bound(lbl_ref)
    width = hpt * HG_DK

    def project(h, rs):
        res = jnp.dot(h, w_ref[...], preferred_element_type=F32)
        for hh in range(hpt):
            seg = lambda s: res[:, s * width + hh * HG_DK: s * width + (hh + 1) * HG_DK]
            q_ref[hh, rs, :] = (_silu(seg(0)) * scale).astype(BF16)
            forget = lb[hh] + (1.0 - lb[hh]) * _sigmoid(seg(1))
            kk_ref[hh, rs, :] = 1.0 - forget
            lf_ref[hh, rs, :] = jnp.log(forget)
            v_ref[hh, rs, :] = seg(2).astype(BF16)
            g_ref[hh, rs, :] = _silu(seg(3)).astype(BF16)

    @pl.when(j == 0)
    def _():
        sub = x_ref.shape[0] // n_sub
        for s in range(n_sub):
            rs = slice(s * sub, (s + 1) * sub)
            h = _norm_mod(x_ref[rs, :], gain_ref[...], mod_ref[0:1, :], mod_ref[1:2, :]).astype(BF16)
            h_ref[rs, :] = h
            project(h, rs)

    @pl.when(j > 0)
    def _():
        project(h_ref[...], slice(None))


def _hgrn_inproj_call(x, mod_l, gain, lb_logits, w_perm, n_heads, side_w, side_layer, tm=1024, hpt=2):
    t, d = x.shape
    depth = lb_logits.shape[0]
    assert depth == 2
    tn = 4 * hpt * HG_DK
    n_inner = n_heads // hpt
    head_major = lambda dt: jax.ShapeDtypeStruct((n_heads, t, HG_DK), dt)
    ospec = pl.BlockSpec((hpt, tm, HG_DK), lambda i, j: (j, i, 0))
    s_in, s_out, s_shape = _cast_job(side_w, side_layer, (t // tm) * n_inner, lambda i, j: i * n_inner + j)
    return pl.pallas_call(
        functools.partial(_hgrn_inproj_kernel, hpt=hpt, scale=1.0 / math.sqrt(HG_DK), n_sub=4),
        grid=(t // tm, n_inner),
        in_specs=[
            pl.BlockSpec((tm, d), lambda i, j: (i, 0)),
            _resident((N_MOD, d)),
            _resident((1, d)),
            pl.BlockSpec((depth, hpt, 1, HG_DK), lambda i, j: (0, j, 0, 0)),
            pl.BlockSpec((d, tn), lambda i, j: (0, j)),
            s_in,
        ],
        out_specs=[ospec] * 5 + [s_out],
        out_shape=[head_major(BF16), head_major(F32), head_major(F32), head_major(BF16), head_major(BF16),
                   s_shape],
        scratch_shapes=[pltpu.VMEM((tm, d), BF16)],
        compiler_params=_cparams("arbitrary", "arbitrary"),
        name="hgrn_inproj",
    )(x, mod_l, gain, lb_logits.reshape(depth, n_heads, 1, HG_DK), w_perm, side_w)


def _cum_matrix(tr):
    n_chunks = tr // CHUNK
    r = np.arange(tr)[:, None]
    c = np.arange(tr)[None, :]
    same = (r // CHUNK) == (c // CHUNK)
    incl = same & (c <= r)
    pivot = same & (c % CHUNK <= CHUNK // 2 - 1)
    rows = np.zeros((2 * n_chunks, tr), np.float32)
    for ch in range(n_chunks):
        rows[2 * ch, ch * CHUNK: ch * CHUNK + CHUNK // 2] = 1.0
        rows[2 * ch + 1, ch * CHUNK: (ch + 1) * CHUNK] = 1.0
    mats = np.concatenate([incl.astype(np.float32) - pivot.astype(np.float32), rows], axis=0)
    return jnp.asarray(mats, dtype=BF16)


def _hgrn_kernel(gain_ref, cm_ref, q_ref, lf_ref, kk_ref, v_ref, g_ref, side_ref,
                 o_ref, sideb_ref, st_ref, *, hb, tr):
    tstep = pl.program_id(1)
    sideb_ref[...] = side_ref[...].astype(BF16)

    @pl.when(tstep == 0)
    def _():
        st_ref[...] = jnp.zeros_like(st_ref)

    rowc = lax.broadcasted_iota(jnp.int32, (CHUNK, 2 * CHUNK), 0)
    colc = lax.broadcasted_iota(jnp.int32, (CHUNK, 2 * CHUNK), 1)
    causal2 = (colc % CHUNK) <= rowc
    n_chunks = tr // CHUNK
    zc = jnp.zeros((CHUNK, HG_DK), BF16)
    zs = jnp.zeros((HG_DK, HG_DK), BF16)

    def block_diag(a, b, z):
        return jnp.concatenate([jnp.concatenate([a, z], axis=1), jnp.concatenate([z, b], axis=1)], axis=0)

    for pp in range(hb // 2):
        heads = (2 * pp, 2 * pp + 1)
        qe, ke, qb, ku, dec = [], [], [], [], []
        for hh in heads:
            logf = lf_ref[hh]
            p_hi = logf.astype(BF16)
            p_lo = (logf - p_hi.astype(F32)).astype(BF16)
            cums = jnp.dot(cm_ref[...], jnp.concatenate([p_hi, p_lo], axis=1), preferred_element_type=F32)
            cums = cums[:, :HG_DK] + cums[:, HG_DK:]
            bmp = cums[:tr]
            qe_f = q_ref[hh].astype(F32) * jnp.exp(bmp)
            ke_f = kk_ref[hh] * jnp.exp(-bmp)
            qb_h, ku_h, dec_h = [], [], []
            for c in range(n_chunks):
                rs = slice(c * CHUNK, (c + 1) * CHUNK)
                piv = cums[tr + 2 * c: tr + 2 * c + 1]
                blast = cums[tr + 2 * c + 1: tr + 2 * c + 2]
                qb_h.append((qe_f[rs] * jnp.exp(piv)).astype(BF16))
                ku_h.append((ke_f[rs] * jnp.exp(blast - piv)).astype(BF16))
                dec_h.append(jnp.exp(blast))
            qe.append(qe_f.astype(BF16))
            ke.append(ke_f.astype(BF16))
            qb.append(qb_h)
            ku.append(ku_h)
            dec.append(dec_h)

        for c in range(n_chunks):
            rs = slice(c * CHUNK, (c + 1) * CHUNK)
            v0 = v_ref[heads[0], rs, :]
            v1 = v_ref[heads[1], rs, :]
            a = lax.dot_general(jnp.concatenate([qe[0][rs], qe[1][rs]], axis=1),
                                block_diag(ke[0][rs], ke[1][rs], zc),
                                (((1,), (1,)), ((), ())), preferred_element_type=F32)
            a = jnp.where(causal2, a, 0.0).astype(BF16)
            st = st_ref[pp]
            st_b = st.astype(BF16)
            o = (jnp.dot(a, block_diag(v0, v1, zc), preferred_element_type=F32)
                 + lax.dot_general(jnp.concatenate([qb[0][c], qb[1][c]], axis=1),
                                   block_diag(st_b[:, :HG_DK], st_b[:, HG_DK:], zs),
                                   (((1,), (1,)), ((), ())), preferred_element_type=F32))
            upd = lax.dot_general(jnp.concatenate([v0, v1], axis=0), block_diag(ku[0][c], ku[1][c], zc),
                                  (((0,), (0,)), ((), ())), preferred_element_type=F32)
            st_ref[pp] = st * jnp.concatenate([dec[0][c], dec[1][c]], axis=1) + upd
            for e, hh in enumerate(heads):
                oh = o[:, e * HG_DK:(e + 1) * HG_DK]
                on = oh * lax.rsqrt(jnp.mean(oh * oh, axis=-1, keepdims=True) + EPS) * gain_ref[hh]
                o_ref[rs, hh * HG_DK:(hh + 1) * HG_DK] = (on * g_ref[hh, rs, :].astype(F32)).astype(BF16)


def _hgrn_call(q, lf, kk, v, g, o_gain, side_w, side_layer, hb=8, tr=256):
    n_heads, t, dk = q.shape
    assert dk == HG_DK and hb % 2 == 0
    cm = _cum_matrix(tr)
    n_inner = t // tr
    blk = pl.BlockSpec((hb, tr, dk), lambda h, s: (h, s, 0))
    s_in, s_out, s_shape = _cast_job(side_w, side_layer, (n_heads // hb) * n_inner,
                                     lambda h, s: h * n_inner + s)
    return pl.pallas_call(
        functools.partial(_hgrn_kernel, hb=hb, tr=tr),
        grid=(n_heads // hb, n_inner),
        in_specs=[
            pl.BlockSpec((hb, 1, dk), lambda h, s: (h, 0, 0)),
            _resident(cm.shape),
            blk, blk, blk, blk, blk,
            s_in,
        ],
        out_specs=[pl.BlockSpec((tr, hb * dk), lambda h, s: (s, h)), s_out],
        out_shape=[jax.ShapeDtypeStruct((t, n_heads * dk), BF16), s_shape],
        scratch_shapes=[pltpu.VMEM((hb // 2, dk, 2 * dk), F32)],
        compiler_params=_cparams("arbitrary", "arbitrary"),
        name="hgrn_recurrence",
    )(o_gain.reshape(n_heads, 1, dk), cm, q, lf, kk, v, g, side_w)


def kernel(x, c, mod_w, mod_b, norm_mix, norm_mlp, attn_w_in, attn_w_out, attn_q_gain, attn_k_gain,
           attn_sinks, hgrn_w_in, hgrn_w_out, hgrn_o_gain, hgrn_lb_logits, mlp_w1, mlp_w2):
    b, t, d = x.shape
    assert b == 1 and mod_w.shape[0] == 2
    xs = x.reshape(t, d)

    mod = _mod_call(c, mod_w, mod_b)

    q_dim = attn_w_out.shape[1]
    kv_dim = (attn_w_in.shape[2] - q_dim) // 2
    qk_gain = jnp.concatenate([jnp.tile(attn_q_gain[0], q_dim // HEAD_DIM),
                               jnp.tile(attn_k_gain[0], kv_dim // HEAD_DIM)]).reshape(1, q_dim + kv_dim)
    q, k, v = _attn_inproj_call(xs, mod[0], norm_mix[0].reshape(1, d), attn_w_in, qk_gain)
    att, w1b, w2b = _attn_call(q, k, v, attn_sinks[0], mlp_w1, mlp_w2, layer=0)
    xs = _outproj_call(att, xs, mod[0], attn_w_out, gate_row=2)
    n_heads = hgrn_o_gain.shape[1]
    hpt = 2
    xs, hw_perm = _mlp_call(xs, mod[0], norm_mlp[0].reshape(1, d), w1b, w2b,
                            side=_hgrn_w_in_side_job(hgrn_w_in, 0, n_heads, hpt))

    hq, hlf, hkk, hv, hg, w1b = _hgrn_inproj_call(xs, mod[1], norm_mix[1].reshape(1, d), hgrn_lb_logits,
                                                  hw_perm, n_heads, mlp_w1, 1, hpt=hpt)
    ho, w2b = _hgrn_call(hq, hlf, hkk, hv, hg, hgrn_o_gain[0], mlp_w2, 1)
    xs = _outproj_call(ho, xs, mod[1], hgrn_w_out, gate_row=2)
    xs = _mlp_call(xs, mod[1], norm_mlp[1].reshape(1, d), w1b, w2b)
    return xs.reshape(b, t, d)
```

```python
import functools
import math

import numpy as np
import jax
import jax.numpy as jnp
from jax import lax
from jax.experimental import pallas as pl
from jax.experimental.pallas import tpu as pltpu

F32 = jnp.float32
BF16 = jnp.bfloat16

EPS = 1e-6
N_MOD = 6

HEAD_DIM = 64
GROUP = 8
WINDOW = 128
LANES = 128
LOG2E = math.log2(math.e)

HG_DK = 128
CHUNK = 64

VMEM_LIMIT = 56 * 1024 * 1024


def _cparams(*sem):
    return pltpu.CompilerParams(dimension_semantics=sem, vmem_limit_bytes=VMEM_LIMIT)


def _resident(shape):
    nd = len(shape)
    return pl.BlockSpec(shape, lambda *_: (0,) * nd, pipeline_mode=pl.Buffered(1))


def _cast_job(w, layer, n_steps, step_of):
    _, r, c = w.shape
    rb = r // n_steps
    assert rb * n_steps == r and rb % 16 == 0
    in_spec = pl.BlockSpec((None, rb, c), lambda *g: (layer, step_of(*g), 0))
    out_spec = pl.BlockSpec((rb, c), lambda *g: (step_of(*g), 0))
    return in_spec, out_spec, jax.ShapeDtypeStruct((r, c), BF16)


def _sigmoid(v):
    return 0.5 * jnp.tanh(0.5 * v) + 0.5


def _silu(v):
    return v * _sigmoid(v)


def _norm_mod(x, gain, shift, scale):
    ms = jnp.mean(x * x, axis=-1, keepdims=True)
    y = x * lax.rsqrt(ms + EPS) * gain
    return y * (1.0 + scale) + shift


def _mod_kernel(c_ref, w_ref, b_ref, o_ref):
    c = c_ref[...]
    cond = _silu(c)
    acc = jnp.sum(cond * w_ref[0], axis=0, keepdims=True)
    o_ref[0] = acc + b_ref[0]


def _mod_call(c, mod_w, mod_b, tn=1024):
    depth, d, n = mod_w.shape
    out = pl.pallas_call(
        _mod_kernel,
        grid=(depth, n // tn),
        in_specs=[
            pl.BlockSpec((d, 1), lambda l, j: (0, 0)),
            pl.BlockSpec((1, d, tn), lambda l, j: (l, 0, j)),
            pl.BlockSpec((1, 1, tn), lambda l, j: (l, 0, j)),
        ],
        out_specs=pl.BlockSpec((1, 1, tn), lambda l, j: (l, 0, j)),
        out_shape=jax.ShapeDtypeStruct((depth, 1, n), F32),
        compiler_params=_cparams("arbitrary", "arbitrary"),
        name="mod_proj",
    )(c.reshape(d, 1), mod_w, mod_b.reshape(depth, 1, n))
    return out.reshape(depth, N_MOD, d)


def _attn_inproj_kernel(x_ref, mod_ref, gain_ref, w_ref, qkg_ref, p_ref, pt_ref, side_ref,
                        q_ref, k_ref, v_ref, sideb_ref, wb_ref, *, q_dim, kv_dim, scale, n_sub):
    sideb_ref[...] = side_ref[...].astype(BF16)

    @pl.when(pl.program_id(0) == 0)
    def _():
        wb_ref[...] = w_ref[...].astype(BF16)

    sub = x_ref.shape[0] // n_sub
    for s in range(n_sub):
        rs = slice(s * sub, (s + 1) * sub)
        h = _norm_mod(x_ref[rs, :], gain_ref[...], mod_ref[0:1, :], mod_ref[1:2, :]).astype(BF16)
        proj = jnp.dot(h, wb_ref[...], preferred_element_type=F32)
        qk = proj[:, :q_dim + kv_dim]
        ss = jnp.dot((qk * qk).astype(BF16), p_ref[...], preferred_element_type=F32)
        inv = lax.rsqrt(ss * (1.0 / HEAD_DIM) + EPS)
        inv_hi = inv.astype(BF16)
        inv_lo = (inv - inv_hi.astype(F32)).astype(BF16)
        inv_b = jnp.dot(jnp.concatenate([inv_hi, inv_lo], axis=1), pt_ref[...],
                        preferred_element_type=F32)
        qkn = qk * inv_b * qkg_ref[...]
        q_ref[rs, :] = (qkn[:, :q_dim] * scale).astype(BF16)
        k_ref[rs, :] = qkn[:, q_dim:].astype(BF16)
        v_ref[rs, :] = proj[:, q_dim + kv_dim:].astype(BF16)


def _attn_inproj_call(x, mod_l, gain, w_all, qk_gain, side_w, side_layer, tm=512, n_sub=2):
    t, d = x.shape
    s_in, s_out, s_shape = _cast_job(side_w, side_layer, t // tm, lambda i: i)
    n = w_all.shape[2]
    kv_dim = (n - d) // 2
    q_dim = d
    n_heads = (q_dim + kv_dim) // HEAD_DIM
    ind = (np.arange(q_dim + kv_dim)[:, None] // HEAD_DIM == np.arange(LANES)[None, :])
    p = jnp.asarray(ind, dtype=BF16)
    pt = jnp.asarray(np.concatenate([ind.T, ind.T], axis=0), dtype=BF16)
    assert n_heads <= LANES
    kern = functools.partial(_attn_inproj_kernel, q_dim=q_dim, kv_dim=kv_dim,
                             scale=LOG2E / math.sqrt(HEAD_DIM), n_sub=n_sub)
    return pl.pallas_call(
        kern,
        grid=(t // tm,),
        in_specs=[
            pl.BlockSpec((tm, d), lambda i: (i, 0)),
            _resident((N_MOD, d)),
            _resident((1, d)),
            pl.BlockSpec((None, d, n), lambda i: (0, 0, 0), pipeline_mode=pl.Buffered(1)),
            _resident((1, q_dim + kv_dim)),
            _resident((q_dim + kv_dim, LANES)),
            _resident((2 * LANES, q_dim + kv_dim)),
            s_in,
        ],
        out_specs=[
            pl.BlockSpec((tm, q_dim), lambda i: (i, 0)),
            pl.BlockSpec((tm, kv_dim), lambda i: (i, 0)),
            pl.BlockSpec((tm, kv_dim), lambda i: (i, 0)),
            s_out,
        ],
        out_shape=[
            jax.ShapeDtypeStruct((t, q_dim), BF16),
            jax.ShapeDtypeStruct((t, kv_dim), BF16),
            jax.ShapeDtypeStruct((t, kv_dim), BF16),
            s_shape,
        ],
        scratch_shapes=[pltpu.VMEM((d, n), BF16)],
        compiler_params=_cparams("arbitrary"),
        name="attn_inproj",
    )(x, mod_l, gain, w_all, qk_gain, p, pt, side_w)


def _attn_kernel(sink_ref, q_ref, kc_ref, kp_ref, vc_ref, vp_ref, w1_ref, w2_ref,
                 o_ref, w1b_ref, w2b_ref, bias_ref, *, n_kv, n_blk):
    n = pl.program_id(0)
    w1b_ref[...] = w1_ref[...].astype(BF16)
    w2b_ref[...] = w2_ref[...].astype(BF16)
    n_heads = n_kv * GROUP
    row = lax.broadcasted_iota(jnp.int32, (WINDOW, WINDOW), 0)
    col = lax.broadcasted_iota(jnp.int32, (WINDOW, WINDOW), 1)
    in_cur = col <= row

    @pl.when(n == 0)
    def _():
        dist = jnp.where(in_cur, row - col, row - col + WINDOW).astype(F32)
        for hq in range(n_heads):
            slope = LOG2E * 2.0 ** (-8.0 * (hq + 1) / n_heads)
            bias_ref[hq] = -slope * dist
            bias_ref[n_heads + hq] = jnp.where(in_cur, -slope * dist, -jnp.inf)

    lane = lax.broadcasted_iota(jnp.int32, (WINDOW, LANES), 1)
    low = lane < HEAD_DIM
    one_lo = jnp.where(low, 1.0, 0.0).astype(BF16)
    one_hi = jnp.where(low, 0.0, 1.0).astype(BF16)
    den_rhs = jnp.concatenate([one_lo, one_lo, one_hi, one_hi], axis=0)
    zero = jnp.zeros((WINDOW, LANES), BF16)

    def halves(t, g):
        r = pltpu.roll(t, HEAD_DIM, axis=1)
        if g % 2 == 0:
            return jnp.where(low, t, zero), jnp.where(low, zero, r)
        return jnp.where(low, r, zero), jnp.where(low, zero, t)

    for blk, g in [(b_, g_) for b_ in range(n_blk) for g_ in range(n_kv)]:
        qrows = slice(blk * WINDOW, (blk + 1) * WINDOW)
        sl = slice((g // 2) * LANES, (g // 2 + 1) * LANES)
        if blk == 0:
            k_prev, v_prev = kp_ref[:, sl], vp_ref[:, sl]
            bias_off = jnp.where(n == 0, n_heads, 0)
        else:
            prows = slice((blk - 1) * WINDOW, blk * WINDOW)
            k_prev, v_prev = kc_ref[prows, sl], vc_ref[prows, sl]
            bias_off = 0
        kc_lo, kc_hi = halves(kc_ref[qrows, sl], g)
        kp_lo, kp_hi = halves(k_prev, g)
        vc_lo, vc_hi = halves(vc_ref[qrows, sl], g)
        vp_lo, vp_hi = halves(v_prev, g)
        k_rhs = jnp.concatenate([kc_lo, kc_hi, kp_lo, kp_hi], axis=0)
        v_rhs = jnp.concatenate([vc_lo, vp_lo, vc_hi, vp_hi], axis=0)

        n_pairs = GROUP // 2
        q_base = g * GROUP * HEAD_DIM
        qs = jnp.concatenate(
            [q_ref[qrows, q_base + t * LANES: q_base + (t + 1) * LANES] for t in range(n_pairs)], axis=0)
        logits = lax.dot_general(qs, k_rhs, (((1,), (1,)), ((), ())), preferred_element_type=F32)

        pms, sink_terms = [], []
        for t in range(n_pairs):
            rows = slice(t * WINDOW, (t + 1) * WINDOW)
            probs, mx_e = [], []
            for e in range(2):
                hq = g * GROUP + 2 * t + e
                l_cur = logits[rows, e * WINDOW:(e + 1) * WINDOW]
                l_prev = logits[rows, (2 + e) * WINDOW:(3 + e) * WINDOW]
                s = jnp.where(in_cur, l_cur, l_prev) + bias_ref[hq + bias_off]
                mx = jnp.max(s, axis=1, keepdims=True)
                pe = jnp.exp2(s - mx).astype(BF16)
                probs.append(jnp.where(in_cur, pe, zero))
                probs.append(jnp.where(in_cur, zero, pe))
                mx_e.append(mx)
            pms.append(jnp.concatenate(probs, axis=1))
            hq0 = g * GROUP + 2 * t
            sink_l = jnp.where(low[0:1], sink_ref[0, hq0] * LOG2E, sink_ref[0, hq0 + 1] * LOG2E)
            sink_terms.append(jnp.exp2(sink_l - jnp.where(low, mx_e[0], mx_e[1])))
        res = jnp.dot(jnp.concatenate(pms, axis=0), jnp.concatenate([v_rhs, den_rhs], axis=1),
                      preferred_element_type=F32)
        for t in range(n_pairs):
            rows = slice(t * WINDOW, (t + 1) * WINDOW)
            den = res[rows, LANES:] + sink_terms[t]
            o_ref[qrows, q_base + t * LANES: q_base + (t + 1) * LANES] = (res[rows, :LANES] / den).astype(BF16)


def _attn_call(q, k, v, sinks, mlp_w1, mlp_w2, layer, n_blk=2):
    t, q_dim = q.shape
    kv_dim = k.shape[1]
    n_kv = kv_dim // HEAD_DIM
    n_heads = n_kv * GROUP
    assert n_kv % 2 == 0 and q_dim == n_heads * HEAD_DIM
    rows = n_blk * WINDOW
    nb = t // rows
    cur = lambda n: (n, 0)
    prev = lambda n: (jnp.maximum(n * n_blk - 1, 0), 0)
    w1_in, w1_out, w1_shape = _cast_job(mlp_w1, layer, nb, lambda n: n)
    w2_in, w2_out, w2_shape = _cast_job(mlp_w2, layer, nb, lambda n: n)
    return pl.pallas_call(
        functools.partial(_attn_kernel, n_kv=n_kv, n_blk=n_blk),
        grid=(nb,),
        in_specs=[
            pl.BlockSpec(memory_space=pltpu.SMEM),
            pl.BlockSpec((rows, q_dim), cur),
            pl.BlockSpec((rows, kv_dim), cur),
            pl.BlockSpec((WINDOW, kv_dim), prev),
            pl.BlockSpec((rows, kv_dim), cur),
            pl.BlockSpec((WINDOW, kv_dim), prev),
            w1_in, w2_in,
        ],
        out_specs=[pl.BlockSpec((rows, q_dim), cur), w1_out, w2_out],
        out_shape=[jax.ShapeDtypeStruct((t, q_dim), BF16), w1_shape, w2_shape],
        scratch_shapes=[pltpu.VMEM((2 * n_heads, WINDOW, WINDOW), F32)],
        compiler_params=_cparams("arbitrary"),
        name="swa_attention",
    )(sinks.reshape(1, n_heads), q, k, k, v, v, mlp_w1, mlp_w2)


def _mlp_kernel(*refs, has_side, n_sub):
    if has_side:
        x_ref, a_ref, mod_ref, gain_ref, wo_ref, w1_ref, w2_ref, side_ref, o_ref, sideb_ref, h_ref = refs
        sideb_ref[...] = side_ref[...].astype(BF16)
    else:
        x_ref, a_ref, mod_ref, gain_ref, wo_ref, w1_ref, w2_ref, o_ref, h_ref = refs
    f = pl.program_id(1)
    gate = mod_ref[5:6, :]

    def ffn(h):
        a = jnp.maximum(jnp.dot(h, w1_ref[...], preferred_element_type=F32), 0.0)
        return gate * jnp.dot((a * a).astype(BF16), w2_ref[...], preferred_element_type=F32)

    @pl.when(f == 0)
    def _():
        sub = x_ref.shape[0] // n_sub
        for s in range(n_sub):
            rs = slice(s * sub, (s + 1) * sub)
            x1 = x_ref[rs, :] + mod_ref[2:3, :] * jnp.dot(a_ref[rs, :], wo_ref[...], preferred_element_type=F32)
            h = _norm_mod(x1, gain_ref[...], mod_ref[3:4, :], mod_ref[4:5, :]).astype(BF16)
            h_ref[rs, :] = h
            o_ref[rs, :] = x1 + ffn(h)

    @pl.when(f > 0)
    def _():
        o_ref[...] += ffn(h_ref[...])


def _mlp_call(x, a, mod_l, gain, w_out, w1, w2, side=None, tm=512, tf=1024, n_sub=2):
    t, d = x.shape
    kdim = a.shape[1]
    dff = w1.shape[1]
    nf = dff // tf
    in_specs = [
        pl.BlockSpec((tm, d), lambda i, f: (i, 0)),
        pl.BlockSpec((tm, kdim), lambda i, f: (i, 0)),
        _resident((N_MOD, d)),
        _resident((1, d)),
        _resident((kdim, d)),
        pl.BlockSpec((d, tf), lambda i, f: (0, f)),
        pl.BlockSpec((tf, d), lambda i, f: (f, 0)),
    ]
    out_specs = [pl.BlockSpec((tm, d), lambda i, f: (i, 0))]
    out_shape = [jax.ShapeDtypeStruct((t, d), F32)]
    args = [x, a, mod_l, gain, w_out, w1, w2]
    if side is not None:
        arr, in_fn, out_fn, shape = side
        in_specs.append(in_fn(t // tm, nf))
        out_specs.append(out_fn(t // tm, nf))
        out_shape.append(shape)
        args.append(arr)
    outs = pl.pallas_call(
        functools.partial(_mlp_kernel, has_side=side is not None, n_sub=n_sub),
        grid=(t // tm, nf),
        in_specs=in_specs,
        out_specs=out_specs,
        out_shape=out_shape,
        scratch_shapes=[pltpu.VMEM((tm, d), BF16)],
        compiler_params=_cparams("arbitrary", "arbitrary"),
        name="outproj_relu2_mlp",
    )(*args)
    return outs if side is not None else outs[0]


def _hgrn_w_in_side_job(hgrn_w_in, layer, n_heads, hpt):
    _, d, n = hgrn_w_in.shape
    cw = hpt * HG_DK
    n_tiles = n_heads // hpt
    n_cb = 4 * n_tiles

    def blocks(n_outer, n_inner):
        n_rb = (n_outer * n_inner) // n_cb
        assert n_rb * n_cb == n_outer * n_inner and d % n_rb == 0 and (d // n_rb) % 16 == 0
        return d // n_rb

    def in_fn(n_outer, n_inner):
        rb = blocks(n_outer, n_inner)

        def imap(i, f):
            step = i * n_inner + f
            out_cb = step % n_cb
            return (layer, step // n_cb, (out_cb % 4) * n_tiles + out_cb // 4)
        return pl.BlockSpec((None, rb, cw), imap)

    def out_fn(n_outer, n_inner):
        rb = blocks(n_outer, n_inner)

        def omap(i, f):
            step = i * n_inner + f
            return (step // n_cb, step % n_cb)
        return pl.BlockSpec((rb, cw), omap)

    return hgrn_w_in, in_fn, out_fn, jax.ShapeDtypeStruct((d, n), BF16)


def _layer_lower_bound(lbl_ref):
    l0 = lbl_ref[0]
    l1 = lbl_ref[1]
    lm = jnp.maximum(l0, l1)
    e0 = jnp.exp(l0 - lm)
    e1 = jnp.exp(l1 - lm)
    p0 = e0 / (e0 + e1)
    p1 = e1 / (e0 + e1)
    return (p0 + p1) - p0


def _hgrn_inproj_kernel(x_ref, mod_ref, gain_ref, lbl_ref, w_ref, side_ref, side2_ref,
                        q_ref, lf_ref, kk_ref, v_ref, g_ref, sideb_ref, side2b_ref, h_ref, *, hpt, scale, n_sub):
    j = pl.program_id(1)
    sideb_ref[...] = side_ref[...].astype(BF16)
    side2b_ref[...] = side2_ref[...].astype(BF16)
    lb = _layer_lower_bound(lbl_ref)
    width = hpt * HG_DK

    def project(h, rs):
        res = jnp.dot(h, w_ref[...], preferred_element_type=F32)
        for hh in range(hpt):
            seg = lambda s: res[:, s * width + hh * HG_DK: s * width + (hh + 1) * HG_DK]
            q_ref[hh, rs, :] = (_silu(seg(0)) * scale).astype(BF16)
            forget = lb[hh] + (1.0 - lb[hh]) * _sigmoid(seg(1))
            kk_ref[hh, rs, :] = 1.0 - forget
            lf_ref[hh, rs, :] = jnp.log(forget)
            v_ref[hh, rs, :] = seg(2).astype(BF16)
            g_ref[hh, rs, :] = _silu(seg(3)).astype(BF16)

    @pl.when(j == 0)
    def _():
        sub = x_ref.shape[0] // n_sub
        for s in range(n_sub):
            rs = slice(s * sub, (s + 1) * sub)
            h = _norm_mod(x_ref[rs, :], gain_ref[...], mod_ref[0:1, :], mod_ref[1:2, :]).astype(BF16)
            h_ref[rs, :] = h
            project(h, rs)

    @pl.when(j > 0)
    def _():
        project(h_ref[...], slice(None))


def _hgrn_inproj_call(x, mod_l, gain, lb_logits, w_perm, n_heads, side_w, side_layer, side2_w, side2_layer,
                      tm=1024, hpt=2):
    t, d = x.shape
    depth = lb_logits.shape[0]
    assert depth == 2
    tn = 4 * hpt * HG_DK
    n_inner = n_heads // hpt
    head_major = lambda dt: jax.ShapeDtypeStruct((n_heads, t, HG_DK), dt)
    ospec = pl.BlockSpec((hpt, tm, HG_DK), lambda i, j: (j, i, 0))
    s_in, s_out, s_shape = _cast_job(side_w, side_layer, (t // tm) * n_inner, lambda i, j: i * n_inner + j)
    s2_in, s2_out, s2_shape = _cast_job(side2_w, side2_layer, (t // tm) * n_inner,
                                        lambda i, j: i * n_inner + j)
    return pl.pallas_call(
        functools.partial(_hgrn_inproj_kernel, hpt=hpt, scale=1.0 / math.sqrt(HG_DK), n_sub=4),
        grid=(t // tm, n_inner),
        in_specs=[
            pl.BlockSpec((tm, d), lambda i, j: (i, 0)),
            _resident((N_MOD, d)),
            _resident((1, d)),
            pl.BlockSpec((depth, hpt, 1, HG_DK), lambda i, j: (0, j, 0, 0)),
            pl.BlockSpec((d, tn), lambda i, j: (0, j)),
            s_in, s2_in,
        ],
        out_specs=[ospec] * 5 + [s_out, s2_out],
        out_shape=[head_major(BF16), head_major(F32), head_major(F32), head_major(BF16), head_major(BF16),
                   s_shape, s2_shape],
        scratch_shapes=[pltpu.VMEM((tm, d), BF16)],
        compiler_params=_cparams("arbitrary", "arbitrary"),
        name="hgrn_inproj",
    )(x, mod_l, gain, lb_logits.reshape(depth, n_heads, 1, HG_DK), w_perm, side_w, side2_w)


def _cum_matrix(tr):
    n_chunks = tr // CHUNK
    r = np.arange(tr)[:, None]
    c = np.arange(tr)[None, :]
    same = (r // CHUNK) == (c // CHUNK)
    incl = same & (c <= r)
    pivot = same & (c % CHUNK <= CHUNK // 2 - 1)
    rows = np.zeros((2 * n_chunks, tr), np.float32)
    for ch in range(n_chunks):
        rows[2 * ch, ch * CHUNK: ch * CHUNK + CHUNK // 2] = 1.0
        rows[2 * ch + 1, ch * CHUNK: (ch + 1) * CHUNK] = 1.0
    mats = np.concatenate([incl.astype(np.float32) - pivot.astype(np.float32), rows], axis=0)
    return jnp.asarray(mats, dtype=BF16)


def _hgrn_kernel(gain_ref, cm_ref, q_ref, lf_ref, kk_ref, v_ref, g_ref, side_ref,
                 o_ref, sideb_ref, st_ref, *, hb, tr):
    tstep = pl.program_id(1)
    sideb_ref[...] = side_ref[...].astype(BF16)

    @pl.when(tstep == 0)
    def _():
        st_ref[...] = jnp.zeros_like(st_ref)

    rowc = lax.broadcasted_iota(jnp.int32, (CHUNK, 2 * CHUNK), 0)
    colc = lax.broadcasted_iota(jnp.int32, (CHUNK, 2 * CHUNK), 1)
    causal2 = (colc % CHUNK) <= rowc
    n_chunks = tr // CHUNK
    zc = jnp.zeros((CHUNK, HG_DK), BF16)
    zs = jnp.zeros((HG_DK, HG_DK), BF16)

    def block_diag(a, b, z):
        return jnp.concatenate([jnp.concatenate([a, z], axis=1), jnp.concatenate([z, b], axis=1)], axis=0)

    for pp in range(hb // 2):
        heads = (2 * pp, 2 * pp + 1)
        qe, ke, qb, ku, dec = [], [], [], [], []
        for hh in heads:
            logf = lf_ref[hh]
            p_hi = logf.astype(BF16)
            p_lo = (logf - p_hi.astype(F32)).astype(BF16)
            cums = jnp.dot(cm_ref[...], jnp.concatenate([p_hi, p_lo], axis=1), preferred_element_type=F32)
            cums = cums[:, :HG_DK] + cums[:, HG_DK:]
            bmp = cums[:tr]
            qe_f = q_ref[hh].astype(F32) * jnp.exp(bmp)
            ke_f = kk_ref[hh] * jnp.exp(-bmp)
            qb_h, ku_h, dec_h = [], [], []
            for c in range(n_chunks):
                rs = slice(c * CHUNK, (c + 1) * CHUNK)
                piv = cums[tr + 2 * c: tr + 2 * c + 1]
                blast = cums[tr + 2 * c + 1: tr + 2 * c + 2]
                qb_h.append((qe_f[rs] * jnp.exp(piv)).astype(BF16))
                ku_h.append((ke_f[rs] * jnp.exp(blast - piv)).astype(BF16))
                dec_h.append(jnp.exp(blast))
            qe.append(qe_f.astype(BF16))
            ke.append(ke_f.astype(BF16))
            qb.append(qb_h)
            ku.append(ku_h)
            dec.append(dec_h)

        for c in range(n_chunks):
            rs = slice(c * CHUNK, (c + 1) * CHUNK)
            v0 = v_ref[heads[0], rs, :]
            v1 = v_ref[heads[1], rs, :]
            a = lax.dot_general(jnp.concatenate([qe[0][rs], qe[1][rs]], axis=1),
                                block_diag(ke[0][rs], ke[1][rs], zc),
                                (((1,), (1,)), ((), ())), preferred_element_type=F32)
            a = jnp.where(causal2, a, 0.0).astype(BF16)
            st = st_ref[pp]
            st_b = st.astype(BF16)
            o = (jnp.dot(a, block_diag(v0, v1, zc), preferred_element_type=F32)
                 + lax.dot_general(jnp.concatenate([qb[0][c], qb[1][c]], axis=1),
                                   block_diag(st_b[:, :HG_DK], st_b[:, HG_DK:], zs),
                                   (((1,), (1,)), ((), ())), preferred_element_type=F32))
            upd = lax.dot_general(jnp.concatenate([v0, v1], axis=0), block_diag(ku[0][c], ku[1][c], zc),
                                  (((0,), (0,)), ((), ())), preferred_element_type=F32)
            st_ref[pp] = st * jnp.concatenate([dec[0][c], dec[1][c]], axis=1) + upd
            for e, hh in enumerate(heads):
                oh = o[:, e * HG_DK:(e + 1) * HG_DK]
                on = oh * lax.rsqrt(jnp.mean(oh * oh, axis=-1, keepdims=True) + EPS) * gain_ref[hh]
                o_ref[rs, hh * HG_DK:(hh + 1) * HG_DK] = (on * g_ref[hh, rs, :].astype(F32)).astype(BF16)


def _hgrn_call(q, lf, kk, v, g, o_gain, side_w, side_layer, hb=8, tr=256):
    n_heads, t, dk = q.shape
    assert dk == HG_DK and hb % 2 == 0
    cm = _cum_matrix(tr)
    n_inner = t // tr
    blk = pl.BlockSpec((hb, tr, dk), lambda h, s: (h, s, 0))
    s_in, s_out, s_shape = _cast_job(side_w, side_layer, (n_heads // hb) * n_inner,
                                     lambda h, s: h * n_inner + s)
    return pl.pallas_call(
        functools.partial(_hgrn_kernel, hb=hb, tr=tr),
        grid=(n_heads // hb, n_inner),
        in_specs=[
            pl.BlockSpec((hb, 1, dk), lambda h, s: (h, 0, 0)),
            _resident(cm.shape),
            blk, blk, blk, blk, blk,
            s_in,
        ],
        out_specs=[pl.BlockSpec((tr, hb * dk), lambda h, s: (s, h)), s_out],
        out_shape=[jax.ShapeDtypeStruct((t, n_heads * dk), BF16), s_shape],
        scratch_shapes=[pltpu.VMEM((hb // 2, dk, 2 * dk), F32)],
        compiler_params=_cparams("arbitrary", "arbitrary"),
        name="hgrn_recurrence",
    )(o_gain.reshape(n_heads, 1, dk), cm, q, lf, kk, v, g, side_w)


def kernel(x, c, mod_w, mod_b, norm_mix, norm_mlp, attn_w_in, attn_w_out, attn_q_gain, attn_k_gain,
           attn_sinks, hgrn_w_in, hgrn_w_out, hgrn_o_gain, hgrn_lb_logits, mlp_w1, mlp_w2):
    b, t, d = x.shape
    assert b == 1 and mod_w.shape[0] == 2
    xs = x.reshape(t, d)

    mod = _mod_call(c, mod_w, mod_b)

    q_dim = attn_w_out.shape[1]
    kv_dim = (attn_w_in.shape[2] - q_dim) // 2
    qk_gain = jnp.concatenate([jnp.tile(attn_q_gain[0], q_dim // HEAD_DIM),
                               jnp.tile(attn_k_gain[0], kv_dim // HEAD_DIM)]).reshape(1, q_dim + kv_dim)
    q, k, v, wob = _attn_inproj_call(xs, mod[0], norm_mix[0].reshape(1, d), attn_w_in, qk_gain, attn_w_out, 0)
    att, w1b, w2b = _attn_call(q, k, v, attn_sinks[0], mlp_w1, mlp_w2, layer=0)
    n_heads = hgrn_o_gain.shape[1]
    hpt = 2
    xs, hw_perm = _mlp_call(xs, att, mod[0], norm_mlp[0].reshape(1, d), wob, w1b, w2b,
                            side=_hgrn_w_in_side_job(hgrn_w_in, 0, n_heads, hpt))

    hq, hlf, hkk, hv, hg, w1b, wob = _hgrn_inproj_call(xs, mod[1], norm_mix[1].reshape(1, d), hgrn_lb_logits,
                                                       hw_perm, n_heads, mlp_w1, 1, hgrn_w_out, 0, hpt=hpt)
    ho, w2b = _hgrn_call(hq, hlf, hkk, hv, hg, hgrn_o_gain[0], mlp_w2, 1)
    xs = _mlp_call(xs, ho, mod[1], norm_mlp[1].reshape(1, d), wob, w1b, w2b)
    return xs.reshape(b, t, d)
```

```python
import functools
import math

import numpy as np
import jax
import jax.numpy as jnp
from jax import lax
from jax.experimental import pallas as pl
from jax.experimental.pallas import tpu as pltpu

F32 = jnp.float32
BF16 = jnp.bfloat16

EPS = 1e-6
N_MOD = 6

HEAD_DIM = 64
GROUP = 8
WINDOW = 128
LANES = 128
LOG2E = math.log2(math.e)

HG_DK = 128
CHUNK = 64

VMEM_LIMIT = 56 * 1024 * 1024


def _cparams(*sem):
    return pltpu.CompilerParams(dimension_semantics=sem, vmem_limit_bytes=VMEM_LIMIT)


def _resident(shape):
    nd = len(shape)
    return pl.BlockSpec(shape, lambda *_: (0,) * nd, pipeline_mode=pl.Buffered(1))


def _cast_job(w, layer, n_steps, step_of):
    _, r, c = w.shape
    rb = r // n_steps
    assert rb * n_steps == r and rb % 16 == 0
    in_spec = pl.BlockSpec((None, rb, c), lambda *g: (layer, step_of(*g), 0))
    out_spec = pl.BlockSpec((rb, c), lambda *g: (step_of(*g), 0))
    return in_spec, out_spec, jax.ShapeDtypeStruct((r, c), BF16)


def _sigmoid(v):
    return 0.5 * jnp.tanh(0.5 * v) + 0.5


def _silu(v):
    return v * _sigmoid(v)


def _norm_mod(x, gain, shift, scale):
    ms = jnp.mean(x * x, axis=-1, keepdims=True)
    y = x * lax.rsqrt(ms + EPS) * gain
    return y * (1.0 + scale) + shift


def _mod_kernel(c_ref, w_ref, b_ref, side_ref, o_ref, sideb_ref):
    sideb_ref[...] = side_ref[...].astype(BF16)
    c = c_ref[...]
    cond = _silu(c)
    acc = jnp.sum(cond * w_ref[0], axis=0, keepdims=True)
    o_ref[0] = acc + b_ref[0]


def _mod_call(c, mod_w, mod_b, side_w, side_layer, tn=768):
    depth, d, n = mod_w.shape
    nj = n // tn
    s_in, s_out, s_shape = _cast_job(side_w, side_layer, depth * nj, lambda l, j: l * nj + j)
    out, side_b = pl.pallas_call(
        _mod_kernel,
        grid=(depth, nj),
        in_specs=[
            pl.BlockSpec((d, 1), lambda l, j: (0, 0)),
            pl.BlockSpec((1, d, tn), lambda l, j: (l, 0, j)),
            pl.BlockSpec((1, 1, tn), lambda l, j: (l, 0, j)),
            s_in,
        ],
        out_specs=[pl.BlockSpec((1, 1, tn), lambda l, j: (l, 0, j)), s_out],
        out_shape=[jax.ShapeDtypeStruct((depth, 1, n), F32), s_shape],
        compiler_params=_cparams("arbitrary", "arbitrary"),
        name="mod_proj",
    )(c.reshape(d, 1), mod_w, mod_b.reshape(depth, 1, n), side_w)
    return out.reshape(depth, N_MOD, d), side_b


PROJ_COLS = 256


def _qkv_finish(proj, qkg_ref, p_ref, pt_ref, q_dim, kv_dim, scale):
    qk = proj[:, :q_dim + kv_dim]
    ss = jnp.dot((qk * qk).astype(BF16), p_ref[...], preferred_element_type=F32)
    inv = lax.rsqrt(ss * (1.0 / HEAD_DIM) + EPS)
    inv_hi = inv.astype(BF16)
    inv_lo = (inv - inv_hi.astype(F32)).astype(BF16)
    inv_b = jnp.dot(jnp.concatenate([inv_hi, inv_lo], axis=1), pt_ref[...],
                    preferred_element_type=F32)
    qkn = qk * inv_b * qkg_ref[...]
    return ((qkn[:, :q_dim] * scale).astype(BF16), qkn[:, q_dim:].astype(BF16),
            proj[:, q_dim + kv_dim:].astype(BF16))


def _attn_layer_kernel(sink_ref, x_ref, mod_ref, gain_ref, wb_ref, qkg_ref, p_ref, pt_ref,
                       wo_ref, w1_ref, w2_ref,
                       o_ref, wob_ref, w1b_ref, w2b_ref,
                       q_s, k_s, v_s, kt_s, vt_s, bias_ref, *, n_kv, n_blk, scale):
    s = pl.program_id(0)
    wob_ref[...] = wo_ref[...].astype(BF16)
    w1b_ref[...] = w1_ref[...].astype(BF16)
    w2b_ref[...] = w2_ref[...].astype(BF16)
    n_heads = n_kv * GROUP
    q_dim = n_heads * HEAD_DIM
    kv_dim = n_kv * HEAD_DIM
    row = lax.broadcasted_iota(jnp.int32, (WINDOW, WINDOW), 0)
    col = lax.broadcasted_iota(jnp.int32, (WINDOW, WINDOW), 1)
    in_cur = col <= row

    @pl.when(s == 0)
    def _():
        dist = jnp.where(in_cur, row - col, row - col + WINDOW).astype(F32)
        for hq in range(n_heads):
            slope = LOG2E * 2.0 ** (-8.0 * (hq + 1) / n_heads)
            bias_ref[hq] = -slope * dist
            bias_ref[n_heads + hq] = jnp.where(in_cur, -slope * dist, -jnp.inf)
        q_s[...] = jnp.zeros_like(q_s)
        k_s[...] = jnp.zeros_like(k_s)
        v_s[...] = jnp.zeros_like(v_s)
        kt_s[...] = jnp.zeros_like(kt_s)
        vt_s[...] = jnp.zeros_like(vt_s)

    wslot = s % 2
    rslot = 1 - wslot

    lane = lax.broadcasted_iota(jnp.int32, (WINDOW, LANES), 1)
    low = lane < HEAD_DIM
    one_lo = jnp.where(low, 1.0, 0.0).astype(BF16)
    one_hi = jnp.where(low, 0.0, 1.0).astype(BF16)
    den_rhs = jnp.concatenate([one_lo, one_lo, one_hi, one_hi], axis=0)
    zero = jnp.zeros((WINDOW, LANES), BF16)

    def halves(t, g):
        r = pltpu.roll(t, HEAD_DIM, axis=1)
        if g % 2 == 0:
            return jnp.where(low, t, zero), jnp.where(low, zero, r)
        return jnp.where(low, r, zero), jnp.where(low, zero, t)

    units = [(b_, g_) for b_ in range(n_blk) for g_ in range(n_kv)]
    n_pairs = GROUP // 2

    h = _norm_mod(x_ref[...], gain_ref[...], mod_ref[0:1, :], mod_ref[1:2, :]).astype(BF16)
    n_cols = (q_dim + 2 * kv_dim) // PROJ_COLS
    proj_chunks = []

    def project_next():
        if len(proj_chunks) < n_cols:
            cs = slice(len(proj_chunks) * PROJ_COLS, (len(proj_chunks) + 1) * PROJ_COLS)
            proj_chunks.append(jnp.dot(h, wb_ref[:, cs], preferred_element_type=F32))

    def logits_stage(blk, g):
        qrows = slice(blk * WINDOW, (blk + 1) * WINDOW)
        sl = slice((g // 2) * LANES, (g // 2 + 1) * LANES)
        if blk == 0:
            k_prev, v_prev = kt_s[:, sl], vt_s[:, sl]
            bias_off = jnp.where(s == 1, n_heads, 0)
        else:
            prows = slice((blk - 1) * WINDOW, blk * WINDOW)
            k_prev, v_prev = k_s[rslot, prows, sl], v_s[rslot, prows, sl]
            bias_off = 0
        kc_lo, kc_hi = halves(k_s[rslot, qrows, sl], g)
        kp_lo, kp_hi = halves(k_prev, g)
        vc_lo, vc_hi = halves(v_s[rslot, qrows, sl], g)
        vp_lo, vp_hi = halves(v_prev, g)
        k_rhs = jnp.concatenate([kc_lo, kc_hi, kp_lo, kp_hi], axis=0)
        v_rhs = jnp.concatenate([vc_lo, vp_lo, vc_hi, vp_hi], axis=0)
        q_base = g * GROUP * HEAD_DIM
        qs = jnp.concatenate(
            [q_s[rslot, qrows, q_base + t * LANES: q_base + (t + 1) * LANES] for t in range(n_pairs)], axis=0)
        logits = lax.dot_general(qs, k_rhs, (((1,), (1,)), ((), ())), preferred_element_type=F32)
        return logits, v_rhs, bias_off

    def softmax_pv_stage(blk, g, logits, v_rhs, bias_off):
        qrows = slice(blk * WINDOW, (blk + 1) * WINDOW)
        q_base = g * GROUP * HEAD_DIM
        pms, sink_terms = [], []
        for t in range(n_pairs):
            rows = slice(t * WINDOW, (t + 1) * WINDOW)
            probs, mx_e = [], []
            for e in range(2):
                hq = g * GROUP + 2 * t + e
                l_cur = logits[rows, e * WINDOW:(e + 1) * WINDOW]
                l_prev = logits[rows, (2 + e) * WINDOW:(3 + e) * WINDOW]
                sc = jnp.where(in_cur, l_cur, l_prev) + bias_ref[hq + bias_off]
                mx = jnp.max(sc, axis=1, keepdims=True)
                pe = jnp.exp2(sc - mx).astype(BF16)
                probs.append(jnp.where(in_cur, pe, zero))
                probs.append(jnp.where(in_cur, zero, pe))
                mx_e.append(mx)
            pms.append(jnp.concatenate(probs, axis=1))
            hq0 = g * GROUP + 2 * t
            sink_l = jnp.where(low[0:1], sink_ref[0, hq0] * LOG2E, sink_ref[0, hq0 + 1] * LOG2E)
            sink_terms.append(jnp.exp2(sink_l - jnp.where(low, mx_e[0], mx_e[1])))
        res = jnp.dot(jnp.concatenate(pms, axis=0), jnp.concatenate([v_rhs, den_rhs], axis=1),
                      preferred_element_type=F32)
        for t in range(n_pairs):
            rows = slice(t * WINDOW, (t + 1) * WINDOW)
            den = res[rows, LANES:] + sink_terms[t]
            o_ref[qrows, q_base + t * LANES: q_base + (t + 1) * LANES] = (res[rows, :LANES] / den).astype(BF16)

    staged = logits_stage(*units[0])
    project_next()
    project_next()
    for u, (blk, g) in enumerate(units):
        cur = staged
        project_next()
        if u + 1 < len(units):
            staged = logits_stage(*units[u + 1])
        softmax_pv_stage(blk, g, *cur)
    while len(proj_chunks) < n_cols:
        project_next()

    last = slice((n_blk - 1) * WINDOW, n_blk * WINDOW)
    kt_s[...] = k_s[rslot, last, :]
    vt_s[...] = v_s[rslot, last, :]

    qn, kn, vn = _qkv_finish(jnp.concatenate(proj_chunks, axis=1), qkg_ref, p_ref, pt_ref, q_dim, kv_dim, scale)
    q_s[wslot] = qn
    k_s[wslot] = kn
    v_s[wslot] = vn


def _attn_layer_call(x, mod_l, gain, w_in_b, qk_gain, sinks, w_out_all, mlp_w1, mlp_w2, layer, n_blk=2):
    t, d = x.shape
    n = w_in_b.shape[1]
    kv_dim = (n - d) // 2
    q_dim = d
    n_kv = kv_dim // HEAD_DIM
    n_heads = n_kv * GROUP
    assert n_kv % 2 == 0 and q_dim == n_heads * HEAD_DIM and (q_dim + kv_dim) // HEAD_DIM <= LANES
    rows = n_blk * WINDOW
    nt = t // rows
    ind = (np.arange(q_dim + kv_dim)[:, None] // HEAD_DIM == np.arange(LANES)[None, :])
    p = jnp.asarray(ind, dtype=BF16)
    pt = jnp.asarray(np.concatenate([ind.T, ind.T], axis=0), dtype=BF16)
    proj_tile = lambda s: (jnp.minimum(s, nt - 1), 0)
    attn_tile = lambda s: (jnp.maximum(s - 1, 0), 0)
    cast_step = lambda s: jnp.minimum(s, nt - 1)
    wo_in, wo_out, wo_shape = _cast_job(w_out_all, layer, nt, cast_step)
    w1_in, w1_out, w1_shape = _cast_job(mlp_w1, layer, nt, cast_step)
    w2_in, w2_out, w2_shape = _cast_job(mlp_w2, layer, nt, cast_step)
    return pl.pallas_call(
        functools.partial(_attn_layer_kernel, n_kv=n_kv, n_blk=n_blk, scale=LOG2E / math.sqrt(HEAD_DIM)),
        grid=(nt + 1,),
        in_specs=[
            pl.BlockSpec(memory_space=pltpu.SMEM),
            pl.BlockSpec((rows, d), proj_tile),
            _resident((N_MOD, d)),
            _resident((1, d)),
            _resident((d, n)),
            _resident((1, q_dim + kv_dim)),
            _resident((q_dim + kv_dim, LANES)),
            _resident((2 * LANES, q_dim + kv_dim)),
            wo_in, w1_in, w2_in,
        ],
        out_specs=[pl.BlockSpec((rows, q_dim), attn_tile), wo_out, w1_out, w2_out],
        out_shape=[jax.ShapeDtypeStruct((t, q_dim), BF16), wo_shape, w1_shape, w2_shape],
        scratch_shapes=[
            pltpu.VMEM((2, rows, q_dim), BF16),
            pltpu.VMEM((2, rows, kv_dim), BF16),
            pltpu.VMEM((2, rows, kv_dim), BF16),
            pltpu.VMEM((WINDOW, kv_dim), BF16),
            pltpu.VMEM((WINDOW, kv_dim), BF16),
            pltpu.VMEM((2 * n_heads, WINDOW, WINDOW), F32),
        ],
        compiler_params=_cparams("arbitrary"),
        name="swa_attention_layer",
    )(sinks.reshape(1, n_heads), x, mod_l, gain, w_in_b, qk_gain, p, pt, w_out_all, mlp_w1, mlp_w2)


def _mlp_kernel(*refs, has_side, n_sub):
    if has_side:
        x_ref, a_ref, mod_ref, gain_ref, wo_ref, w1_ref, w2_ref, side_ref, o_ref, sideb_ref, h_ref = refs
        sideb_ref[...] = side_ref[...].astype(BF16)
    else:
        x_ref, a_ref, mod_ref, gain_ref, wo_ref, w1_ref, w2_ref, o_ref, h_ref = refs
    f = pl.program_id(1)
    gate = mod_ref[5:6, :]

    def ffn(h):
        a = jnp.maximum(jnp.dot(h, w1_ref[...], preferred_element_type=F32), 0.0)
        return gate * jnp.dot((a * a).astype(BF16), w2_ref[...], preferred_element_type=F32)

    @pl.when(f == 0)
    def _():
        sub = x_ref.shape[0] // n_sub
        for s in range(n_sub):
            rs = slice(s * sub, (s + 1) * sub)
            x1 = x_ref[rs, :] + mod_ref[2:3, :] * jnp.dot(a_ref[rs, :], wo_ref[...], preferred_element_type=F32)
            h = _norm_mod(x1, gain_ref[...], mod_ref[3:4, :], mod_ref[4:5, :]).astype(BF16)
            h_ref[rs, :] = h
            o_ref[rs, :] = x1 + ffn(h)

    @pl.when(f > 0)
    def _():
        o_ref[...] += ffn(h_ref[...])


def _mlp_call(x, a, mod_l, gain, w_out, w1, w2, side=None, tm=512, tf=1024, n_sub=2):
    t, d = x.shape
    kdim = a.shape[1]
    dff = w1.shape[1]
    nf = dff // tf
    in_specs = [
        pl.BlockSpec((tm, d), lambda i, f: (i, 0)),
        pl.BlockSpec((tm, kdim), lambda i, f: (i, 0)),
        _resident((N_MOD, d)),
        _resident((1, d)),
        _resident((kdim, d)),
        pl.BlockSpec((d, tf), lambda i, f: (0, f)),
        pl.BlockSpec((tf, d), lambda i, f: (f, 0)),
    ]
    out_specs = [pl.BlockSpec((tm, d), lambda i, f: (i, 0))]
    out_shape = [jax.ShapeDtypeStruct((t, d), F32)]
    args = [x, a, mod_l, gain, w_out, w1, w2]
    if side is not None:
        arr, in_fn, out_fn, shape = side
        in_specs.append(in_fn(t // tm, nf))
        out_specs.append(out_fn(t // tm, nf))
        out_shape.append(shape)
        args.append(arr)
    outs = pl.pallas_call(
        functools.partial(_mlp_kernel, has_side=side is not None, n_sub=n_sub),
        grid=(t // tm, nf),
        in_specs=in_specs,
        out_specs=out_specs,
        out_shape=out_shape,
        scratch_shapes=[pltpu.VMEM((tm, d), BF16)],
        compiler_params=_cparams("arbitrary", "arbitrary"),
        name="outproj_relu2_mlp",
    )(*args)
    return outs if side is not None else outs[0]


def _hgrn_w_in_side_job(hgrn_w_in, layer, n_heads, hpt):
    _, d, n = hgrn_w_in.shape
    cw = hpt * HG_DK
    n_tiles = n_heads // hpt
    n_cb = 4 * n_tiles

    def blocks(n_outer, n_inner):
        n_rb = (n_outer * n_inner) // n_cb
        assert n_rb * n_cb == n_outer * n_inner and d % n_rb == 0 and (d // n_rb) % 16 == 0
        return d // n_rb

    def in_fn(n_outer, n_inner):
        rb = blocks(n_outer, n_inner)

        def imap(i, f):
            step = i * n_inner + f
            out_cb = step % n_cb
            return (layer, step // n_cb, (out_cb % 4) * n_tiles + out_cb // 4)
        return pl.BlockSpec((None, rb, cw), imap)

    def out_fn(n_outer, n_inner):
        rb = blocks(n_outer, n_inner)

        def omap(i, f):
            step = i * n_inner + f
            return (step // n_cb, step % n_cb)
        return pl.BlockSpec((rb, cw), omap)

    return hgrn_w_in, in_fn, out_fn, jax.ShapeDtypeStruct((d, n), BF16)


def _layer_lower_bound(lbl_ref):
    l0 = lbl_ref[0]
    l1 = lbl_ref[1]
    lm = jnp.maximum(l0, l1)
    e0 = jnp.exp(l0 - lm)
    e1 = jnp.exp(l1 - lm)
    p0 = e0 / (e0 + e1)
    p1 = e1 / (e0 + e1)
    return (p0 + p1) - p0


def _hgrn_inproj_kernel(x_ref, mod_ref, gain_ref, lbl_ref, w_ref, side_ref, side2_ref,
                        q_ref, lf_ref, kk_ref, v_ref, g_ref, sideb_ref, side2b_ref, h_ref, *, hpt, scale, n_sub):
    j = pl.program_id(1)
    sideb_ref[...] = side_ref[...].astype(BF16)
    side2b_ref[...] = side2_ref[...].astype(BF16)
    lb = _layer_lower_bound(lbl_ref)
    width = hpt * HG_DK

    def project(h, rs):
        res = jnp.dot(h, w_ref[...], preferred_element_type=F32)
        for hh in range(hpt):
            seg = lambda s: res[:, s * width + hh * HG_DK: s * width + (hh + 1) * HG_DK]
            q_ref[hh, rs, :] = (_silu(seg(0)) * scale).astype(BF16)
            forget = lb[hh] + (1.0 - lb[hh]) * _sigmoid(seg(1))
            kk_ref[hh, rs, :] = 1.0 - forget
            lf_ref[hh, rs, :] = jnp.log(forget)
            v_ref[hh, rs, :] = seg(2).astype(BF16)
            g_ref[hh, rs, :] = _silu(seg(3)).astype(BF16)

    @pl.when(j == 0)
    def _():
        sub = x_ref.shape[0] // n_sub
        for s in range(n_sub):
            rs = slice(s * sub, (s + 1) * sub)
            h = _norm_mod(x_ref[rs, :], gain_ref[...], mod_ref[0:1, :], mod_ref[1:2, :]).astype(BF16)
            h_ref[rs, :] = h
            project(h, rs)

    @pl.when(j > 0)
    def _():
        project(h_ref[...], slice(None))


def _hgrn_inproj_call(x, mod_l, gain, lb_logits, w_perm, n_heads, side_w, side_layer, side2_w, side2_layer,
                      tm=1024, hpt=2):
    t, d = x.shape
    depth = lb_logits.shape[0]
    assert depth == 2
    tn = 4 * hpt * HG_DK
    n_inner = n_heads // hpt
    head_major = lambda dt: jax.ShapeDtypeStruct((n_heads, t, HG_DK), dt)
    ospec = pl.BlockSpec((hpt, tm, HG_DK), lambda i, j: (j, i, 0))
    s_in, s_out, s_shape = _cast_job(side_w, side_layer, (t // tm) * n_inner, lambda i, j: i * n_inner + j)
    s2_in, s2_out, s2_shape = _cast_job(side2_w, side2_layer, (t // tm) * n_inner,
                                        lambda i, j: i * n_inner + j)
    return pl.pallas_call(
        functools.partial(_hgrn_inproj_kernel, hpt=hpt, scale=1.0 / math.sqrt(HG_DK), n_sub=4),
        grid=(t // tm, n_inner),
        in_specs=[
            pl.BlockSpec((tm, d), lambda i, j: (i, 0)),
            _resident((N_MOD, d)),
            _resident((1, d)),
            pl.BlockSpec((depth, hpt, 1, HG_DK), lambda i, j: (0, j, 0, 0)),
            pl.BlockSpec((d, tn), lambda i, j: (0, j)),
            s_in, s2_in,
        ],
        out_specs=[ospec] * 5 + [s_out, s2_out],
        out_shape=[head_major(BF16), head_major(F32), head_major(F32), head_major(BF16), head_major(BF16),
                   s_shape, s2_shape],
        scratch_shapes=[pltpu.VMEM((tm, d), BF16)],
        compiler_params=_cparams("arbitrary", "arbitrary"),
        name="hgrn_inproj",
    )(x, mod_l, gain, lb_logits.reshape(depth, n_heads, 1, HG_DK), w_perm, side_w, side2_w)


def _cum_matrix(tr):
    n_chunks = tr // CHUNK
    r = np.arange(tr)[:, None]
    c = np.arange(tr)[None, :]
    same = (r // CHUNK) == (c // CHUNK)
    incl = same & (c <= r)
    pivot = same & (c % CHUNK <= CHUNK // 2 - 1)
    rows = np.zeros((2 * n_chunks, tr), np.float32)
    for ch in range(n_chunks):
        rows[2 * ch, ch * CHUNK: ch * CHUNK + CHUNK // 2] = 1.0
        rows[2 * ch + 1, ch * CHUNK: (ch + 1) * CHUNK] = 1.0
    mats = np.concatenate([incl.astype(np.float32) - pivot.astype(np.float32), rows], axis=0)
    return jnp.asarray(mats, dtype=BF16)


def _hgrn_kernel(gain_ref, cm_ref, q_ref, lf_ref, kk_ref, v_ref, g_ref, side_ref,
                 o_ref, sideb_ref, st_ref, *, hb, tr):
    tstep = pl.program_id(1)
    sideb_ref[...] = side_ref[...].astype(BF16)

    @pl.when(tstep == 0)
    def _():
        st_ref[...] = jnp.zeros_like(st_ref)

    rowc = lax.broadcasted_iota(jnp.int32, (CHUNK, 2 * CHUNK), 0)
    colc = lax.broadcasted_iota(jnp.int32, (CHUNK, 2 * CHUNK), 1)
    causal2 = (colc % CHUNK) <= rowc
    n_chunks = tr // CHUNK
    zc = jnp.zeros((CHUNK, HG_DK), BF16)
    zs = jnp.zeros((HG_DK, HG_DK), BF16)

    def block_diag(a, b, z):
        return jnp.concatenate([jnp.concatenate([a, z], axis=1), jnp.concatenate([z, b], axis=1)], axis=0)

    for pp in range(hb // 2):
        heads = (2 * pp, 2 * pp + 1)
        qe, ke, qb, ku, dec = [], [], [], [], []
        for hh in heads:
            logf = lf_ref[hh]
            p_hi = logf.astype(BF16)
            p_lo = (logf - p_hi.astype(F32)).astype(BF16)
            cums = jnp.dot(cm_ref[...], jnp.concatenate([p_hi, p_lo], axis=1), preferred_element_type=F32)
            cums = cums[:, :HG_DK] + cums[:, HG_DK:]
            bmp = cums[:tr]
            qe_f = q_ref[hh].astype(F32) * jnp.exp(bmp)
            ke_f = kk_ref[hh] * jnp.exp(-bmp)
            qb_h, ku_h, dec_h = [], [], []
            for c in range(n_chunks):
                rs = slice(c * CHUNK, (c + 1) * CHUNK)
                piv = cums[tr + 2 * c: tr + 2 * c + 1]
                blast = cums[tr + 2 * c + 1: tr + 2 * c + 2]
                qb_h.append((qe_f[rs] * jnp.exp(piv)).astype(BF16))
                ku_h.append((ke_f[rs] * jnp.exp(blast - piv)).astype(BF16))
                dec_h.append(jnp.exp(blast))
            qe.append(qe_f.astype(BF16))
            ke.append(ke_f.astype(BF16))
            qb.append(qb_h)
            ku.append(ku_h)
            dec.append(dec_h)

        for c in range(n_chunks):
            rs = slice(c * CHUNK, (c + 1) * CHUNK)
            v0 = v_ref[heads[0], rs, :]
            v1 = v_ref[heads[1], rs, :]
            a = lax.dot_general(jnp.concatenate([qe[0][rs], qe[1][rs]], axis=1),
                                block_diag(ke[0][rs], ke[1][rs], zc),
                                (((1,), (1,)), ((), ())), preferred_element_type=F32)
            a = jnp.where(causal2, a, 0.0).astype(BF16)
            st = st_ref[pp]
            st_b = st.astype(BF16)
            o = (jnp.dot(a, block_diag(v0, v1, zc), preferred_element_type=F32)
                 + lax.dot_general(jnp.concatenate([qb[0][c], qb[1][c]], axis=1),
                                   block_diag(st_b[:, :HG_DK], st_b[:, HG_DK:], zs),
                                   (((1,), (1,)), ((), ())), preferred_element_type=F32))
            upd = lax.dot_general(jnp.concatenate([v0, v1], axis=0), block_diag(ku[0][c], ku[1][c], zc),
                                  (((0,), (0,)), ((), ())), preferred_element_type=F32)
            st_ref[pp] = st * jnp.concatenate([dec[0][c], dec[1][c]], axis=1) + upd
            for e, hh in enumerate(heads):
                oh = o[:, e * HG_DK:(e + 1) * HG_DK]
                on = oh * lax.rsqrt(jnp.mean(oh * oh, axis=-1, keepdims=True) + EPS) * gain_ref[hh]
                o_ref[rs, hh * HG_DK:(hh + 1) * HG_DK] = (on * g_ref[hh, rs, :].astype(F32)).astype(BF16)


def _hgrn_call(q, lf, kk, v, g, o_gain, side_w, side_layer, hb=8, tr=256):
    n_heads, t, dk = q.shape
    assert dk == HG_DK and hb % 2 == 0
    cm = _cum_matrix(tr)
    n_inner = t // tr
    blk = pl.BlockSpec((hb, tr, dk), lambda h, s: (h, s, 0))
    s_in, s_out, s_shape = _cast_job(side_w, side_layer, (n_heads // hb) * n_inner,
                                     lambda h, s: h * n_inner + s)
    return pl.pallas_call(
        functools.partial(_hgrn_kernel, hb=hb, tr=tr),
        grid=(n_heads // hb, n_inner),
        in_specs=[
            pl.BlockSpec((hb, 1, dk), lambda h, s: (h, 0, 0)),
            _resident(cm.shape),
            blk, blk, blk, blk, blk,
            s_in,
        ],
        out_specs=[pl.BlockSpec((tr, hb * dk), lambda h, s: (s, h)), s_out],
        out_shape=[jax.ShapeDtypeStruct((t, n_heads * dk), BF16), s_shape],
        scratch_shapes=[pltpu.VMEM((hb // 2, dk, 2 * dk), F32)],
        compiler_params=_cparams("arbitrary", "arbitrary"),
        name="hgrn_recurrence",
    )(o_gain.reshape(n_heads, 1, dk), cm, q, lf, kk, v, g, side_w)


def kernel(x, c, mod_w, mod_b, norm_mix, norm_mlp, attn_w_in, attn_w_out, attn_q_gain, attn_k_gain,
           attn_sinks, hgrn_w_in, hgrn_w_out, hgrn_o_gain, hgrn_lb_logits, mlp_w1, mlp_w2):
    b, t, d = x.shape
    assert b == 1 and mod_w.shape[0] == 2
    xs = x.reshape(t, d)

    mod, w_in_b = _mod_call(c, mod_w, mod_b, attn_w_in, 0)

    q_dim = attn_w_out.shape[1]
    kv_dim = (attn_w_in.shape[2] - q_dim) // 2
    qk_gain = jnp.concatenate([jnp.tile(attn_q_gain[0], q_dim // HEAD_DIM),
                               jnp.tile(attn_k_gain[0], kv_dim // HEAD_DIM)]).reshape(1, q_dim + kv_dim)
    att, wob, w1b, w2b = _attn_layer_call(xs, mod[0], norm_mix[0].reshape(1, d), w_in_b, qk_gain,
                                          attn_sinks[0], attn_w_out, mlp_w1, mlp_w2, layer=0)
    n_heads = hgrn_o_gain.shape[1]
    hpt = 2
    xs, hw_perm = _mlp_call(xs, att, mod[0], norm_mlp[0].reshape(1, d), wob, w1b, w2b,
                            side=_hgrn_w_in_side_job(hgrn_w_in, 0, n_heads, hpt))

    hq, hlf, hkk, hv, hg, w1b, wob = _hgrn_inproj_call(xs, mod[1], norm_mix[1].reshape(1, d), hgrn_lb_logits,
                                                       hw_perm, n_heads, mlp_w1, 1, hgrn_w_out, 0, hpt=hpt)
    ho, w2b = _hgrn_call(hq, hlf, hkk, hv, hg, hgrn_o_gain[0], mlp_w2, 1)
    xs = _mlp_call(xs, ho, mod[1], norm_mlp[1].reshape(1, d), wob, w1b, w2b)
    return xs.reshape(b, t, d)
```

```python
import functools
import math

import numpy as np
import jax
import jax.numpy as jnp
from jax import lax
from jax.experimental import pallas as pl
from jax.experimental.pallas import tpu as pltpu

F32 = jnp.float32
BF16 = jnp.bfloat16

EPS = 1e-6
N_MOD = 6

HEAD_DIM = 64
GROUP = 8
WINDOW = 128
LANES = 128
LOG2E = math.log2(math.e)

HG_DK = 128
CHUNK = 64

VMEM_LIMIT = 56 * 1024 * 1024


def _cparams(*sem):
    return pltpu.CompilerParams(dimension_semantics=sem, vmem_limit_bytes=VMEM_LIMIT)


def _resident(shape):
    nd = len(shape)
    return pl.BlockSpec(shape, lambda *_: (0,) * nd, pipeline_mode=pl.Buffered(1))


def _cast_job(w, layer, n_steps, step_of):
    _, r, c = w.shape
    rb = r // n_steps
    assert rb * n_steps == r and rb % 16 == 0
    in_spec = pl.BlockSpec((None, rb, c), lambda *g: (layer, step_of(*g), 0))
    out_spec = pl.BlockSpec((rb, c), lambda *g: (step_of(*g), 0))
    return in_spec, out_spec, jax.ShapeDtypeStruct((r, c), BF16)


def _sigmoid(v):
    return 0.5 * jnp.tanh(0.5 * v) + 0.5


def _silu(v):
    return v * _sigmoid(v)


def _norm_mod(x, gain, shift, scale):
    ms = jnp.mean(x * x, axis=-1, keepdims=True)
    y = x * lax.rsqrt(ms + EPS) * gain
    return y * (1.0 + scale) + shift


def _mod_kernel(c_ref, w_ref, b_ref, side_ref, o_ref, sideb_ref):
    sideb_ref[...] = side_ref[...].astype(BF16)
    c = c_ref[...]
    cond = _silu(c)
    acc = jnp.sum(cond * w_ref[0], axis=0, keepdims=True)
    o_ref[0] = acc + b_ref[0]


def _mod_call(c, mod_w, mod_b, side_w, side_layer, tn=768):
    depth, d, n = mod_w.shape
    nj = n // tn
    s_in, s_out, s_shape = _cast_job(side_w, side_layer, depth * nj, lambda l, j: l * nj + j)
    out, side_b = pl.pallas_call(
        _mod_kernel,
        grid=(depth, nj),
        in_specs=[
            pl.BlockSpec((d, 1), lambda l, j: (0, 0)),
            pl.BlockSpec((1, d, tn), lambda l, j: (l, 0, j)),
            pl.BlockSpec((1, 1, tn), lambda l, j: (l, 0, j)),
            s_in,
        ],
        out_specs=[pl.BlockSpec((1, 1, tn), lambda l, j: (l, 0, j)), s_out],
        out_shape=[jax.ShapeDtypeStruct((depth, 1, n), F32), s_shape],
        compiler_params=_cparams("arbitrary", "arbitrary"),
        name="mod_proj",
    )(c.reshape(d, 1), mod_w, mod_b.reshape(depth, 1, n), side_w)
    return out.reshape(depth, N_MOD, d), side_b


PROJ_COLS = 256


def _qkv_finish(proj, qkg_ref, p_ref, pt_ref, q_dim, kv_dim, scale):
    qk = proj[:, :q_dim + kv_dim]
    ss = jnp.dot((qk * qk).astype(BF16), p_ref[...], preferred_element_type=F32)
    inv = lax.rsqrt(ss * (1.0 / HEAD_DIM) + EPS)
    inv_hi = inv.astype(BF16)
    inv_lo = (inv - inv_hi.astype(F32)).astype(BF16)
    inv_b = jnp.dot(jnp.concatenate([inv_hi, inv_lo], axis=1), pt_ref[...],
                    preferred_element_type=F32)
    qkn = qk * inv_b * qkg_ref[...]
    return ((qkn[:, :q_dim] * scale).astype(BF16), qkn[:, q_dim:].astype(BF16),
            proj[:, q_dim + kv_dim:].astype(BF16))


def _attn_layer_kernel(sink_ref, x_ref, mod_ref, gain_ref, wb_ref, qkg_ref, p_ref, pt_ref,
                       wo_ref, w1_ref, w2_ref,
                       o_ref, wob_ref, w1b_ref, w2b_ref,
                       q_s, k_s, v_s, kt_s, vt_s, bias_ref, *, n_kv, n_blk, scale):
    s = pl.program_id(0)
    wob_ref[...] = wo_ref[...].astype(BF16)
    w1b_ref[...] = w1_ref[...].astype(BF16)
    w2b_ref[...] = w2_ref[...].astype(BF16)
    n_heads = n_kv * GROUP
    q_dim = n_heads * HEAD_DIM
    kv_dim = n_kv * HEAD_DIM
    row = lax.broadcasted_iota(jnp.int32, (WINDOW, WINDOW), 0)
    col = lax.broadcasted_iota(jnp.int32, (WINDOW, WINDOW), 1)
    in_cur = col <= row

    @pl.when(s == 0)
    def _():
        dist = jnp.where(in_cur, row - col, row - col + WINDOW).astype(F32)
        for hq in range(n_heads):
            slope = LOG2E * 2.0 ** (-8.0 * (hq + 1) / n_heads)
            bias_ref[hq] = -slope * dist
            bias_ref[n_heads + hq] = jnp.where(in_cur, -slope * dist, -jnp.inf)
        q_s[...] = jnp.zeros_like(q_s)
        k_s[...] = jnp.zeros_like(k_s)
        v_s[...] = jnp.zeros_like(v_s)
        kt_s[...] = jnp.zeros_like(kt_s)
        vt_s[...] = jnp.zeros_like(vt_s)

    wslot = s % 2
    rslot = 1 - wslot

    lane = lax.broadcasted_iota(jnp.int32, (WINDOW, LANES), 1)
    low = lane < HEAD_DIM
    one_lo = jnp.where(low, 1.0, 0.0).astype(BF16)
    one_hi = jnp.where(low, 0.0, 1.0).astype(BF16)
    den_rhs = jnp.concatenate([one_lo, one_lo, one_hi, one_hi], axis=0)
    zero = jnp.zeros((WINDOW, LANES), BF16)

    def halves(t, g):
        r = pltpu.roll(t, HEAD_DIM, axis=1)
        if g % 2 == 0:
            return jnp.where(low, t, zero), jnp.where(low, zero, r)
        return jnp.where(low, r, zero), jnp.where(low, zero, t)

    units = [(b_, g_) for b_ in range(n_blk) for g_ in range(n_kv)]
    n_pairs = GROUP // 2

    h = _norm_mod(x_ref[...], gain_ref[...], mod_ref[0:1, :], mod_ref[1:2, :]).astype(BF16)
    n_cols = (q_dim + 2 * kv_dim) // PROJ_COLS
    proj_chunks = []

    def project_next():
        if len(proj_chunks) < n_cols:
            cs = slice(len(proj_chunks) * PROJ_COLS, (len(proj_chunks) + 1) * PROJ_COLS)
            proj_chunks.append(jnp.dot(h, wb_ref[:, cs], preferred_element_type=F32))

    def logits_stage(blk, g):
        qrows = slice(blk * WINDOW, (blk + 1) * WINDOW)
        sl = slice((g // 2) * LANES, (g // 2 + 1) * LANES)
        if blk == 0:
            k_prev, v_prev = kt_s[:, sl], vt_s[:, sl]
            bias_off = jnp.where(s == 1, n_heads, 0)
        else:
            prows = slice((blk - 1) * WINDOW, blk * WINDOW)
            k_prev, v_prev = k_s[rslot, prows, sl], v_s[rslot, prows, sl]
            bias_off = 0
        kc_lo, kc_hi = halves(k_s[rslot, qrows, sl], g)
        kp_lo, kp_hi = halves(k_prev, g)
        vc_lo, vc_hi = halves(v_s[rslot, qrows, sl], g)
        vp_lo, vp_hi = halves(v_prev, g)
        k_rhs = jnp.concatenate([kc_lo, kc_hi, kp_lo, kp_hi], axis=0)
        v_rhs = jnp.concatenate([vc_lo, vp_lo, vc_hi, vp_hi], axis=0)
        q_base = g * GROUP * HEAD_DIM
        qs = jnp.concatenate(
            [q_s[rslot, qrows, q_base + t * LANES: q_base + (t + 1) * LANES] for t in range(n_pairs)], axis=0)
        logits = lax.dot_general(qs, k_rhs, (((1,), (1,)), ((), ())), preferred_element_type=F32)
        return logits, v_rhs, bias_off

    def softmax_pv_stage(blk, g, logits, v_rhs, bias_off):
        qrows = slice(blk * WINDOW, (blk + 1) * WINDOW)
        q_base = g * GROUP * HEAD_DIM
        pms, sink_terms = [], []
        for t in range(n_pairs):
            rows = slice(t * WINDOW, (t + 1) * WINDOW)
            probs, mx_e = [], []
            for e in range(2):
                hq = g * GROUP + 2 * t + e
                l_cur = logits[rows, e * WINDOW:(e + 1) * WINDOW]
                l_prev = logits[rows, (2 + e) * WINDOW:(3 + e) * WINDOW]
                sc = jnp.where(in_cur, l_cur, l_prev) + bias_ref[hq + bias_off]
                mx = jnp.max(sc, axis=1, keepdims=True)
                pe = jnp.exp2(sc - mx).astype(BF16)
                probs.append(jnp.where(in_cur, pe, zero))
                probs.append(jnp.where(in_cur, zero, pe))
                mx_e.append(mx)
            pms.append(jnp.concatenate(probs, axis=1))
            hq0 = g * GROUP + 2 * t
            sink_l = jnp.where(low[0:1], sink_ref[0, hq0] * LOG2E, sink_ref[0, hq0 + 1] * LOG2E)
            sink_terms.append(jnp.exp2(sink_l - jnp.where(low, mx_e[0], mx_e[1])))
        res = jnp.dot(jnp.concatenate(pms, axis=0), jnp.concatenate([v_rhs, den_rhs], axis=1),
                      preferred_element_type=F32)
        for t in range(n_pairs):
            rows = slice(t * WINDOW, (t + 1) * WINDOW)
            den = res[rows, LANES:] + sink_terms[t]
            o_ref[qrows, q_base + t * LANES: q_base + (t + 1) * LANES] = (res[rows, :LANES] / den).astype(BF16)

    staged = logits_stage(*units[0])
    project_next()
    project_next()
    for u, (blk, g) in enumerate(units):
        cur = staged
        project_next()
        if u + 1 < len(units):
            staged = logits_stage(*units[u + 1])
        softmax_pv_stage(blk, g, *cur)
    while len(proj_chunks) < n_cols:
        project_next()

    last = slice((n_blk - 1) * WINDOW, n_blk * WINDOW)
    kt_s[...] = k_s[rslot, last, :]
    vt_s[...] = v_s[rslot, last, :]

    qn, kn, vn = _qkv_finish(jnp.concatenate(proj_chunks, axis=1), qkg_ref, p_ref, pt_ref, q_dim, kv_dim, scale)
    q_s[wslot] = qn
    k_s[wslot] = kn
    v_s[wslot] = vn


def _attn_layer_call(x, mod_l, gain, w_in_b, qk_gain, sinks, w_out_all, mlp_w1, mlp_w2, layer, n_blk=2):
    t, d = x.shape
    n = w_in_b.shape[1]
    kv_dim = (n - d) // 2
    q_dim = d
    n_kv = kv_dim // HEAD_DIM
    n_heads = n_kv * GROUP
    assert n_kv % 2 == 0 and q_dim == n_heads * HEAD_DIM and (q_dim + kv_dim) // HEAD_DIM <= LANES
    rows = n_blk * WINDOW
    nt = t // rows
    ind = (np.arange(q_dim + kv_dim)[:, None] // HEAD_DIM == np.arange(LANES)[None, :])
    p = jnp.asarray(ind, dtype=BF16)
    pt = jnp.asarray(np.concatenate([ind.T, ind.T], axis=0), dtype=BF16)
    proj_tile = lambda s: (jnp.minimum(s, nt - 1), 0)
    attn_tile = lambda s: (jnp.maximum(s - 1, 0), 0)
    cast_step = lambda s: jnp.minimum(s, nt - 1)
    wo_in, wo_out, wo_shape = _cast_job(w_out_all, layer, nt, cast_step)
    w1_in, w1_out, w1_shape = _cast_job(mlp_w1, layer, nt, cast_step)
    w2_in, w2_out, w2_shape = _cast_job(mlp_w2, layer, nt, cast_step)
    return pl.pallas_call(
        functools.partial(_attn_layer_kernel, n_kv=n_kv, n_blk=n_blk, scale=LOG2E / math.sqrt(HEAD_DIM)),
        grid=(nt + 1,),
        in_specs=[
            pl.BlockSpec(memory_space=pltpu.SMEM),
            pl.BlockSpec((rows, d), proj_tile),
            _resident((N_MOD, d)),
            _resident((1, d)),
            _resident((d, n)),
            _resident((1, q_dim + kv_dim)),
            _resident((q_dim + kv_dim, LANES)),
            _resident((2 * LANES, q_dim + kv_dim)),
            wo_in, w1_in, w2_in,
        ],
        out_specs=[pl.BlockSpec((rows, q_dim), attn_tile), wo_out, w1_out, w2_out],
        out_shape=[jax.ShapeDtypeStruct((t, q_dim), BF16), wo_shape, w1_shape, w2_shape],
        scratch_shapes=[
            pltpu.VMEM((2, rows, q_dim), BF16),
            pltpu.VMEM((2, rows, kv_dim), BF16),
            pltpu.VMEM((2, rows, kv_dim), BF16),
            pltpu.VMEM((WINDOW, kv_dim), BF16),
            pltpu.VMEM((WINDOW, kv_dim), BF16),
            pltpu.VMEM((2 * n_heads, WINDOW, WINDOW), F32),
        ],
        compiler_params=_cparams("arbitrary"),
        name="swa_attention_layer",
    )(sinks.reshape(1, n_heads), x, mod_l, gain, w_in_b, qk_gain, p, pt, w_out_all, mlp_w1, mlp_w2)


def _mlp_kernel(*refs, has_side, n_sub):
    if has_side:
        x_ref, a_ref, mod_ref, gain_ref, wo_ref, w1_ref, w2_ref, side_ref, o_ref, sideb_ref, h_ref = refs
        sideb_ref[...] = side_ref[...].astype(BF16)
    else:
        x_ref, a_ref, mod_ref, gain_ref, wo_ref, w1_ref, w2_ref, o_ref, h_ref = refs
    f = pl.program_id(1)
    gate = mod_ref[5:6, :]

    def ffn(h):
        a = jnp.maximum(jnp.dot(h, w1_ref[...], preferred_element_type=F32), 0.0)
        return gate * jnp.dot((a * a).astype(BF16), w2_ref[...], preferred_element_type=F32)

    @pl.when(f == 0)
    def _():
        sub = x_ref.shape[0] // n_sub
        for s in range(n_sub):
            rs = slice(s * sub, (s + 1) * sub)
            x1 = x_ref[rs, :] + mod_ref[2:3, :] * jnp.dot(a_ref[rs, :], wo_ref[...], preferred_element_type=F32)
            h = _norm_mod(x1, gain_ref[...], mod_ref[3:4, :], mod_ref[4:5, :]).astype(BF16)
            h_ref[rs, :] = h
            o_ref[rs, :] = x1 + ffn(h)

    @pl.when(f > 0)
    def _():
        o_ref[...] += ffn(h_ref[...])


def _mlp_call(x, a, mod_l, gain, w_out, w1, w2, side=None, tm=512, tf=1024, n_sub=2):
    t, d = x.shape
    kdim = a.shape[1]
    dff = w1.shape[1]
    nf = dff // tf
    in_specs = [
        pl.BlockSpec((tm, d), lambda i, f: (i, 0)),
        pl.BlockSpec((tm, kdim), lambda i, f: (i, 0)),
        _resident((N_MOD, d)),
        _resident((1, d)),
        _resident((kdim, d)),
        pl.BlockSpec((d, tf), lambda i, f: (0, f)),
        pl.BlockSpec((tf, d), lambda i, f: (f, 0)),
    ]
    out_specs = [pl.BlockSpec((tm, d), lambda i, f: (i, 0))]
    out_shape = [jax.ShapeDtypeStruct((t, d), F32)]
    args = [x, a, mod_l, gain, w_out, w1, w2]
    if side is not None:
        arr, in_fn, out_fn, shape = side
        in_specs.append(in_fn(t // tm, nf))
        out_specs.append(out_fn(t // tm, nf))
        out_shape.append(shape)
        args.append(arr)
    outs = pl.pallas_call(
        functools.partial(_mlp_kernel, has_side=side is not None, n_sub=n_sub),
        grid=(t // tm, nf),
        in_specs=in_specs,
        out_specs=out_specs,
        out_shape=out_shape,
        scratch_shapes=[pltpu.VMEM((tm, d), BF16)],
        compiler_params=_cparams("arbitrary", "arbitrary"),
        name="outproj_relu2_mlp",
    )(*args)
    return outs if side is not None else outs[0]


def _hgrn_w_in_side_job(hgrn_w_in, layer, n_heads, hpt):
    _, d, n = hgrn_w_in.shape
    cw = hpt * HG_DK
    n_tiles = n_heads // hpt
    n_cb = 4 * n_tiles

    def blocks(n_outer, n_inner):
        n_rb = (n_outer * n_inner) // n_cb
        assert n_rb * n_cb == n_outer * n_inner and d % n_rb == 0 and (d // n_rb) % 16 == 0
        return d // n_rb

    def in_fn(n_outer, n_inner):
        rb = blocks(n_outer, n_inner)

        def imap(i, f):
            step = i * n_inner + f
            out_cb = step % n_cb
            return (layer, step // n_cb, (out_cb % 4) * n_tiles + out_cb // 4)
        return pl.BlockSpec((None, rb, cw), imap)

    def out_fn(n_outer, n_inner):
        rb = blocks(n_outer, n_inner)

        def omap(i, f):
            step = i * n_inner + f
            return (step // n_cb, step % n_cb)
        return pl.BlockSpec((rb, cw), omap)

    return hgrn_w_in, in_fn, out_fn, jax.ShapeDtypeStruct((d, n), BF16)


def _layer_lower_bound(lbl_ref):
    l0 = lbl_ref[0]
    l1 = lbl_ref[1]
    lm = jnp.maximum(l0, l1)
    e0 = jnp.exp(l0 - lm)
    e1 = jnp.exp(l1 - lm)
    p0 = e0 / (e0 + e1)
    p1 = e1 / (e0 + e1)
    return (p0 + p1) - p0


def _hgrn_inproj_kernel(x_ref, mod_ref, gain_ref, lbl_ref, w_ref, side_ref, side2_ref,
                        q_ref, lf_ref, kk_ref, v_ref, g_ref, sideb_ref, side2b_ref, h_ref, *, hpt, scale, n_sub):
    j = pl.program_id(1)
    sideb_ref[...] = side_ref[...].astype(BF16)
    side2b_ref[...] = side2_ref[...].astype(BF16)
    lb = _layer_lower_bound(lbl_ref)
    width = hpt * HG_DK

    def project(h, rs):
        res = jnp.dot(h, w_ref[...], preferred_element_type=F32)
        for hh in range(hpt):
            seg = lambda s: res[:, s * width + hh * HG_DK: s * width + (hh + 1) * HG_DK]
            q_ref[hh, rs, :] = (_silu(seg(0)) * scale).astype(BF16)
            forget = lb[hh] + (1.0 - lb[hh]) * _sigmoid(seg(1))
            kk_ref[hh, rs, :] = 1.0 - forget
            lf_ref[hh, rs, :] = jnp.log(forget)
            v_ref[hh, rs, :] = seg(2).astype(BF16)
            g_ref[hh, rs, :] = _silu(seg(3)).astype(BF16)

    @pl.when(j == 0)
    def _():
        sub = x_ref.shape[0] // n_sub
        for s in range(n_sub):
            rs = slice(s * sub, (s + 1) * sub)
            h = _norm_mod(x_ref[rs, :], gain_ref[...], mod_ref[0:1, :], mod_ref[1:2, :]).astype(BF16)
            h_ref[rs, :] = h
            project(h, rs)

    @pl.when(j > 0)
    def _():
        project(h_ref[...], slice(None))


def _hgrn_inproj_call(x, mod_l, gain, lb_logits, w_perm, n_heads, side_w, side_layer, side2_w, side2_layer,
                      tm=1024, hpt=2):
    t, d = x.shape
    depth = lb_logits.shape[0]
    assert depth == 2
    tn = 4 * hpt * HG_DK
    n_inner = n_heads // hpt
    head_major = lambda dt: jax.ShapeDtypeStruct((n_heads, t, HG_DK), dt)
    ospec = pl.BlockSpec((hpt, tm, HG_DK), lambda i, j: (j, i, 0))
    s_in, s_out, s_shape = _cast_job(side_w, side_layer, (t // tm) * n_inner, lambda i, j: i * n_inner + j)
    s2_in, s2_out, s2_shape = _cast_job(side2_w, side2_layer, (t // tm) * n_inner,
                                        lambda i, j: i * n_inner + j)
    return pl.pallas_call(
        functools.partial(_hgrn_inproj_kernel, hpt=hpt, scale=1.0 / math.sqrt(HG_DK), n_sub=4),
        grid=(t // tm, n_inner),
        in_specs=[
            pl.BlockSpec((tm, d), lambda i, j: (i, 0)),
            _resident((N_MOD, d)),
            _resident((1, d)),
            pl.BlockSpec((depth, hpt, 1, HG_DK), lambda i, j: (0, j, 0, 0)),
            pl.BlockSpec((d, tn), lambda i, j: (0, j)),
            s_in, s2_in,
        ],
        out_specs=[ospec] * 5 + [s_out, s2_out],
        out_shape=[head_major(BF16), head_major(F32), head_major(F32), head_major(BF16), head_major(BF16),
                   s_shape, s2_shape],
        scratch_shapes=[pltpu.VMEM((tm, d), BF16)],
        compiler_params=_cparams("arbitrary", "arbitrary"),
        name="hgrn_inproj",
    )(x, mod_l, gain, lb_logits.reshape(depth, n_heads, 1, HG_DK), w_perm, side_w, side2_w)


def _cum_matrix(tr):
    n_chunks = tr // CHUNK
    r = np.arange(tr)[:, None]
    c = np.arange(tr)[None, :]
    same = (r // CHUNK) == (c // CHUNK)
    incl = same & (c <= r)
    pivot = same & (c % CHUNK <= CHUNK // 2 - 1)
    rows = np.zeros((2 * n_chunks, tr), np.float32)
    for ch in range(n_chunks):
        rows[2 * ch, ch * CHUNK: ch * CHUNK + CHUNK // 2] = 1.0
        rows[2 * ch + 1, ch * CHUNK: (ch + 1) * CHUNK] = 1.0
    mats = np.concatenate([incl.astype(np.float32) - pivot.astype(np.float32), rows], axis=0)
    return jnp.asarray(mats, dtype=BF16)


def _hgrn_kernel(gain_ref, cm_ref, q_ref, lf_ref, kk_ref, v_ref, g_ref, side_ref,
                 o_ref, sideb_ref, st_ref, *, hb, tr):
    tstep = pl.program_id(1)
    sideb_ref[...] = side_ref[...].astype(BF16)

    @pl.when(tstep == 0)
    def _():
        st_ref[...] = jnp.zeros_like(st_ref)

    rowc = lax.broadcasted_iota(jnp.int32, (CHUNK, 2 * CHUNK), 0)
    colc = lax.broadcasted_iota(jnp.int32, (CHUNK, 2 * CHUNK), 1)
    causal2 = (colc % CHUNK) <= rowc
    n_chunks = tr // CHUNK
    zc = jnp.zeros((CHUNK, HG_DK), BF16)
    zs = jnp.zeros((HG_DK, HG_DK), BF16)

    def block_diag(a, b, z):
        return jnp.concatenate([jnp.concatenate([a, z], axis=1), jnp.concatenate([z, b], axis=1)], axis=0)

    n_pairs = hb // 2
    pairs = [(2 * pp, 2 * pp + 1) for pp in range(n_pairs)]
    chunk_rows = [slice(c * CHUNK, (c + 1) * CHUNK) for c in range(n_chunks)]

    cums = []
    for hh in range(hb):
        logf = lf_ref[hh]
        p_hi = logf.astype(BF16)
        p_lo = (logf - p_hi.astype(F32)).astype(BF16)
        cm = jnp.dot(cm_ref[...], jnp.concatenate([p_hi, p_lo], axis=1), preferred_element_type=F32)
        cums.append(cm[:, :HG_DK] + cm[:, HG_DK:])

    qe, ke, qb, ku, dec = [], [], [], [], []
    for hh in range(hb):
        bmp = cums[hh][:tr]
        qe_f = q_ref[hh].astype(F32) * jnp.exp(bmp)
        ke_f = kk_ref[hh] * jnp.exp(-bmp)
        qb_h, ku_h, dec_h = [], [], []
        for c, rs in enumerate(chunk_rows):
            piv = cums[hh][tr + 2 * c: tr + 2 * c + 1]
            blast = cums[hh][tr + 2 * c + 1: tr + 2 * c + 2]
            qb_h.append((qe_f[rs] * jnp.exp(piv)).astype(BF16))
            ku_h.append((ke_f[rs] * jnp.exp(blast - piv)).astype(BF16))
            dec_h.append(jnp.exp(blast))
        qe.append(qe_f.astype(BF16))
        ke.append(ke_f.astype(BF16))
        qb.append(qb_h)
        ku.append(ku_h)
        dec.append(dec_h)

    amat = [[None] * n_chunks for _ in pairs]
    upd = [[None] * n_chunks for _ in pairs]
    for c, rs in enumerate(chunk_rows):
        for pp, (h0, h1) in enumerate(pairs):
            v0 = v_ref[h0, rs, :]
            v1 = v_ref[h1, rs, :]
            a = lax.dot_general(jnp.concatenate([qe[h0][rs], qe[h1][rs]], axis=1),
                                block_diag(ke[h0][rs], ke[h1][rs], zc),
                                (((1,), (1,)), ((), ())), preferred_element_type=F32)
            amat[pp][c] = jnp.where(causal2, a, 0.0).astype(BF16)
            upd[pp][c] = lax.dot_general(jnp.concatenate([v0, v1], axis=0),
                                         block_diag(ku[h0][c], ku[h1][c], zc),
                                         (((0,), (0,)), ((), ())), preferred_element_type=F32)

    for c, rs in enumerate(chunk_rows):
        for pp, (h0, h1) in enumerate(pairs):
            st = st_ref[pp]
            st_b = st.astype(BF16)
            o = (jnp.dot(amat[pp][c], block_diag(v_ref[h0, rs, :], v_ref[h1, rs, :], zc),
                         preferred_element_type=F32)
                 + lax.dot_general(jnp.concatenate([qb[h0][c], qb[h1][c]], axis=1),
                                   block_diag(st_b[:, :HG_DK], st_b[:, HG_DK:], zs),
                                   (((1,), (1,)), ((), ())), preferred_element_type=F32))
            st_ref[pp] = st * jnp.concatenate([dec[h0][c], dec[h1][c]], axis=1) + upd[pp][c]
            for e, hh in enumerate((h0, h1)):
                oh = o[:, e * HG_DK:(e + 1) * HG_DK]
                on = oh * lax.rsqrt(jnp.mean(oh * oh, axis=-1, keepdims=True) + EPS) * gain_ref[hh]
                o_ref[rs, hh * HG_DK:(hh + 1) * HG_DK] = (on * g_ref[hh, rs, :].astype(F32)).astype(BF16)


def _hgrn_call(q, lf, kk, v, g, o_gain, side_w, side_layer, hb=8, tr=256):
    n_heads, t, dk = q.shape
    assert dk == HG_DK and hb % 2 == 0
    cm = _cum_matrix(tr)
    n_inner = t // tr
    blk = pl.BlockSpec((hb, tr, dk), lambda h, s: (h, s, 0))
    s_in, s_out, s_shape = _cast_job(side_w, side_layer, (n_heads // hb) * n_inner,
                                     lambda h, s: h * n_inner + s)
    return pl.pallas_call(
        functools.partial(_hgrn_kernel, hb=hb, tr=tr),
        grid=(n_heads // hb, n_inner),
        in_specs=[
            pl.BlockSpec((hb, 1, dk), lambda h, s: (h, 0, 0)),
            _resident(cm.shape),
            blk, blk, blk, blk, blk,
            s_in,
        ],
        out_specs=[pl.BlockSpec((tr, hb * dk), lambda h, s: (s, h)), s_out],
        out_shape=[jax.ShapeDtypeStruct((t, n_heads * dk), BF16), s_shape],
        scratch_shapes=[pltpu.VMEM((hb // 2, dk, 2 * dk), F32)],
        compiler_params=_cparams("arbitrary", "arbitrary"),
        name="hgrn_recurrence",
    )(o_gain.reshape(n_heads, 1, dk), cm, q, lf, kk, v, g, side_w)


def kernel(x, c, mod_w, mod_b, norm_mix, norm_mlp, attn_w_in, attn_w_out, attn_q_gain, attn_k_gain,
           attn_sinks, hgrn_w_in, hgrn_w_out, hgrn_o_gain, hgrn_lb_logits, mlp_w1, mlp_w2):
    b, t, d = x.shape
    assert b == 1 and mod_w.shape[0] == 2
    xs = x.reshape(t, d)

    mod, w_in_b = _mod_call(c, mod_w, mod_b, attn_w_in, 0)

    q_dim = attn_w_out.shape[1]
    kv_dim = (attn_w_in.shape[2] - q_dim) // 2
    qk_gain = jnp.concatenate([jnp.tile(attn_q_gain[0], q_dim // HEAD_DIM),
                               jnp.tile(attn_k_gain[0], kv_dim // HEAD_DIM)]).reshape(1, q_dim + kv_dim)
    att, wob, w1b, w2b = _attn_layer_call(xs, mod[0], norm_mix[0].reshape(1, d), w_in_b, qk_gain,
                                          attn_sinks[0], attn_w_out, mlp_w1, mlp_w2, layer=0)
    n_heads = hgrn_o_gain.shape[1]
    hpt = 2
    xs, hw_perm = _mlp_call(xs, att, mod[0], norm_mlp[0].reshape(1, d), wob, w1b, w2b,
                            side=_hgrn_w_in_side_job(hgrn_w_in, 0, n_heads, hpt))

    hq, hlf, hkk, hv, hg, w1b, wob = _hgrn_inproj_call(xs, mod[1], norm_mix[1].reshape(1, d), hgrn_lb_logits,
                                                       hw_perm, n_heads, mlp_w1, 1, hgrn_w_out, 0, hpt=hpt)
    ho, w2b = _hgrn_call(hq, hlf, hkk, hv, hg, hgrn_o_gain[0], mlp_w2, 1)
    xs = _mlp_call(xs, ho, mod[1], norm_mlp[1].reshape(1, d), wob, w1b, w2b)
    return xs.reshape(b, t, d)
```

```python
import functools
import math

import numpy as np
import jax
import jax.numpy as jnp
from jax import lax
from jax.experimental import pallas as pl
from jax.experimental.pallas import tpu as pltpu

F32 = jnp.float32
BF16 = jnp.bfloat16

EPS = 1e-6
N_MOD = 6

HEAD_DIM = 64
GROUP = 8
WINDOW = 128
LANES = 128
LOG2E = math.log2(math.e)

HG_DK = 128
CHUNK = 64

VMEM_LIMIT = 56 * 1024 * 1024


def _cparams(*sem):
    return pltpu.CompilerParams(dimension_semantics=sem, vmem_limit_bytes=VMEM_LIMIT)


def _resident(shape):
    nd = len(shape)
    return pl.BlockSpec(shape, lambda *_: (0,) * nd, pipeline_mode=pl.Buffered(1))


def _cast_job(w, layer, n_steps, step_of):
    _, r, c = w.shape
    rb = r // n_steps
    assert rb * n_steps == r and rb % 16 == 0
    in_spec = pl.BlockSpec((None, rb, c), lambda *g: (layer, step_of(*g), 0))
    out_spec = pl.BlockSpec((rb, c), lambda *g: (step_of(*g), 0))
    return in_spec, out_spec, jax.ShapeDtypeStruct((r, c), BF16)


def _sigmoid(v):
    return 0.5 * jnp.tanh(0.5 * v) + 0.5


def _silu(v):
    return v * _sigmoid(v)


def _norm_mod(x, gain, shift, scale):
    ms = jnp.mean(x * x, axis=-1, keepdims=True)
    y = x * lax.rsqrt(ms + EPS) * gain
    return y * (1.0 + scale) + shift


def _mod_kernel(c_ref, w_ref, b_ref, side_ref, o_ref, sideb_ref):
    sideb_ref[...] = side_ref[...].astype(BF16)
    c = c_ref[...]
    cond = _silu(c)
    acc = jnp.sum(cond * w_ref[0], axis=0, keepdims=True)
    o_ref[0] = acc + b_ref[0]


def _mod_call(c, mod_w, mod_b, side_w, side_layer, tn=768):
    depth, d, n = mod_w.shape
    nj = n // tn
    s_in, s_out, s_shape = _cast_job(side_w, side_layer, depth * nj, lambda l, j: l * nj + j)
    out, side_b = pl.pallas_call(
        _mod_kernel,
        grid=(depth, nj),
        in_specs=[
            pl.BlockSpec((d, 1), lambda l, j: (0, 0)),
            pl.BlockSpec((1, d, tn), lambda l, j: (l, 0, j)),
            pl.BlockSpec((1, 1, tn), lambda l, j: (l, 0, j)),
            s_in,
        ],
        out_specs=[pl.BlockSpec((1, 1, tn), lambda l, j: (l, 0, j)), s_out],
        out_shape=[jax.ShapeDtypeStruct((depth, 1, n), F32), s_shape],
        compiler_params=_cparams("arbitrary", "arbitrary"),
        name="mod_proj",
    )(c.reshape(d, 1), mod_w, mod_b.reshape(depth, 1, n), side_w)
    return out.reshape(depth, N_MOD, d), side_b


PROJ_COLS = 256


def _qkv_finish(proj, qkg_ref, p_ref, pt_ref, q_dim, kv_dim, scale):
    qk = proj[:, :q_dim + kv_dim]
    ss = jnp.dot((qk * qk).astype(BF16), p_ref[...], preferred_element_type=F32)
    inv = lax.rsqrt(ss * (1.0 / HEAD_DIM) + EPS)
    inv_hi = inv.astype(BF16)
    inv_lo = (inv - inv_hi.astype(F32)).astype(BF16)
    inv_b = jnp.dot(jnp.concatenate([inv_hi, inv_lo], axis=1), pt_ref[...],
                    preferred_element_type=F32)
    qkn = qk * inv_b * qkg_ref[...]
    return ((qkn[:, :q_dim] * scale).astype(BF16), qkn[:, q_dim:].astype(BF16),
            proj[:, q_dim + kv_dim:].astype(BF16))


def _attn_layer_kernel(sink_ref, x_ref, mod_ref, gain_ref, wb_ref, qkg_ref, p_ref, pt_ref,
                       wo_ref, w1_ref, w2_ref,
                       o_ref, wob_ref, w1b_ref, w2b_ref,
                       q_s, k_s, v_s, kt_s, vt_s, bias_ref, *, n_kv, n_blk, scale):
    s = pl.program_id(0)
    wob_ref[...] = wo_ref[...].astype(BF16)
    w1b_ref[...] = w1_ref[...].astype(BF16)
    w2b_ref[...] = w2_ref[...].astype(BF16)
    n_heads = n_kv * GROUP
    q_dim = n_heads * HEAD_DIM
    kv_dim = n_kv * HEAD_DIM
    row = lax.broadcasted_iota(jnp.int32, (WINDOW, WINDOW), 0)
    col = lax.broadcasted_iota(jnp.int32, (WINDOW, WINDOW), 1)
    in_cur = col <= row

    @pl.when(s == 0)
    def _():
        dist = jnp.where(in_cur, row - col, row - col + WINDOW).astype(F32)
        for hq in range(n_heads):
            slope = LOG2E * 2.0 ** (-8.0 * (hq + 1) / n_heads)
            bias_ref[hq] = -slope * dist
            bias_ref[n_heads + hq] = jnp.where(in_cur, -slope * dist, -jnp.inf)
        q_s[...] = jnp.zeros_like(q_s)
        k_s[...] = jnp.zeros_like(k_s)
        v_s[...] = jnp.zeros_like(v_s)
        kt_s[...] = jnp.zeros_like(kt_s)
        vt_s[...] = jnp.zeros_like(vt_s)

    wslot = s % 2
    rslot = 1 - wslot

    lane = lax.broadcasted_iota(jnp.int32, (WINDOW, LANES), 1)
    low = lane < HEAD_DIM
    one_lo = jnp.where(low, 1.0, 0.0).astype(BF16)
    one_hi = jnp.where(low, 0.0, 1.0).astype(BF16)
    den_rhs = jnp.concatenate([one_lo, one_lo, one_hi, one_hi], axis=0)
    zero = jnp.zeros((WINDOW, LANES), BF16)

    def halves(t, g):
        r = pltpu.roll(t, HEAD_DIM, axis=1)
        if g % 2 == 0:
            return jnp.where(low, t, zero), jnp.where(low, zero, r)
        return jnp.where(low, r, zero), jnp.where(low, zero, t)

    units = [(b_, g_) for b_ in range(n_blk) for g_ in range(n_kv)]
    n_pairs = GROUP // 2

    h = _norm_mod(x_ref[...], gain_ref[...], mod_ref[0:1, :], mod_ref[1:2, :]).astype(BF16)
    n_cols = (q_dim + 2 * kv_dim) // PROJ_COLS
    proj_chunks = []

    def project_next():
        if len(proj_chunks) < n_cols:
            cs = slice(len(proj_chunks) * PROJ_COLS, (len(proj_chunks) + 1) * PROJ_COLS)
            proj_chunks.append(jnp.dot(h, wb_ref[:, cs], preferred_element_type=F32))

    def logits_stage(blk, g):
        qrows = slice(blk * WINDOW, (blk + 1) * WINDOW)
        sl = slice((g // 2) * LANES, (g // 2 + 1) * LANES)
        if blk == 0:
            k_prev, v_prev = kt_s[:, sl], vt_s[:, sl]
            bias_off = jnp.where(s == 1, n_heads, 0)
        else:
            prows = slice((blk - 1) * WINDOW, blk * WINDOW)
            k_prev, v_prev = k_s[rslot, prows, sl], v_s[rslot, prows, sl]
            bias_off = 0
        kc_lo, kc_hi = halves(k_s[rslot, qrows, sl], g)
        kp_lo, kp_hi = halves(k_prev, g)
        vc_lo, vc_hi = halves(v_s[rslot, qrows, sl], g)
        vp_lo, vp_hi = halves(v_prev, g)
        k_rhs = jnp.concatenate([kc_lo, kc_hi, kp_lo, kp_hi], axis=0)
        v_rhs = jnp.concatenate([vc_lo, vp_lo, vc_hi, vp_hi], axis=0)
        q_base = g * GROUP * HEAD_DIM
        qs = jnp.concatenate(
            [q_s[rslot, qrows, q_base + t * LANES: q_base + (t + 1) * LANES] for t in range(n_pairs)], axis=0)
        logits = lax.dot_general(qs, k_rhs, (((1,), (1,)), ((), ())), preferred_element_type=F32)
        return logits, v_rhs, bias_off

    def softmax_pv_stage(blk, g, logits, v_rhs, bias_off):
        qrows = slice(blk * WINDOW, (blk + 1) * WINDOW)
        q_base = g * GROUP * HEAD_DIM
        pms, sink_terms = [], []
        for t in range(n_pairs):
            rows = slice(t * WINDOW, (t + 1) * WINDOW)
            probs, mx_e = [], []
            for e in range(2):
                hq = g * GROUP + 2 * t + e
                l_cur = logits[rows, e * WINDOW:(e + 1) * WINDOW]
                l_prev = logits[rows, (2 + e) * WINDOW:(3 + e) * WINDOW]
                sc = jnp.where(in_cur, l_cur, l_prev) + bias_ref[hq + bias_off]
                mx = jnp.max(sc, axis=1, keepdims=True)
                pe = jnp.exp2(sc - mx).astype(BF16)
                probs.append(jnp.where(in_cur, pe, zero))
                probs.append(jnp.where(in_cur, zero, pe))
                mx_e.append(mx)
            pms.append(jnp.concatenate(probs, axis=1))
            hq0 = g * GROUP + 2 * t
            sink_l = jnp.where(low[0:1], sink_ref[0, hq0] * LOG2E, sink_ref[0, hq0 + 1] * LOG2E)
            sink_terms.append(jnp.exp2(sink_l - jnp.where(low, mx_e[0], mx_e[1])))
        res = jnp.dot(jnp.concatenate(pms, axis=0), jnp.concatenate([v_rhs, den_rhs], axis=1),
                      preferred_element_type=F32)
        for t in range(n_pairs):
            rows = slice(t * WINDOW, (t + 1) * WINDOW)
            den = res[rows, LANES:] + sink_terms[t]
            o_ref[qrows, q_base + t * LANES: q_base + (t + 1) * LANES] = (res[rows, :LANES] / den).astype(BF16)

    staged = logits_stage(*units[0])
    project_next()
    project_next()
    for u, (blk, g) in enumerate(units):
        cur = staged
        project_next()
        if u + 1 < len(units):
            staged = logits_stage(*units[u + 1])
        softmax_pv_stage(blk, g, *cur)
    while len(proj_chunks) < n_cols:
        project_next()

    last = slice((n_blk - 1) * WINDOW, n_blk * WINDOW)
    kt_s[...] = k_s[rslot, last, :]
    vt_s[...] = v_s[rslot, last, :]

    qn, kn, vn = _qkv_finish(jnp.concatenate(proj_chunks, axis=1), qkg_ref, p_ref, pt_ref, q_dim, kv_dim, scale)
    q_s[wslot] = qn
    k_s[wslot] = kn
    v_s[wslot] = vn


def _attn_layer_call(x, mod_l, gain, w_in_b, qk_gain, sinks, w_out_all, mlp_w1, mlp_w2, layer, n_blk=2):
    t, d = x.shape
    n = w_in_b.shape[1]
    kv_dim = (n - d) // 2
    q_dim = d
    n_kv = kv_dim // HEAD_DIM
    n_heads = n_kv * GROUP
    assert n_kv % 2 == 0 and q_dim == n_heads * HEAD_DIM and (q_dim + kv_dim) // HEAD_DIM <= LANES
    rows = n_blk * WINDOW
    nt = t // rows
    ind = (np.arange(q_dim + kv_dim)[:, None] // HEAD_DIM == np.arange(LANES)[None, :])
    p = jnp.asarray(ind, dtype=BF16)
    pt = jnp.asarray(np.concatenate([ind.T, ind.T], axis=0), dtype=BF16)
    proj_tile = lambda s: (jnp.minimum(s, nt - 1), 0)
    attn_tile = lambda s: (jnp.maximum(s - 1, 0), 0)
    cast_step = lambda s: jnp.minimum(s, nt - 1)
    wo_in, wo_out, wo_shape = _cast_job(w_out_all, layer, nt, cast_step)
    w1_in, w1_out, w1_shape = _cast_job(mlp_w1, layer, nt, cast_step)
    w2_in, w2_out, w2_shape = _cast_job(mlp_w2, layer, nt, cast_step)
    return pl.pallas_call(
        functools.partial(_attn_layer_kernel, n_kv=n_kv, n_blk=n_blk, scale=LOG2E / math.sqrt(HEAD_DIM)),
        grid=(nt + 1,),
        in_specs=[
            pl.BlockSpec(memory_space=pltpu.SMEM),
            pl.BlockSpec((rows, d), proj_tile),
            _resident((N_MOD, d)),
            _resident((1, d)),
            _resident((d, n)),
            _resident((1, q_dim + kv_dim)),
            _resident((q_dim + kv_dim, LANES)),
            _resident((2 * LANES, q_dim + kv_dim)),
            wo_in, w1_in, w2_in,
        ],
        out_specs=[pl.BlockSpec((rows, q_dim), attn_tile), wo_out, w1_out, w2_out],
        out_shape=[jax.ShapeDtypeStruct((t, q_dim), BF16), wo_shape, w1_shape, w2_shape],
        scratch_shapes=[
            pltpu.VMEM((2, rows, q_dim), BF16),
            pltpu.VMEM((2, rows, kv_dim), BF16),
            pltpu.VMEM((2, rows, kv_dim), BF16),
            pltpu.VMEM((WINDOW, kv_dim), BF16),
            pltpu.VMEM((WINDOW, kv_dim), BF16),
            pltpu.VMEM((2 * n_heads, WINDOW, WINDOW), F32),
        ],
        compiler_params=_cparams("arbitrary"),
        name="swa_attention_layer",
    )(sinks.reshape(1, n_heads), x, mod_l, gain, w_in_b, qk_gain, p, pt, w_out_all, mlp_w1, mlp_w2)


def _mlp_kernel(*refs, has_side, n_sub):
    if has_side:
        x_ref, a_ref, mod_ref, gain_ref, wo_ref, w1_ref, w2_ref, side_ref, o_ref, sideb_ref, h_ref = refs
        sideb_ref[...] = side_ref[...].astype(BF16)
    else:
        x_ref, a_ref, mod_ref, gain_ref, wo_ref, w1_ref, w2_ref, o_ref, h_ref = refs
    f = pl.program_id(1)
    gate = mod_ref[5:6, :]

    def ffn(h):
        a = jnp.maximum(jnp.dot(h, w1_ref[...], preferred_element_type=F32), 0.0)
        return gate * jnp.dot((a * a).astype(BF16), w2_ref[...], preferred_element_type=F32)

    @pl.when(f == 0)
    def _():
        sub = x_ref.shape[0] // n_sub
        for s in range(n_sub):
            rs = slice(s * sub, (s + 1) * sub)
            x1 = x_ref[rs, :] + mod_ref[2:3, :] * jnp.dot(a_ref[rs, :], wo_ref[...], preferred_element_type=F32)
            h = _norm_mod(x1, gain_ref[...], mod_ref[3:4, :], mod_ref[4:5, :]).astype(BF16)
            h_ref[rs, :] = h
            o_ref[rs, :] = x1 + ffn(h)

    @pl.when(f > 0)
    def _():
        o_ref[...] += ffn(h_ref[...])


def _mlp_call(x, a, mod_l, gain, w_out, w1, w2, side=None, tm=512, tf=1024, n_sub=2):
    t, d = x.shape
    kdim = a.shape[1]
    dff = w1.shape[1]
    nf = dff // tf
    in_specs = [
        pl.BlockSpec((tm, d), lambda i, f: (i, 0)),
        pl.BlockSpec((tm, kdim), lambda i, f: (i, 0)),
        _resident((N_MOD, d)),
        _resident((1, d)),
        _resident((kdim, d)),
        pl.BlockSpec((d, tf), lambda i, f: (0, f)),
        pl.BlockSpec((tf, d), lambda i, f: (f, 0)),
    ]
    out_specs = [pl.BlockSpec((tm, d), lambda i, f: (i, 0))]
    out_shape = [jax.ShapeDtypeStruct((t, d), F32)]
    args = [x, a, mod_l, gain, w_out, w1, w2]
    if side is not None:
        arr, in_fn, out_fn, shape = side
        in_specs.append(in_fn(t // tm, nf))
        out_specs.append(out_fn(t // tm, nf))
        out_shape.append(shape)
        args.append(arr)
    outs = pl.pallas_call(
        functools.partial(_mlp_kernel, has_side=side is not None, n_sub=n_sub),
        grid=(t // tm, nf),
        in_specs=in_specs,
        out_specs=out_specs,
        out_shape=out_shape,
        scratch_shapes=[pltpu.VMEM((tm, d), BF16)],
        compiler_params=_cparams("arbitrary", "arbitrary"),
        name="outproj_relu2_mlp",
    )(*args)
    return outs if side is not None else outs[0]


def _hgrn_w_in_side_job(hgrn_w_in, layer, n_heads, hpt):
    _, d, n = hgrn_w_in.shape
    cw = hpt * HG_DK
    n_tiles = n_heads // hpt
    n_cb = 4 * n_tiles

    def blocks(n_outer, n_inner):
        n_rb = (n_outer * n_inner) // n_cb
        assert n_rb * n_cb == n_outer * n_inner and d % n_rb == 0 and (d // n_rb) % 16 == 0
        return d // n_rb

    def in_fn(n_outer, n_inner):
        rb = blocks(n_outer, n_inner)

        def imap(i, f):
            step = i * n_inner + f
            out_cb = step % n_cb
            return (layer, step // n_cb, (out_cb % 4) * n_tiles + out_cb // 4)
        return pl.BlockSpec((None, rb, cw), imap)

    def out_fn(n_outer, n_inner):
        rb = blocks(n_outer, n_inner)

        def omap(i, f):
            step = i * n_inner + f
            return (step // n_cb, step % n_cb)
        return pl.BlockSpec((rb, cw), omap)

    return hgrn_w_in, in_fn, out_fn, jax.ShapeDtypeStruct((d, n), BF16)


def _layer_lower_bound(lbl_ref):
    l0 = lbl_ref[0]
    l1 = lbl_ref[1]
    lm = jnp.maximum(l0, l1)
    e0 = jnp.exp(l0 - lm)
    e1 = jnp.exp(l1 - lm)
    p0 = e0 / (e0 + e1)
    p1 = e1 / (e0 + e1)
    return (p0 + p1) - p0


def _hgrn_inproj_kernel(x_ref, mod_ref, gain_ref, lbl_ref, w_ref, side_ref, side2_ref,
                        q_ref, lf_ref, kk_ref, v_ref, g_ref, sideb_ref, side2b_ref, h_ref, *, hpt, scale, n_sub):
    j = pl.program_id(1)
    sideb_ref[...] = side_ref[...].astype(BF16)
    side2b_ref[...] = side2_ref[...].astype(BF16)
    lb = _layer_lower_bound(lbl_ref)
    width = hpt * HG_DK

    def project(h, rs):
        res = jnp.dot(h, w_ref[...], preferred_element_type=F32)
        for hh in range(hpt):
            seg = lambda s: res[:, s * width + hh * HG_DK: s * width + (hh + 1) * HG_DK]
            q_ref[hh, rs, :] = (_silu(seg(0)) * scale).astype(BF16)
            forget = lb[hh] + (1.0 - lb[hh]) * _sigmoid(seg(1))
            kk_ref[hh, rs, :] = 1.0 - forget
            lf_ref[hh, rs, :] = jnp.log(forget)
            v_ref[hh, rs, :] = seg(2).astype(BF16)
            g_ref[hh, rs, :] = _silu(seg(3)).astype(BF16)

    @pl.when(j == 0)
    def _():
        sub = x_ref.shape[0] // n_sub
        for s in range(n_sub):
            rs = slice(s * sub, (s + 1) * sub)
            h = _norm_mod(x_ref[rs, :], gain_ref[...], mod_ref[0:1, :], mod_ref[1:2, :]).astype(BF16)
            h_ref[rs, :] = h
            project(h, rs)

    @pl.when(j > 0)
    def _():
        project(h_ref[...], slice(None))


def _hgrn_inproj_call(x, mod_l, gain, lb_logits, w_perm, n_heads, side_w, side_layer, side2_w, side2_layer,
                      tm=1024, hpt=2):
    t, d = x.shape
    depth = lb_logits.shape[0]
    assert depth == 2
    tn = 4 * hpt * HG_DK
    n_inner = n_heads // hpt
    head_major = lambda dt: jax.ShapeDtypeStruct((n_heads, t, HG_DK), dt)
    ospec = pl.BlockSpec((hpt, tm, HG_DK), lambda i, j: (j, i, 0))
    s_in, s_out, s_shape = _cast_job(side_w, side_layer, (t // tm) * n_inner, lambda i, j: i * n_inner + j)
    s2_in, s2_out, s2_shape = _cast_job(side2_w, side2_layer, (t // tm) * n_inner,
                                        lambda i, j: i * n_inner + j)
    return pl.pallas_call(
        functools.partial(_hgrn_inproj_kernel, hpt=hpt, scale=1.0 / math.sqrt(HG_DK), n_sub=4),
        grid=(t // tm, n_inner),
        in_specs=[
            pl.BlockSpec((tm, d), lambda i, j: (i, 0)),
            _resident((N_MOD, d)),
            _resident((1, d)),
            pl.BlockSpec((depth, hpt, 1, HG_DK), lambda i, j: (0, j, 0, 0)),
            pl.BlockSpec((d, tn), lambda i, j: (0, j)),
            s_in, s2_in,
        ],
        out_specs=[ospec] * 5 + [s_out, s2_out],
        out_shape=[head_major(BF16), head_major(F32), head_major(F32), head_major(BF16), head_major(BF16),
                   s_shape, s2_shape],
        scratch_shapes=[pltpu.VMEM((tm, d), BF16)],
        compiler_params=_cparams("arbitrary", "arbitrary"),
        name="hgrn_inproj",
    )(x, mod_l, gain, lb_logits.reshape(depth, n_heads, 1, HG_DK), w_perm, side_w, side2_w)


def _cum_matrix(tr):
    n_chunks = tr // CHUNK
    r = np.arange(tr)[:, None]
    c = np.arange(tr)[None, :]
    same = (r // CHUNK) == (c // CHUNK)
    incl = same & (c <= r)
    pivot = same & (c % CHUNK <= CHUNK // 2 - 1)
    rows = np.zeros((2 * n_chunks, tr), np.float32)
    for ch in range(n_chunks):
        rows[2 * ch, ch * CHUNK: ch * CHUNK + CHUNK // 2] = 1.0
        rows[2 * ch + 1, ch * CHUNK: (ch + 1) * CHUNK] = 1.0
    mats = np.concatenate([incl.astype(np.float32) - pivot.astype(np.float32), rows], axis=0)
    return jnp.asarray(mats, dtype=BF16)


def _hgrn_kernel(gain_ref, cm_ref, q_ref, lf_ref, kk_ref, v_ref, g_ref, side_ref,
                 o_ref, sideb_ref, st_ref, *, hb, tr):
    tstep = pl.program_id(1)
    sideb_ref[...] = side_ref[...].astype(BF16)

    @pl.when(tstep == 0)
    def _():
        st_ref[...] = jnp.zeros_like(st_ref)

    rowc = lax.broadcasted_iota(jnp.int32, (CHUNK, 2 * CHUNK), 0)
    colc = lax.broadcasted_iota(jnp.int32, (CHUNK, 2 * CHUNK), 1)
    causal2 = (colc % CHUNK) <= rowc
    n_chunks = tr // CHUNK
    zc = jnp.zeros((CHUNK, HG_DK), BF16)
    zs = jnp.zeros((HG_DK, HG_DK), BF16)

    def block_diag(a, b, z):
        return jnp.concatenate([jnp.concatenate([a, z], axis=1), jnp.concatenate([z, b], axis=1)], axis=0)

    n_pairs = hb // 2
    pairs = [(2 * pp, 2 * pp + 1) for pp in range(n_pairs)]
    chunk_rows = [slice(c * CHUNK, (c + 1) * CHUNK) for c in range(n_chunks)]

    cums = []
    for hh in range(hb):
        logf = lf_ref[hh]
        p_hi = logf.astype(BF16)
        p_lo = (logf - p_hi.astype(F32)).astype(BF16)
        cm = jnp.dot(cm_ref[...], jnp.concatenate([p_hi, p_lo], axis=1), preferred_element_type=F32)
        cums.append(cm[:, :HG_DK] + cm[:, HG_DK:])

    qe, ke, qb, ku, dec = [], [], [], [], []
    for hh in range(hb):
        bmp = cums[hh][:tr]
        qe_f = q_ref[hh].astype(F32) * jnp.exp(bmp)
        ke_f = kk_ref[hh] * jnp.exp(-bmp)
        qb_h, ku_h, dec_h = [], [], []
        for c, rs in enumerate(chunk_rows):
            piv = cums[hh][tr + 2 * c: tr + 2 * c + 1]
            blast = cums[hh][tr + 2 * c + 1: tr + 2 * c + 2]
            qb_h.append((qe_f[rs] * jnp.exp(piv)).astype(BF16))
            ku_h.append((ke_f[rs] * jnp.exp(blast - piv)).astype(BF16))
            dec_h.append(jnp.exp(blast))
        qe.append(qe_f.astype(BF16))
        ke.append(ke_f.astype(BF16))
        qb.append(qb_h)
        ku.append(ku_h)
        dec.append(dec_h)

    amat = [[None] * n_chunks for _ in pairs]
    upd = [[None] * n_chunks for _ in pairs]
    for c, rs in enumerate(chunk_rows):
        for pp, (h0, h1) in enumerate(pairs):
            v0 = v_ref[h0, rs, :]
            v1 = v_ref[h1, rs, :]
            a = lax.dot_general(jnp.concatenate([qe[h0][rs], qe[h1][rs]], axis=1),
                                block_diag(ke[h0][rs], ke[h1][rs], zc),
                                (((1,), (1,)), ((), ())), preferred_element_type=F32)
            amat[pp][c] = jnp.where(causal2, a, 0.0).astype(BF16)
            upd[pp][c] = lax.dot_general(jnp.concatenate([v0, v1], axis=0),
                                         block_diag(ku[h0][c], ku[h1][c], zc),
                                         (((0,), (0,)), ((), ())), preferred_element_type=F32)

    for c, rs in enumerate(chunk_rows):
        for pp, (h0, h1) in enumerate(pairs):
            st = st_ref[pp]
            st_b = st.astype(BF16)
            o = (jnp.dot(amat[pp][c], block_diag(v_ref[h0, rs, :], v_ref[h1, rs, :], zc),
                         preferred_element_type=F32)
                 + lax.dot_general(jnp.concatenate([qb[h0][c], qb[h1][c]], axis=1),
                                   block_diag(st_b[:, :HG_DK], st_b[:, HG_DK:], zs),
                                   (((1,), (1,)), ((), ())), preferred_element_type=F32))
            st_ref[pp] = st * jnp.concatenate([dec[h0][c], dec[h1][c]], axis=1) + upd[pp][c]
            for e, hh in enumerate((h0, h1)):
                oh = o[:, e * HG_DK:(e + 1) * HG_DK]
                on = oh * lax.rsqrt(jnp.mean(oh * oh, axis=-1, keepdims=True) + EPS) * gain_ref[hh]
                o_ref[rs, hh * HG_DK:(hh + 1) * HG_DK] = (on * g_ref[hh, rs, :].astype(F32)).astype(BF16)


def _hgrn_call(q, lf, kk, v, g, o_gain, side_w, side_layer, hb=16, tr=256):
    n_heads, t, dk = q.shape
    assert dk == HG_DK and hb % 2 == 0
    cm = _cum_matrix(tr)
    n_inner = t // tr
    blk = pl.BlockSpec((hb, tr, dk), lambda h, s: (h, s, 0))
    s_in, s_out, s_shape = _cast_job(side_w, side_layer, (n_heads // hb) * n_inner,
                                     lambda h, s: h * n_inner + s)
    return pl.pallas_call(
        functools.partial(_hgrn_kernel, hb=hb, tr=tr),
        grid=(n_heads // hb, n_inner),
        in_specs=[
            pl.BlockSpec((hb, 1, dk), lambda h, s: (h, 0, 0)),
            _resident(cm.shape),
            blk, blk, blk, blk, blk,
            s_in,
        ],
        out_specs=[pl.BlockSpec((tr, hb * dk), lambda h, s: (s, h)), s_out],
        out_shape=[jax.ShapeDtypeStruct((t, n_heads * dk), BF16), s_shape],
        scratch_shapes=[pltpu.VMEM((hb // 2, dk, 2 * dk), F32)],
        compiler_params=_cparams("arbitrary", "arbitrary"),
        name="hgrn_recurrence",
    )(o_gain.reshape(n_heads, 1, dk), cm, q, lf, kk, v, g, side_w)


def kernel(x, c, mod_w, mod_b, norm_mix, norm_mlp, attn_w_in, attn_w_out, attn_q_gain, attn_k_gain,
           attn_sinks, hgrn_w_in, hgrn_w_out, hgrn_o_gain, hgrn_lb_logits, mlp_w1, mlp_w2):
    b, t, d = x.shape
    assert b == 1 and mod_w.shape[0] == 2
    xs = x.reshape(t, d)

    mod, w_in_b = _mod_call(c, mod_w, mod_b, attn_w_in, 0)

    q_dim = attn_w_out.shape[1]
    kv_dim = (attn_w_in.shape[2] - q_dim) // 2
    qk_gain = jnp.concatenate([jnp.tile(attn_q_gain[0], q_dim // HEAD_DIM),
                               jnp.tile(attn_k_gain[0], kv_dim // HEAD_DIM)]).reshape(1, q_dim + kv_dim)
    att, wob, w1b, w2b = _attn_layer_call(xs, mod[0], norm_mix[0].reshape(1, d), w_in_b, qk_gain,
                                          attn_sinks[0], attn_w_out, mlp_w1, mlp_w2, layer=0)
    n_heads = hgrn_o_gain.shape[1]
    hpt = 2
    xs, hw_perm = _mlp_call(xs, att, mod[0], norm_mlp[0].reshape(1, d), wob, w1b, w2b,
                            side=_hgrn_w_in_side_job(hgrn_w_in, 0, n_heads, hpt))

    hq, hlf, hkk, hv, hg, w1b, wob = _hgrn_inproj_call(xs, mod[1], norm_mix[1].reshape(1, d), hgrn_lb_logits,
                                                       hw_perm, n_heads, mlp_w1, 1, hgrn_w_out, 0, hpt=hpt)
    ho, w2b = _hgrn_call(hq, hlf, hkk, hv, hg, hgrn_o_gain[0], mlp_w2, 1)
    xs = _mlp_call(xs, ho, mod[1], norm_mlp[1].reshape(1, d), wob, w1b, w2b)
    return xs.reshape(b, t, d)
```

```python
import functools
import math

import numpy as np
import jax
import jax.numpy as jnp
from jax import lax
from jax.experimental import pallas as pl
from jax.experimental.pallas import tpu as pltpu

F32 = jnp.float32
BF16 = jnp.bfloat16

EPS = 1e-6
N_MOD = 6

HEAD_DIM = 64
GROUP = 8
WINDOW = 128
LANES = 128
LOG2E = math.log2(math.e)

HG_DK = 128
CHUNK = 64

VMEM_LIMIT = 56 * 1024 * 1024


def _cparams(*sem):
    return pltpu.CompilerParams(dimension_semantics=sem, vmem_limit_bytes=VMEM_LIMIT)


def _resident(shape):
    nd = len(shape)
    return pl.BlockSpec(shape, lambda *_: (0,) * nd, pipeline_mode=pl.Buffered(1))


def _cast_job(w, layer, n_steps, step_of):
    _, r, c = w.shape
    rb = r // n_steps
    assert rb * n_steps == r and rb % 16 == 0
    in_spec = pl.BlockSpec((None, rb, c), lambda *g: (layer, step_of(*g), 0))
    out_spec = pl.BlockSpec((rb, c), lambda *g: (step_of(*g), 0))
    return in_spec, out_spec, jax.ShapeDtypeStruct((r, c), BF16)


def _sigmoid(v):
    return 0.5 * jnp.tanh(0.5 * v) + 0.5


def _silu(v):
    return v * _sigmoid(v)


def _norm_mod(x, gain, shift, scale):
    ms = jnp.mean(x * x, axis=-1, keepdims=True)
    y = x * lax.rsqrt(ms + EPS) * gain
    return y * (1.0 + scale) + shift


def _mod_block(c_ref, w_ref, b_ref):
    cond = _silu(c_ref[...])
    return jnp.sum(cond * w_ref[...], axis=0, keepdims=True) + b_ref[...]


def _mod_specs(mod_w, layer, tn, col_of):
    _, d, n = mod_w.shape
    in_specs = [
        pl.BlockSpec((d, 1), lambda *g: (0, 0)),
        pl.BlockSpec((None, d, tn), lambda *g: (layer, 0, col_of(*g))),
        pl.BlockSpec((None, 1, tn), lambda *g: (layer, 0, col_of(*g))),
    ]
    out_spec = pl.BlockSpec((1, tn), lambda *g: (0, col_of(*g)))
    return in_specs, out_spec, jax.ShapeDtypeStruct((1, n), F32)


def _mod_kernel(c_ref, w_ref, b_ref, side_ref, o_ref, sideb_ref):
    sideb_ref[...] = side_ref[...].astype(BF16)
    o_ref[...] = _mod_block(c_ref, w_ref, b_ref)


def _mod_call(c, mod_w, mod_b, layer, side_w, side_layer, tn=768):
    depth, d, n = mod_w.shape
    nj = n // tn
    m_in, m_out, m_shape = _mod_specs(mod_w, layer, tn, lambda j: j)
    s_in, s_out, s_shape = _cast_job(side_w, side_layer, nj, lambda j: j)
    out, side_b = pl.pallas_call(
        _mod_kernel,
        grid=(nj,),
        in_specs=m_in + [s_in],
        out_specs=[m_out, s_out],
        out_shape=[m_shape, s_shape],
        compiler_params=_cparams("arbitrary"),
        name="mod_proj",
    )(c.reshape(d, 1), mod_w, mod_b.reshape(depth, 1, n), side_w)
    return out.reshape(N_MOD, d), side_b


PROJ_COLS = 256


def _qkv_finish(proj, qkg_ref, p_ref, pt_ref, q_dim, kv_dim, scale):
    qk = proj[:, :q_dim + kv_dim]
    ss = jnp.dot((qk * qk).astype(BF16), p_ref[...], preferred_element_type=F32)
    inv = lax.rsqrt(ss * (1.0 / HEAD_DIM) + EPS)
    inv_hi = inv.astype(BF16)
    inv_lo = (inv - inv_hi.astype(F32)).astype(BF16)
    inv_b = jnp.dot(jnp.concatenate([inv_hi, inv_lo], axis=1), pt_ref[...],
                    preferred_element_type=F32)
    qkn = qk * inv_b * qkg_ref[...]
    return ((qkn[:, :q_dim] * scale).astype(BF16), qkn[:, q_dim:].astype(BF16),
            proj[:, q_dim + kv_dim:].astype(BF16))


def _attn_layer_kernel(sink_ref, x_ref, mod_ref, gain_ref, wb_ref, qkg_ref, p_ref, pt_ref,
                       wo_ref, w1_ref, w2_ref, hw_ref,
                       o_ref, wob_ref, w1b_ref, w2b_ref, hwb_ref,
                       q_s, k_s, v_s, kt_s, vt_s, bias_ref, *, n_kv, n_blk, scale, hg_tiles, hg_cw):
    s = pl.program_id(0)
    wob_ref[...] = wo_ref[...].astype(BF16)
    w1b_ref[...] = w1_ref[...].astype(BF16)
    w2b_ref[...] = w2_ref[...].astype(BF16)
    for j in range(hg_tiles):
        for sg in range(4):
            dst = (j * 4 + sg) * hg_cw
            src = (sg * hg_tiles + j) * hg_cw
            hwb_ref[:, dst:dst + hg_cw] = hw_ref[:, src:src + hg_cw].astype(BF16)
    n_heads = n_kv * GROUP
    q_dim = n_heads * HEAD_DIM
    kv_dim = n_kv * HEAD_DIM
    row = lax.broadcasted_iota(jnp.int32, (WINDOW, WINDOW), 0)
    col = lax.broadcasted_iota(jnp.int32, (WINDOW, WINDOW), 1)
    in_cur = col <= row

    @pl.when(s == 0)
    def _():
        dist = jnp.where(in_cur, row - col, row - col + WINDOW).astype(F32)
        for hq in range(n_heads):
            slope = LOG2E * 2.0 ** (-8.0 * (hq + 1) / n_heads)
            bias_ref[hq] = -slope * dist
            bias_ref[n_heads + hq] = jnp.where(in_cur, -slope * dist, -jnp.inf)
        q_s[...] = jnp.zeros_like(q_s)
        k_s[...] = jnp.zeros_like(k_s)
        v_s[...] = jnp.zeros_like(v_s)
        kt_s[...] = jnp.zeros_like(kt_s)
        vt_s[...] = jnp.zeros_like(vt_s)

    wslot = s % 2
    rslot = 1 - wslot

    lane = lax.broadcasted_iota(jnp.int32, (WINDOW, LANES), 1)
    low = lane < HEAD_DIM
    one_lo = jnp.where(low, 1.0, 0.0).astype(BF16)
    one_hi = jnp.where(low, 0.0, 1.0).astype(BF16)
    den_rhs = jnp.concatenate([one_lo, one_lo, one_hi, one_hi], axis=0)
    zero = jnp.zeros((WINDOW, LANES), BF16)

    def halves(t, g):
        r = pltpu.roll(t, HEAD_DIM, axis=1)
        if g % 2 == 0:
            return jnp.where(low, t, zero), jnp.where(low, zero, r)
        return jnp.where(low, r, zero), jnp.where(low, zero, t)

    units = [(b_, g_) for b_ in range(n_blk) for g_ in range(n_kv)]
    n_pairs = GROUP // 2

    h = _norm_mod(x_ref[...], gain_ref[...], mod_ref[0:1, :], mod_ref[1:2, :]).astype(BF16)
    n_cols = (q_dim + 2 * kv_dim) // PROJ_COLS
    proj_chunks = []

    def project_next():
        if len(proj_chunks) < n_cols:
            cs = slice(len(proj_chunks) * PROJ_COLS, (len(proj_chunks) + 1) * PROJ_COLS)
            proj_chunks.append(jnp.dot(h, wb_ref[:, cs], preferred_element_type=F32))

    def logits_stage(blk, g):
        qrows = slice(blk * WINDOW, (blk + 1) * WINDOW)
        sl = slice((g // 2) * LANES, (g // 2 + 1) * LANES)
        if blk == 0:
            k_prev, v_prev = kt_s[:, sl], vt_s[:, sl]
            bias_off = jnp.where(s == 1, n_heads, 0)
        else:
            prows = slice((blk - 1) * WINDOW, blk * WINDOW)
            k_prev, v_prev = k_s[rslot, prows, sl], v_s[rslot, prows, sl]
            bias_off = 0
        kc_lo, kc_hi = halves(k_s[rslot, qrows, sl], g)
        kp_lo, kp_hi = halves(k_prev, g)
        vc_lo, vc_hi = halves(v_s[rslot, qrows, sl], g)
        vp_lo, vp_hi = halves(v_prev, g)
        k_rhs = jnp.concatenate([kc_lo, kc_hi, kp_lo, kp_hi], axis=0)
        v_rhs = jnp.concatenate([vc_lo, vp_lo, vc_hi, vp_hi], axis=0)
        q_base = g * GROUP * HEAD_DIM
        qs = jnp.concatenate(
            [q_s[rslot, qrows, q_base + t * LANES: q_base + (t + 1) * LANES] for t in range(n_pairs)], axis=0)
        logits = lax.dot_general(qs, k_rhs, (((1,), (1,)), ((), ())), preferred_element_type=F32)
        return logits, v_rhs, bias_off

    def softmax_pv_stage(blk, g, logits, v_rhs, bias_off):
        qrows = slice(blk * WINDOW, (blk + 1) * WINDOW)
        q_base = g * GROUP * HEAD_DIM
        pms, sink_terms = [], []
        for t in range(n_pairs):
            rows = slice(t * WINDOW, (t + 1) * WINDOW)
            probs, mx_e = [], []
            for e in range(2):
                hq = g * GROUP + 2 * t + e
                l_cur = logits[rows, e * WINDOW:(e + 1) * WINDOW]
                l_prev = logits[rows, (2 + e) * WINDOW:(3 + e) * WINDOW]
                sc = jnp.where(in_cur, l_cur, l_prev) + bias_ref[hq + bias_off]
                mx = jnp.max(sc, axis=1, keepdims=True)
                pe = jnp.exp2(sc - mx).astype(BF16)
                probs.append(jnp.where(in_cur, pe, zero))
                probs.append(jnp.where(in_cur, zero, pe))
                mx_e.append(mx)
            pms.append(jnp.concatenate(probs, axis=1))
            hq0 = g * GROUP + 2 * t
            sink_l = jnp.where(low[0:1], sink_ref[0, hq0] * LOG2E, sink_ref[0, hq0 + 1] * LOG2E)
            sink_terms.append(jnp.exp2(sink_l - jnp.where(low, mx_e[0], mx_e[1])))
        res = jnp.dot(jnp.concatenate(pms, axis=0), jnp.concatenate([v_rhs, den_rhs], axis=1),
                      preferred_element_type=F32)
        for t in range(n_pairs):
            rows = slice(t * WINDOW, (t + 1) * WINDOW)
            den = res[rows, LANES:] + sink_terms[t]
            o_ref[qrows, q_base + t * LANES: q_base + (t + 1) * LANES] = (res[rows, :LANES] / den).astype(BF16)

    staged = logits_stage(*units[0])
    project_next()
    project_next()
    for u, (blk, g) in enumerate(units):
        cur = staged
        project_next()
        if u + 1 < len(units):
            staged = logits_stage(*units[u + 1])
        softmax_pv_stage(blk, g, *cur)
    while len(proj_chunks) < n_cols:
        project_next()

    last = slice((n_blk - 1) * WINDOW, n_blk * WINDOW)
    kt_s[...] = k_s[rslot, last, :]
    vt_s[...] = v_s[rslot, last, :]

    qn, kn, vn = _qkv_finish(jnp.concatenate(proj_chunks, axis=1), qkg_ref, p_ref, pt_ref, q_dim, kv_dim, scale)
    q_s[wslot] = qn
    k_s[wslot] = kn
    v_s[wslot] = vn


def _attn_layer_call(x, mod_l, gain, w_in_b, qk_gain, sinks, w_out_all, mlp_w1, mlp_w2, layer,
                     hgrn_w_in, hg_layer, hg_tiles, n_blk=2):
    t, d = x.shape
    n = w_in_b.shape[1]
    kv_dim = (n - d) // 2
    q_dim = d
    n_kv = kv_dim // HEAD_DIM
    n_heads = n_kv * GROUP
    assert n_kv % 2 == 0 and q_dim == n_heads * HEAD_DIM and (q_dim + kv_dim) // HEAD_DIM <= LANES
    rows = n_blk * WINDOW
    nt = t // rows
    ind = (np.arange(q_dim + kv_dim)[:, None] // HEAD_DIM == np.arange(LANES)[None, :])
    p = jnp.asarray(ind, dtype=BF16)
    pt = jnp.asarray(np.concatenate([ind.T, ind.T], axis=0), dtype=BF16)
    proj_tile = lambda s: (jnp.minimum(s, nt - 1), 0)
    attn_tile = lambda s: (jnp.maximum(s - 1, 0), 0)
    cast_step = lambda s: jnp.minimum(s, nt - 1)
    wo_in, wo_out, wo_shape = _cast_job(w_out_all, layer, nt, cast_step)
    w1_in, w1_out, w1_shape = _cast_job(mlp_w1, layer, nt, cast_step)
    w2_in, w2_out, w2_shape = _cast_job(mlp_w2, layer, nt, cast_step)
    hw_in, hw_out, hw_shape = _cast_job(hgrn_w_in, hg_layer, nt, cast_step)
    hg_cw = hgrn_w_in.shape[2] // (4 * hg_tiles)
    return pl.pallas_call(
        functools.partial(_attn_layer_kernel, n_kv=n_kv, n_blk=n_blk, scale=LOG2E / math.sqrt(HEAD_DIM),
                          hg_tiles=hg_tiles, hg_cw=hg_cw),
        grid=(nt + 1,),
        in_specs=[
            pl.BlockSpec(memory_space=pltpu.SMEM),
            pl.BlockSpec((rows, d), proj_tile),
            _resident((N_MOD, d)),
            _resident((1, d)),
            _resident((d, n)),
            _resident((1, q_dim + kv_dim)),
            _resident((q_dim + kv_dim, LANES)),
            _resident((2 * LANES, q_dim + kv_dim)),
            wo_in, w1_in, w2_in, hw_in,
        ],
        out_specs=[pl.BlockSpec((rows, q_dim), attn_tile), wo_out, w1_out, w2_out, hw_out],
        out_shape=[jax.ShapeDtypeStruct((t, q_dim), BF16), wo_shape, w1_shape, w2_shape, hw_shape],
        scratch_shapes=[
            pltpu.VMEM((2, rows, q_dim), BF16),
            pltpu.VMEM((2, rows, kv_dim), BF16),
            pltpu.VMEM((2, rows, kv_dim), BF16),
            pltpu.VMEM((WINDOW, kv_dim), BF16),
            pltpu.VMEM((WINDOW, kv_dim), BF16),
            pltpu.VMEM((2 * n_heads, WINDOW, WINDOW), F32),
        ],
        compiler_params=_cparams("arbitrary"),
        name="swa_attention_layer",
    )(sinks.reshape(1, n_heads), x, mod_l, gain, w_in_b, qk_gain, p, pt, w_out_all, mlp_w1, mlp_w2, hgrn_w_in)


def _mlp_kernel(*refs, has_mod, n_sub):
    if has_mod:
        (x_ref, a_ref, mod_ref, gain_ref, wo_ref, w1_ref, w2_ref, c_ref, mw_ref, mb_ref,
         o_ref, mo_ref, h_ref) = refs
        mo_ref[...] = _mod_block(c_ref, mw_ref, mb_ref)
    else:
        x_ref, a_ref, mod_ref, gain_ref, wo_ref, w1_ref, w2_ref, o_ref, h_ref = refs
    f = pl.program_id(1)
    gate = mod_ref[5:6, :]

    def ffn(h):
        a = jnp.maximum(jnp.dot(h, w1_ref[...], preferred_element_type=F32), 0.0)
        return gate * jnp.dot((a * a).astype(BF16), w2_ref[...], preferred_element_type=F32)

    @pl.when(f == 0)
    def _():
        sub = x_ref.shape[0] // n_sub
        for s in range(n_sub):
            rs = slice(s * sub, (s + 1) * sub)
            x1 = x_ref[rs, :] + mod_ref[2:3, :] * jnp.dot(a_ref[rs, :], wo_ref[...], preferred_element_type=F32)
            h = _norm_mod(x1, gain_ref[...], mod_ref[3:4, :], mod_ref[4:5, :]).astype(BF16)
            h_ref[rs, :] = h
            o_ref[rs, :] = x1 + ffn(h)

    @pl.when(f > 0)
    def _():
        o_ref[...] += ffn(h_ref[...])


def _mlp_call(x, a, mod_l, gain, w_out, w1, w2, mod_job=None, tm=512, tf=1024, n_sub=2):
    t, d = x.shape
    kdim = a.shape[1]
    dff = w1.shape[1]
    nf = dff // tf
    in_specs = [
        pl.BlockSpec((tm, d), lambda i, f: (i, 0)),
        pl.BlockSpec((tm, kdim), lambda i, f: (i, 0)),
        _resident((N_MOD, d)),
        _resident((1, d)),
        _resident((kdim, d)),
        pl.BlockSpec((d, tf), lambda i, f: (0, f)),
        pl.BlockSpec((tf, d), lambda i, f: (f, 0)),
    ]
    out_specs = [pl.BlockSpec((tm, d), lambda i, f: (i, 0))]
    out_shape = [jax.ShapeDtypeStruct((t, d), F32)]
    args = [x, a, mod_l, gain, w_out, w1, w2]
    if mod_job is not None:
        c, mod_w, mod_b, layer = mod_job
        depth, _, n = mod_w.shape
        steps = (t // tm) * nf
        tn = next(w for w in range(LANES, n + 1, LANES) if n % w == 0 and n // w <= steps)
        n_blocks = n // tn
        m_in, m_out, m_shape = _mod_specs(mod_w, layer, tn, lambda i, f: jnp.minimum(i * nf + f, n_blocks - 1))
        in_specs += m_in
        out_specs.append(m_out)
        out_shape.append(m_shape)
        args += [c.reshape(d, 1), mod_w, mod_b.reshape(depth, 1, n)]
    outs = pl.pallas_call(
        functools.partial(_mlp_kernel, has_mod=mod_job is not None, n_sub=n_sub),
        grid=(t // tm, nf),
        in_specs=in_specs,
        out_specs=out_specs,
        out_shape=out_shape,
        scratch_shapes=[pltpu.VMEM((tm, d), BF16)],
        compiler_params=_cparams("arbitrary", "arbitrary"),
        name="outproj_relu2_mlp",
    )(*args)
    if mod_job is None:
        return outs[0]
    return outs[0], outs[1].reshape(N_MOD, d)


def _layer_lower_bound(lbl_ref):
    l0 = lbl_ref[0]
    l1 = lbl_ref[1]
    lm = jnp.maximum(l0, l1)
    e0 = jnp.exp(l0 - lm)
    e1 = jnp.exp(l1 - lm)
    p0 = e0 / (e0 + e1)
    p1 = e1 / (e0 + e1)
    return (p0 + p1) - p0


def _hgrn_inproj_kernel(x_ref, mod_ref, gain_ref, lbl_ref, w_ref, side_ref, side2_ref,
                        q_ref, lf_ref, kk_ref, v_ref, g_ref, sideb_ref, side2b_ref, h_ref, *, hpt, scale, n_sub):
    j = pl.program_id(1)
    sideb_ref[...] = side_ref[...].astype(BF16)
    side2b_ref[...] = side2_ref[...].astype(BF16)
    lb = _layer_lower_bound(lbl_ref)
    width = hpt * HG_DK

    def project(h, rs):
        res = jnp.dot(h, w_ref[...], preferred_element_type=F32)
        for hh in range(hpt):
            seg = lambda s: res[:, s * width + hh * HG_DK: s * width + (hh + 1) * HG_DK]
            q_ref[hh, rs, :] = (_silu(seg(0)) * scale).astype(BF16)
            forget = lb[hh] + (1.0 - lb[hh]) * _sigmoid(seg(1))
            kk_ref[hh, rs, :] = 1.0 - forget
            lf_ref[hh, rs, :] = jnp.log(forget)
            v_ref[hh, rs, :] = seg(2).astype(BF16)
            g_ref[hh, rs, :] = _silu(seg(3)).astype(BF16)

    @pl.when(j == 0)
    def _():
        sub = x_ref.shape[0] // n_sub
        for s in range(n_sub):
            rs = slice(s * sub, (s + 1) * sub)
            h = _norm_mod(x_ref[rs, :], gain_ref[...], mod_ref[0:1, :], mod_ref[1:2, :]).astype(BF16)
            h_ref[rs, :] = h
            project(h, rs)

    @pl.when(j > 0)
    def _():
        project(h_ref[...], slice(None))


def _hgrn_inproj_call(x, mod_l, gain, lb_logits, w_perm, n_heads, side_w, side_layer, side2_w, side2_layer,
                      tm=1024, hpt=2):
    t, d = x.shape
    depth = lb_logits.shape[0]
    assert depth == 2
    tn = 4 * hpt * HG_DK
    n_inner = n_heads // hpt
    head_major = lambda dt: jax.ShapeDtypeStruct((n_heads, t, HG_DK), dt)
    ospec = pl.BlockSpec((hpt, tm, HG_DK), lambda i, j: (j, i, 0))
    s_in, s_out, s_shape = _cast_job(side_w, side_layer, (t // tm) * n_inner, lambda i, j: i * n_inner + j)
    s2_in, s2_out, s2_shape = _cast_job(side2_w, side2_layer, (t // tm) * n_inner,
                                        lambda i, j: i * n_inner + j)
    return pl.pallas_call(
        functools.partial(_hgrn_inproj_kernel, hpt=hpt, scale=1.0 / math.sqrt(HG_DK), n_sub=4),
        grid=(t // tm, n_inner),
        in_specs=[
            pl.BlockSpec((tm, d), lambda i, j: (i, 0)),
            _resident((N_MOD, d)),
            _resident((1, d)),
            pl.BlockSpec((depth, hpt, 1, HG_DK), lambda i, j: (0, j, 0, 0)),
            pl.BlockSpec((d, tn), lambda i, j: (0, j)),
            s_in, s2_in,
        ],
        out_specs=[ospec] * 5 + [s_out, s2_out],
        out_shape=[head_major(BF16), head_major(F32), head_major(F32), head_major(BF16), head_major(BF16),
                   s_shape, s2_shape],
        scratch_shapes=[pltpu.VMEM((tm, d), BF16)],
        compiler_params=_cparams("arbitrary", "arbitrary"),
        name="hgrn_inproj",
    )(x, mod_l, gain, lb_logits.reshape(depth, n_heads, 1, HG_DK), w_perm, side_w, side2_w)


def _cum_matrix(tr):
    n_chunks = tr // CHUNK
    r = np.arange(tr)[:, None]
    c = np.arange(tr)[None, :]
    same = (r // CHUNK) == (c // CHUNK)
    incl = same & (c <= r)
    pivot = same & (c % CHUNK <= CHUNK // 2 - 1)
    rows = np.zeros((2 * n_chunks, tr), np.float32)
    for ch in range(n_chunks):
        rows[2 * ch, ch * CHUNK: ch * CHUNK + CHUNK // 2] = 1.0
        rows[2 * ch + 1, ch * CHUNK: (ch + 1) * CHUNK] = 1.0
    mats = np.concatenate([incl.astype(np.float32) - pivot.astype(np.float32), rows], axis=0)
    return jnp.asarray(mats, dtype=BF16)


def _hgrn_kernel(gain_ref, cm_ref, q_ref, lf_ref, kk_ref, v_ref, g_ref, side_ref,
                 o_ref, sideb_ref, st_ref, *, hb, tr):
    tstep = pl.program_id(1)
    sideb_ref[...] = side_ref[...].astype(BF16)

    @pl.when(tstep == 0)
    def _():
        st_ref[...] = jnp.zeros_like(st_ref)

    rowc = lax.broadcasted_iota(jnp.int32, (CHUNK, 2 * CHUNK), 0)
    colc = lax.broadcasted_iota(jnp.int32, (CHUNK, 2 * CHUNK), 1)
    causal2 = (colc % CHUNK) <= rowc
    n_chunks = tr // CHUNK
    zc = jnp.zeros((CHUNK, HG_DK), BF16)
    zs = jnp.zeros((HG_DK, HG_DK), BF16)

    def block_diag(a, b, z):
        return jnp.concatenate([jnp.concatenate([a, z], axis=1), jnp.concatenate([z, b], axis=1)], axis=0)

    n_pairs = hb // 2
    pairs = [(2 * pp, 2 * pp + 1) for pp in range(n_pairs)]
    chunk_rows = [slice(c * CHUNK, (c + 1) * CHUNK) for c in range(n_chunks)]

    cums = []
    for hh in range(hb):
        logf = lf_ref[hh]
        p_hi = logf.astype(BF16)
        p_lo = (logf - p_hi.astype(F32)).astype(BF16)
        cm = jnp.dot(cm_ref[...], jnp.concatenate([p_hi, p_lo], axis=1), preferred_element_type=F32)
        cums.append(cm[:, :HG_DK] + cm[:, HG_DK:])

    qe, ke, qb, ku, dec = [], [], [], [], []
    for hh in range(hb):
        bmp = cums[hh][:tr]
        qe_f = q_ref[hh].astype(F32) * jnp.exp(bmp)
        ke_f = kk_ref[hh] * jnp.exp(-bmp)
        qb_h, ku_h, dec_h = [], [], []
        for c, rs in enumerate(chunk_rows):
            piv = cums[hh][tr + 2 * c: tr + 2 * c + 1]
            blast = cums[hh][tr + 2 * c + 1: tr + 2 * c + 2]
            qb_h.append((qe_f[rs] * jnp.exp(piv)).astype(BF16))
            ku_h.append((ke_f[rs] * jnp.exp(blast - piv)).astype(BF16))
            dec_h.append(jnp.exp(blast))
        qe.append(qe_f.astype(BF16))
        ke.append(ke_f.astype(BF16))
        qb.append(qb_h)
        ku.append(ku_h)
        dec.append(dec_h)

    amat = [[None] * n_chunks for _ in pairs]
    upd = [[None] * n_chunks for _ in pairs]
    for c, rs in enumerate(chunk_rows):
        for pp, (h0, h1) in enumerate(pairs):
            v0 = v_ref[h0, rs, :]
            v1 = v_ref[h1, rs, :]
            a = lax.dot_general(jnp.concatenate([qe[h0][rs], qe[h1][rs]], axis=1),
                                block_diag(ke[h0][rs], ke[h1][rs], zc),
                                (((1,), (1,)), ((), ())), preferred_element_type=F32)
            amat[pp][c] = jnp.where(causal2, a, 0.0).astype(BF16)
            upd[pp][c] = lax.dot_general(jnp.concatenate([v0, v1], axis=0),
                                         block_diag(ku[h0][c], ku[h1][c], zc),
                                         (((0,), (0,)), ((), ())), preferred_element_type=F32)

    for c, rs in enumerate(chunk_rows):
        for pp, (h0, h1) in enumerate(pairs):
            st = st_ref[pp]
            st_b = st.astype(BF16)
            o = (jnp.dot(amat[pp][c], block_diag(v_ref[h0, rs, :], v_ref[h1, rs, :], zc),
                         preferred_element_type=F32)
                 + lax.dot_general(jnp.concatenate([qb[h0][c], qb[h1][c]], axis=1),
                                   block_diag(st_b[:, :HG_DK], st_b[:, HG_DK:], zs),
                                   (((1,), (1,)), ((), ())), preferred_element_type=F32))
            st_ref[pp] = st * jnp.concatenate([dec[h0][c], dec[h1][c]], axis=1) + upd[pp][c]
            for e, hh in enumerate((h0, h1)):
                oh = o[:, e * HG_DK:(e + 1) * HG_DK]
                on = oh * lax.rsqrt(jnp.mean(oh * oh, axis=-1, keepdims=True) + EPS) * gain_ref[hh]
                o_ref[rs, hh * HG_DK:(hh + 1) * HG_DK] = (on * g_ref[hh, rs, :].astype(F32)).astype(BF16)


def _hgrn_call(q, lf, kk, v, g, o_gain, side_w, side_layer, hb=16, tr=256):
    n_heads, t, dk = q.shape
    assert dk == HG_DK and hb % 2 == 0
    cm = _cum_matrix(tr)
    n_inner = t // tr
    blk = pl.BlockSpec((hb, tr, dk), lambda h, s: (h, s, 0))
    s_in, s_out, s_shape = _cast_job(side_w, side_layer, (n_heads // hb) * n_inner,
                                     lambda h, s: h * n_inner + s)
    return pl.pallas_call(
        functools.partial(_hgrn_kernel, hb=hb, tr=tr),
        grid=(n_heads // hb, n_inner),
        in_specs=[
            pl.BlockSpec((hb, 1, dk), lambda h, s: (h, 0, 0)),
            _resident(cm.shape),
            blk, blk, blk, blk, blk,
            s_in,
        ],
        out_specs=[pl.BlockSpec((tr, hb * dk), lambda h, s: (s, h)), s_out],
        out_shape=[jax.ShapeDtypeStruct((t, n_heads * dk), BF16), s_shape],
        scratch_shapes=[pltpu.VMEM((hb // 2, dk, 2 * dk), F32)],
        compiler_params=_cparams("arbitrary", "arbitrary"),
        name="hgrn_recurrence",
    )(o_gain.reshape(n_heads, 1, dk), cm, q, lf, kk, v, g, side_w)


def kernel(x, c, mod_w, mod_b, norm_mix, norm_mlp, attn_w_in, attn_w_out, attn_q_gain, attn_k_gain,
           attn_sinks, hgrn_w_in, hgrn_w_out, hgrn_o_gain, hgrn_lb_logits, mlp_w1, mlp_w2):
    b, t, d = x.shape
    assert b == 1 and mod_w.shape[0] == 2
    xs = x.reshape(t, d)

    mod0, w_in_b = _mod_call(c, mod_w, mod_b, 0, attn_w_in, 0)

    q_dim = attn_w_out.shape[1]
    kv_dim = (attn_w_in.shape[2] - q_dim) // 2
    qk_gain = jnp.concatenate([jnp.tile(attn_q_gain[0], q_dim // HEAD_DIM),
                               jnp.tile(attn_k_gain[0], kv_dim // HEAD_DIM)]).reshape(1, q_dim + kv_dim)
    n_heads = hgrn_o_gain.shape[1]
    hpt = 2
    att, wob, w1b, w2b, hw_perm = _attn_layer_call(xs, mod0, norm_mix[0].reshape(1, d), w_in_b, qk_gain,
                                                   attn_sinks[0], attn_w_out, mlp_w1, mlp_w2, 0,
                                                   hgrn_w_in, 0, n_heads // hpt)
    xs, mod1 = _mlp_call(xs, att, mod0, norm_mlp[0].reshape(1, d), wob, w1b, w2b,
                         mod_job=(c, mod_w, mod_b, 1))

    hq, hlf, hkk, hv, hg, w1b, wob = _hgrn_inproj_call(xs, mod1, norm_mix[1].reshape(1, d), hgrn_lb_logits,
                                                       hw_perm, n_heads, mlp_w1, 1, hgrn_w_out, 0, hpt=hpt)
    ho, w2b = _hgrn_call(hq, hlf, hkk, hv, hg, hgrn_o_gain[0], mlp_w2, 1)
    xs = _mlp_call(xs, ho, mod1, norm_mlp[1].reshape(1, d), wob, w1b, w2b)
    return xs.reshape(b, t, d)
```

```python
import functools
import math

import numpy as np
import jax
import jax.numpy as jnp
from jax import lax
from jax.experimental import pallas as pl
from jax.experimental.pallas import tpu as pltpu

F32 = jnp.float32
BF16 = jnp.bfloat16

EPS = 1e-6
N_MOD = 6

HEAD_DIM = 64
GROUP = 8
WINDOW = 128
LANES = 128
LOG2E = math.log2(math.e)

HG_DK = 128
CHUNK = 64

VMEM_LIMIT = 56 * 1024 * 1024


def _cparams(*sem):
    return pltpu.CompilerParams(dimension_semantics=sem, vmem_limit_bytes=VMEM_LIMIT)


def _resident(shape):
    nd = len(shape)
    return pl.BlockSpec(shape, lambda *_: (0,) * nd, pipeline_mode=pl.Buffered(1))


def _cast_job(w, layer, n_steps, step_of):
    _, r, c = w.shape
    rb = r // n_steps
    assert rb * n_steps == r and rb % 16 == 0
    in_spec = pl.BlockSpec((None, rb, c), lambda *g: (layer, step_of(*g), 0))
    out_spec = pl.BlockSpec((rb, c), lambda *g: (step_of(*g), 0))
    return in_spec, out_spec, jax.ShapeDtypeStruct((r, c), BF16)


def _sigmoid(v):
    return 0.5 * jnp.tanh(0.5 * v) + 0.5


def _silu(v):
    return v * _sigmoid(v)


def _norm_mod(x, gain, shift, scale):
    ms = jnp.mean(x * x, axis=-1, keepdims=True)
    y = x * lax.rsqrt(ms + EPS) * gain
    return y * (1.0 + scale) + shift


def _mod_block(c_ref, w_ref, b_ref):
    cond = _silu(c_ref[...])
    return jnp.sum(cond * w_ref[...], axis=0, keepdims=True) + b_ref[...]


def _mod_specs(mod_w, layer, tn, col_of):
    _, d, n = mod_w.shape
    in_specs = [
        pl.BlockSpec((d, 1), lambda *g: (0, 0)),
        pl.BlockSpec((None, d, tn), lambda *g: (layer, 0, col_of(*g))),
        pl.BlockSpec((None, 1, tn), lambda *g: (layer, 0, col_of(*g))),
    ]
    out_spec = pl.BlockSpec((1, tn), lambda *g: (0, col_of(*g)))
    return in_specs, out_spec, jax.ShapeDtypeStruct((1, n), F32)


def _mod_kernel(c_ref, w_ref, b_ref, side_ref, o_ref, sideb_ref):
    sideb_ref[...] = side_ref[...].astype(BF16)
    o_ref[...] = _mod_block(c_ref, w_ref, b_ref)


def _mod_call(c, mod_w, mod_b, layer, side_w, side_layer, tn=768):
    depth, d, n = mod_w.shape
    nj = n // tn
    m_in, m_out, m_shape = _mod_specs(mod_w, layer, tn, lambda j: j)
    s_in, s_out, s_shape = _cast_job(side_w, side_layer, nj, lambda j: j)
    out, side_b = pl.pallas_call(
        _mod_kernel,
        grid=(nj,),
        in_specs=m_in + [s_in],
        out_specs=[m_out, s_out],
        out_shape=[m_shape, s_shape],
        compiler_params=_cparams("arbitrary"),
        name="mod_proj",
    )(c.reshape(d, 1), mod_w, mod_b.reshape(depth, 1, n), side_w)
    return out.reshape(N_MOD, d), side_b


PROJ_COLS = 256


def _qkv_finish(proj, qkg_ref, p_ref, pt_ref, q_dim, kv_dim, scale):
    qk = proj[:, :q_dim + kv_dim]
    ss = jnp.dot((qk * qk).astype(BF16), p_ref[...], preferred_element_type=F32)
    inv = lax.rsqrt(ss * (1.0 / HEAD_DIM) + EPS)
    inv_hi = inv.astype(BF16)
    inv_lo = (inv - inv_hi.astype(F32)).astype(BF16)
    inv_b = jnp.dot(jnp.concatenate([inv_hi, inv_lo], axis=1), pt_ref[...],
                    preferred_element_type=F32)
    qkn = qk * inv_b * qkg_ref[...]
    return ((qkn[:, :q_dim] * scale).astype(BF16), qkn[:, q_dim:].astype(BF16),
            proj[:, q_dim + kv_dim:].astype(BF16))


def _attn_layer_kernel(sink_ref, x_ref, mod_ref, gain_ref, wb_ref, qkg_ref, p_ref, pt_ref,
                       wo_ref, w1_ref, w2_ref, hw_ref,
                       o_ref, wob_ref, w1b_ref, w2b_ref, hwb_ref,
                       q_s, k_s, v_s, kt_s, vt_s, bias_ref, *, n_kv, n_blk, scale, hg_tiles, hg_cw):
    s = pl.program_id(0)
    wob_ref[...] = wo_ref[...].astype(BF16)
    w1b_ref[...] = w1_ref[...].astype(BF16)
    w2b_ref[...] = w2_ref[...].astype(BF16)
    for j in range(hg_tiles):
        for sg in range(4):
            dst = (j * 4 + sg) * hg_cw
            src = (sg * hg_tiles + j) * hg_cw
            hwb_ref[:, dst:dst + hg_cw] = hw_ref[:, src:src + hg_cw].astype(BF16)
    n_heads = n_kv * GROUP
    q_dim = n_heads * HEAD_DIM
    kv_dim = n_kv * HEAD_DIM
    row = lax.broadcasted_iota(jnp.int32, (WINDOW, WINDOW), 0)
    col = lax.broadcasted_iota(jnp.int32, (WINDOW, WINDOW), 1)
    in_cur = col <= row

    @pl.when(s == 0)
    def _():
        dist = jnp.where(in_cur, row - col, row - col + WINDOW).astype(F32)
        for hq in range(n_heads):
            slope = LOG2E * 2.0 ** (-8.0 * (hq + 1) / n_heads)
            bias_ref[hq] = -slope * dist
            bias_ref[n_heads + hq] = jnp.where(in_cur, -slope * dist, -jnp.inf)
        q_s[...] = jnp.zeros_like(q_s)
        k_s[...] = jnp.zeros_like(k_s)
        v_s[...] = jnp.zeros_like(v_s)
        kt_s[...] = jnp.zeros_like(kt_s)
        vt_s[...] = jnp.zeros_like(vt_s)

    wslot = s % 2
    rslot = 1 - wslot

    lane = lax.broadcasted_iota(jnp.int32, (WINDOW, LANES), 1)
    low = lane < HEAD_DIM
    one_lo = jnp.where(low, 1.0, 0.0).astype(BF16)
    one_hi = jnp.where(low, 0.0, 1.0).astype(BF16)
    den_rhs = jnp.concatenate([one_lo, one_lo, one_hi, one_hi], axis=0)
    zero = jnp.zeros((WINDOW, LANES), BF16)

    def halves(t, g):
        r = pltpu.roll(t, HEAD_DIM, axis=1)
        if g % 2 == 0:
            return jnp.where(low, t, zero), jnp.where(low, zero, r)
        return jnp.where(low, r, zero), jnp.where(low, zero, t)

    units = [(b_, g_) for b_ in range(n_blk) for g_ in range(n_kv)]
    n_pairs = GROUP // 2

    h = _norm_mod(x_ref[...], gain_ref[...], mod_ref[0:1, :], mod_ref[1:2, :]).astype(BF16)
    n_cols = (q_dim + 2 * kv_dim) // PROJ_COLS
    proj_chunks = []

    def project_next():
        if len(proj_chunks) < n_cols:
            cs = slice(len(proj_chunks) * PROJ_COLS, (len(proj_chunks) + 1) * PROJ_COLS)
            proj_chunks.append(jnp.dot(h, wb_ref[:, cs], preferred_element_type=F32))

    def logits_stage(blk, g):
        qrows = slice(blk * WINDOW, (blk + 1) * WINDOW)
        sl = slice((g // 2) * LANES, (g // 2 + 1) * LANES)
        if blk == 0:
            k_prev, v_prev = kt_s[:, sl], vt_s[:, sl]
            bias_off = jnp.where(s == 1, n_heads, 0)
        else:
            prows = slice((blk - 1) * WINDOW, blk * WINDOW)
            k_prev, v_prev = k_s[rslot, prows, sl], v_s[rslot, prows, sl]
            bias_off = 0
        kc_lo, kc_hi = halves(k_s[rslot, qrows, sl], g)
        kp_lo, kp_hi = halves(k_prev, g)
        vc_lo, vc_hi = halves(v_s[rslot, qrows, sl], g)
        vp_lo, vp_hi = halves(v_prev, g)
        k_rhs = jnp.concatenate([kc_lo, kc_hi, kp_lo, kp_hi], axis=0)
        v_rhs = jnp.concatenate([vc_lo, vp_lo, vc_hi, vp_hi], axis=0)
        q_base = g * GROUP * HEAD_DIM
        qs = jnp.concatenate(
            [q_s[rslot, qrows, q_base + t * LANES: q_base + (t + 1) * LANES] for t in range(n_pairs)], axis=0)
        logits = lax.dot_general(qs, k_rhs, (((1,), (1,)), ((), ())), preferred_element_type=F32)
        return logits, v_rhs, bias_off

    def softmax_pv_stage(blk, g, logits, v_rhs, bias_off):
        qrows = slice(blk * WINDOW, (blk + 1) * WINDOW)
        q_base = g * GROUP * HEAD_DIM
        pms, sink_terms = [], []
        for t in range(n_pairs):
            rows = slice(t * WINDOW, (t + 1) * WINDOW)
            probs, mx_e = [], []
            for e in range(2):
                hq = g * GROUP + 2 * t + e
                l_cur = logits[rows, e * WINDOW:(e + 1) * WINDOW]
                l_prev = logits[rows, (2 + e) * WINDOW:(3 + e) * WINDOW]
                sc = jnp.where(in_cur, l_cur, l_prev) + bias_ref[hq + bias_off]
                mx = jnp.max(sc, axis=1, keepdims=True)
                pe = jnp.exp2(sc - mx).astype(BF16)
                probs.append(jnp.where(in_cur, pe, zero))
                probs.append(jnp.where(in_cur, zero, pe))
                mx_e.append(mx)
            pms.append(jnp.concatenate(probs, axis=1))
            hq0 = g * GROUP + 2 * t
            sink_l = jnp.where(low[0:1], sink_ref[0, hq0] * LOG2E, sink_ref[0, hq0 + 1] * LOG2E)
            sink_terms.append(jnp.exp2(sink_l - jnp.where(low, mx_e[0], mx_e[1])))
        res = jnp.dot(jnp.concatenate(pms, axis=0), jnp.concatenate([v_rhs, den_rhs], axis=1),
                      preferred_element_type=F32)
        for t in range(n_pairs):
            rows = slice(t * WINDOW, (t + 1) * WINDOW)
            den = res[rows, LANES:] + sink_terms[t]
            o_ref[qrows, q_base + t * LANES: q_base + (t + 1) * LANES] = (res[rows, :LANES] / den).astype(BF16)

    staged = logits_stage(*units[0])
    project_next()
    project_next()
    for u, (blk, g) in enumerate(units):
        cur = staged
        project_next()
        if u + 1 < len(units):
            staged = logits_stage(*units[u + 1])
        softmax_pv_stage(blk, g, *cur)
    while len(proj_chunks) < n_cols:
        project_next()

    last = slice((n_blk - 1) * WINDOW, n_blk * WINDOW)
    kt_s[...] = k_s[rslot, last, :]
    vt_s[...] = v_s[rslot, last, :]

    qn, kn, vn = _qkv_finish(jnp.concatenate(proj_chunks, axis=1), qkg_ref, p_ref, pt_ref, q_dim, kv_dim, scale)
    q_s[wslot] = qn
    k_s[wslot] = kn
    v_s[wslot] = vn


def _attn_layer_call(x, mod_l, gain, w_in_b, qk_gain, sinks, w_out_all, mlp_w1, mlp_w2, layer,
                     hgrn_w_in, hg_layer, hg_tiles, n_blk=2):
    t, d = x.shape
    n = w_in_b.shape[1]
    kv_dim = (n - d) // 2
    q_dim = d
    n_kv = kv_dim // HEAD_DIM
    n_heads = n_kv * GROUP
    assert n_kv % 2 == 0 and q_dim == n_heads * HEAD_DIM and (q_dim + kv_dim) // HEAD_DIM <= LANES
    rows = n_blk * WINDOW
    nt = t // rows
    ind = (np.arange(q_dim + kv_dim)[:, None] // HEAD_DIM == np.arange(LANES)[None, :])
    p = jnp.asarray(ind, dtype=BF16)
    pt = jnp.asarray(np.concatenate([ind.T, ind.T], axis=0), dtype=BF16)
    proj_tile = lambda s: (jnp.minimum(s, nt - 1), 0)
    attn_tile = lambda s: (jnp.maximum(s - 1, 0), 0)
    cast_step = lambda s: jnp.minimum(s, nt - 1)
    wo_in, wo_out, wo_shape = _cast_job(w_out_all, layer, nt, cast_step)
    w1_in, w1_out, w1_shape = _cast_job(mlp_w1, layer, nt, cast_step)
    w2_in, w2_out, w2_shape = _cast_job(mlp_w2, layer, nt, cast_step)
    hw_in, hw_out, hw_shape = _cast_job(hgrn_w_in, hg_layer, nt, cast_step)
    hg_cw = hgrn_w_in.shape[2] // (4 * hg_tiles)
    return pl.pallas_call(
        functools.partial(_attn_layer_kernel, n_kv=n_kv, n_blk=n_blk, scale=LOG2E / math.sqrt(HEAD_DIM),
                          hg_tiles=hg_tiles, hg_cw=hg_cw),
        grid=(nt + 1,),
        in_specs=[
            pl.BlockSpec(memory_space=pltpu.SMEM),
            pl.BlockSpec((rows, d), proj_tile),
            _resident((N_MOD, d)),
            _resident((1, d)),
            _resident((d, n)),
            _resident((1, q_dim + kv_dim)),
            _resident((q_dim + kv_dim, LANES)),
            _resident((2 * LANES, q_dim + kv_dim)),
            wo_in, w1_in, w2_in, hw_in,
        ],
        out_specs=[pl.BlockSpec((rows, q_dim), attn_tile), wo_out, w1_out, w2_out, hw_out],
        out_shape=[jax.ShapeDtypeStruct((t, q_dim), BF16), wo_shape, w1_shape, w2_shape, hw_shape],
        scratch_shapes=[
            pltpu.VMEM((2, rows, q_dim), BF16),
            pltpu.VMEM((2, rows, kv_dim), BF16),
            pltpu.VMEM((2, rows, kv_dim), BF16),
            pltpu.VMEM((WINDOW, kv_dim), BF16),
            pltpu.VMEM((WINDOW, kv_dim), BF16),
            pltpu.VMEM((2 * n_heads, WINDOW, WINDOW), F32),
        ],
        compiler_params=_cparams("arbitrary"),
        name="swa_attention_layer",
    )(sinks.reshape(1, n_heads), x, mod_l, gain, w_in_b, qk_gain, p, pt, w_out_all, mlp_w1, mlp_w2, hgrn_w_in)


def _mlp_kernel(*refs, has_mod, n_sub):
    if has_mod:
        (x_ref, a_ref, mod_ref, gain_ref, wo_ref, w1_ref, w2_ref, c_ref, mw_ref, mb_ref,
         o_ref, mo_ref, h_ref) = refs
        step = pl.program_id(0) * pl.num_programs(1) + pl.program_id(1)
        rb = mw_ref.shape[0]

        @pl.when(step == 0)
        def _():
            mo_ref[...] = mb_ref[...]

        def side_work():
            cond = _silu(c_ref[pl.ds(pl.multiple_of(step * rb, rb), rb), :])
            mo_ref[...] += jnp.sum(cond * mw_ref[...], axis=0, keepdims=True)
    else:
        x_ref, a_ref, mod_ref, gain_ref, wo_ref, w1_ref, w2_ref, o_ref, h_ref = refs

        def side_work():
            pass
    f = pl.program_id(1)
    gate = mod_ref[5:6, :]

    def ffn(h):
        a = jnp.maximum(jnp.dot(h, w1_ref[...], preferred_element_type=F32), 0.0)
        return gate * jnp.dot((a * a).astype(BF16), w2_ref[...], preferred_element_type=F32)

    @pl.when(f == 0)
    def _():
        side_work()
        sub = x_ref.shape[0] // n_sub
        for s in range(n_sub):
            rs = slice(s * sub, (s + 1) * sub)
            x1 = x_ref[rs, :] + mod_ref[2:3, :] * jnp.dot(a_ref[rs, :], wo_ref[...], preferred_element_type=F32)
            h = _norm_mod(x1, gain_ref[...], mod_ref[3:4, :], mod_ref[4:5, :]).astype(BF16)
            h_ref[rs, :] = h
            o_ref[rs, :] = x1 + ffn(h)

    @pl.when(f > 0)
    def _():
        side_work()
        o_ref[...] += ffn(h_ref[...])


def _mlp_call(x, a, mod_l, gain, w_out, w1, w2, mod_job=None, tm=512, tf=1024, n_sub=2):
    t, d = x.shape
    kdim = a.shape[1]
    dff = w1.shape[1]
    nf = dff // tf
    in_specs = [
        pl.BlockSpec((tm, d), lambda i, f: (i, 0)),
        pl.BlockSpec((tm, kdim), lambda i, f: (i, 0)),
        _resident((N_MOD, d)),
        _resident((1, d)),
        _resident((kdim, d)),
        pl.BlockSpec((d, tf), lambda i, f: (0, f)),
        pl.BlockSpec((tf, d), lambda i, f: (f, 0)),
    ]
    out_specs = [pl.BlockSpec((tm, d), lambda i, f: (i, 0))]
    out_shape = [jax.ShapeDtypeStruct((t, d), F32)]
    args = [x, a, mod_l, gain, w_out, w1, w2]
    if mod_job is not None:
        c, mod_w, mod_b, layer = mod_job
        depth, _, n = mod_w.shape
        steps = (t // tm) * nf
        rb = d // steps
        assert rb * steps == d and rb % 8 == 0
        in_specs += [
            _resident((d, 1)),
            pl.BlockSpec((None, rb, n), lambda i, f: (layer, i * nf + f, 0)),
            pl.BlockSpec((None, 1, n), lambda i, f: (layer, 0, 0), pipeline_mode=pl.Buffered(1)),
        ]
        out_specs.append(pl.BlockSpec((1, n), lambda i, f: (0, 0)))
        out_shape.append(jax.ShapeDtypeStruct((1, n), F32))
        args += [c.reshape(d, 1), mod_w, mod_b.reshape(depth, 1, n)]
    outs = pl.pallas_call(
        functools.partial(_mlp_kernel, has_mod=mod_job is not None, n_sub=n_sub),
        grid=(t // tm, nf),
        in_specs=in_specs,
        out_specs=out_specs,
        out_shape=out_shape,
        scratch_shapes=[pltpu.VMEM((tm, d), BF16)],
        compiler_params=_cparams("arbitrary", "arbitrary"),
        name="outproj_relu2_mlp",
    )(*args)
    if mod_job is None:
        return outs[0]
    return outs[0], outs[1].reshape(N_MOD, d)


def _layer_lower_bound(lbl_ref):
    l0 = lbl_ref[0]
    l1 = lbl_ref[1]
    lm = jnp.maximum(l0, l1)
    e0 = jnp.exp(l0 - lm)
    e1 = jnp.exp(l1 - lm)
    p0 = e0 / (e0 + e1)
    p1 = e1 / (e0 + e1)
    return (p0 + p1) - p0


def _hgrn_inproj_kernel(x_ref, mod_ref, gain_ref, lbl_ref, w_ref, side_ref, side2_ref,
                        q_ref, lf_ref, kk_ref, v_ref, g_ref, sideb_ref, side2b_ref, h_ref, *, hpt, scale, n_sub):
    j = pl.program_id(1)
    lb = _layer_lower_bound(lbl_ref)
    width = hpt * HG_DK

    def project(h, rs, with_side):
        if with_side:
            sideb_ref[...] = side_ref[...].astype(BF16)
            side2b_ref[...] = side2_ref[...].astype(BF16)
        res = jnp.dot(h, w_ref[...], preferred_element_type=F32)
        for hh in range(hpt):
            seg = lambda s: res[:, s * width + hh * HG_DK: s * width + (hh + 1) * HG_DK]
            q_ref[hh, rs, :] = (_silu(seg(0)) * scale).astype(BF16)
            forget = lb[hh] + (1.0 - lb[hh]) * _sigmoid(seg(1))
            kk_ref[hh, rs, :] = 1.0 - forget
            lf_ref[hh, rs, :] = jnp.log(forget)
            v_ref[hh, rs, :] = seg(2).astype(BF16)
            g_ref[hh, rs, :] = _silu(seg(3)).astype(BF16)

    @pl.when(j == 0)
    def _():
        sub = x_ref.shape[0] // n_sub
        for s in range(n_sub):
            rs = slice(s * sub, (s + 1) * sub)
            h = _norm_mod(x_ref[rs, :], gain_ref[...], mod_ref[0:1, :], mod_ref[1:2, :]).astype(BF16)
            h_ref[rs, :] = h
            project(h, rs, s == 0)

    @pl.when(j > 0)
    def _():
        project(h_ref[...], slice(None), True)


def _hgrn_inproj_call(x, mod_l, gain, lb_logits, w_perm, n_heads, side_w, side_layer, side2_w, side2_layer,
                      tm=1024, hpt=2):
    t, d = x.shape
    depth = lb_logits.shape[0]
    assert depth == 2
    tn = 4 * hpt * HG_DK
    n_inner = n_heads // hpt
    head_major = lambda dt: jax.ShapeDtypeStruct((n_heads, t, HG_DK), dt)
    ospec = pl.BlockSpec((hpt, tm, HG_DK), lambda i, j: (j, i, 0))
    s_in, s_out, s_shape = _cast_job(side_w, side_layer, (t // tm) * n_inner, lambda i, j: i * n_inner + j)
    s2_in, s2_out, s2_shape = _cast_job(side2_w, side2_layer, (t // tm) * n_inner,
                                        lambda i, j: i * n_inner + j)
    return pl.pallas_call(
        functools.partial(_hgrn_inproj_kernel, hpt=hpt, scale=1.0 / math.sqrt(HG_DK), n_sub=4),
        grid=(t // tm, n_inner),
        in_specs=[
            pl.BlockSpec((tm, d), lambda i, j: (i, 0)),
            _resident((N_MOD, d)),
            _resident((1, d)),
            pl.BlockSpec((depth, hpt, 1, HG_DK), lambda i, j: (0, j, 0, 0)),
            pl.BlockSpec((d, tn), lambda i, j: (0, j)),
            s_in, s2_in,
        ],
        out_specs=[ospec] * 5 + [s_out, s2_out],
        out_shape=[head_major(BF16), head_major(F32), head_major(F32), head_major(BF16), head_major(BF16),
                   s_shape, s2_shape],
        scratch_shapes=[pltpu.VMEM((tm, d), BF16)],
        compiler_params=_cparams("arbitrary", "arbitrary"),
        name="hgrn_inproj",
    )(x, mod_l, gain, lb_logits.reshape(depth, n_heads, 1, HG_DK), w_perm, side_w, side2_w)


def _cum_matrix(tr):
    n_chunks = tr // CHUNK
    r = np.arange(tr)[:, None]
    c = np.arange(tr)[None, :]
    same = (r // CHUNK) == (c // CHUNK)
    incl = same & (c <= r)
    pivot = same & (c % CHUNK <= CHUNK // 2 - 1)
    rows = np.zeros((2 * n_chunks, tr), np.float32)
    for ch in range(n_chunks):
        rows[2 * ch, ch * CHUNK: ch * CHUNK + CHUNK // 2] = 1.0
        rows[2 * ch + 1, ch * CHUNK: (ch + 1) * CHUNK] = 1.0
    mats = np.concatenate([incl.astype(np.float32) - pivot.astype(np.float32), rows], axis=0)
    return jnp.asarray(mats, dtype=BF16)


def _hgrn_kernel(gain_ref, cm_ref, q_ref, lf_ref, kk_ref, v_ref, g_ref, side_ref,
                 o_ref, sideb_ref, st_ref, *, hb, tr):
    tstep = pl.program_id(1)
    sideb_ref[...] = side_ref[...].astype(BF16)

    @pl.when(tstep == 0)
    def _():
        st_ref[...] = jnp.zeros_like(st_ref)

    rowc = lax.broadcasted_iota(jnp.int32, (CHUNK, 2 * CHUNK), 0)
    colc = lax.broadcasted_iota(jnp.int32, (CHUNK, 2 * CHUNK), 1)
    causal2 = (colc % CHUNK) <= rowc
    n_chunks = tr // CHUNK
    zc = jnp.zeros((CHUNK, HG_DK), BF16)
    zs = jnp.zeros((HG_DK, HG_DK), BF16)

    def block_diag(a, b, z):
        return jnp.concatenate([jnp.concatenate([a, z], axis=1), jnp.concatenate([z, b], axis=1)], axis=0)

    n_pairs = hb // 2
    pairs = [(2 * pp, 2 * pp + 1) for pp in range(n_pairs)]
    chunk_rows = [slice(c * CHUNK, (c + 1) * CHUNK) for c in range(n_chunks)]

    cums = []
    for hh in range(hb):
        logf = lf_ref[hh]
        p_hi = logf.astype(BF16)
        p_lo = (logf - p_hi.astype(F32)).astype(BF16)
        cm = jnp.dot(cm_ref[...], jnp.concatenate([p_hi, p_lo], axis=1), preferred_element_type=F32)
        cums.append(cm[:, :HG_DK] + cm[:, HG_DK:])

    qe, ke, qb, ku, dec = [], [], [], [], []
    for hh in range(hb):
        bmp = cums[hh][:tr]
        qe_f = q_ref[hh].astype(F32) * jnp.exp(bmp)
        ke_f = kk_ref[hh] * jnp.exp(-bmp)
        qb_h, ku_h, dec_h = [], [], []
        for c, rs in enumerate(chunk_rows):
            piv = cums[hh][tr + 2 * c: tr + 2 * c + 1]
            blast = cums[hh][tr + 2 * c + 1: tr + 2 * c + 2]
            qb_h.append((qe_f[rs] * jnp.exp(piv)).astype(BF16))
            ku_h.append((ke_f[rs] * jnp.exp(blast - piv)).astype(BF16))
            dec_h.append(jnp.exp(blast))
        qe.append(qe_f.astype(BF16))
        ke.append(ke_f.astype(BF16))
        qb.append(qb_h)
        ku.append(ku_h)
        dec.append(dec_h)

    amat = [[None] * n_chunks for _ in pairs]
    upd = [[None] * n_chunks for _ in pairs]
    for c, rs in enumerate(chunk_rows):
        for pp, (h0, h1) in enumerate(pairs):
            v0 = v_ref[h0, rs, :]
            v1 = v_ref[h1, rs, :]
            a = lax.dot_general(jnp.concatenate([qe[h0][rs], qe[h1][rs]], axis=1),
                                block_diag(ke[h0][rs], ke[h1][rs], zc),
                                (((1,), (1,)), ((), ())), preferred_element_type=F32)
            amat[pp][c] = jnp.where(causal2, a, 0.0).astype(BF16)
            upd[pp][c] = lax.dot_general(jnp.concatenate([v0, v1], axis=0),
                                         block_diag(ku[h0][c], ku[h1][c], zc),
                                         (((0,), (0,)), ((), ())), preferred_element_type=F32)

    for c, rs in enumerate(chunk_rows):
        for pp, (h0, h1) in enumerate(pairs):
            st = st_ref[pp]
            st_b = st.astype(BF16)
            o = (jnp.dot(amat[pp][c], block_diag(v_ref[h0, rs, :], v_ref[h1, rs, :], zc),
                         preferred_element_type=F32)
                 + lax.dot_general(jnp.concatenate([qb[h0][c], qb[h1][c]], axis=1),
                                   block_diag(st_b[:, :HG_DK], st_b[:, HG_DK:], zs),
                                   (((1,), (1,)), ((), ())), preferred_element_type=F32))
            st_ref[pp] = st * jnp.concatenate([dec[h0][c], dec[h1][c]], axis=1) + upd[pp][c]
            for e, hh in enumerate((h0, h1)):
                oh = o[:, e * HG_DK:(e + 1) * HG_DK]
                on = oh * lax.rsqrt(jnp.mean(oh * oh, axis=-1, keepdims=True) + EPS) * gain_ref[hh]
                o_ref[rs, hh * HG_DK:(hh + 1) * HG_DK] = (on * g_ref[hh, rs, :].astype(F32)).astype(BF16)


def _hgrn_call(q, lf, kk, v, g, o_gain, side_w, side_layer, hb=16, tr=256):
    n_heads, t, dk = q.shape
    assert dk == HG_DK and hb % 2 == 0
    cm = _cum_matrix(tr)
    n_inner = t // tr
    blk = pl.BlockSpec((hb, tr, dk), lambda h, s: (h, s, 0))
    s_in, s_out, s_shape = _cast_job(side_w, side_layer, (n_heads // hb) * n_inner,
                                     lambda h, s: h * n_inner + s)
    return pl.pallas_call(
        functools.partial(_hgrn_kernel, hb=hb, tr=tr),
        grid=(n_heads // hb, n_inner),
        in_specs=[
            pl.BlockSpec((hb, 1, dk), lambda h, s: (h, 0, 0)),
            _resident(cm.shape),
            blk, blk, blk, blk, blk,
            s_in,
        ],
        out_specs=[pl.BlockSpec((tr, hb * dk), lambda h, s: (s, h)), s_out],
        out_shape=[jax.ShapeDtypeStruct((t, n_heads * dk), BF16), s_shape],
        scratch_shapes=[pltpu.VMEM((hb // 2, dk, 2 * dk), F32)],
        compiler_params=_cparams("arbitrary", "arbitrary"),
        name="hgrn_recurrence",
    )(o_gain.reshape(n_heads, 1, dk), cm, q, lf, kk, v, g, side_w)


def kernel(x, c, mod_w, mod_b, norm_mix, norm_mlp, attn_w_in, attn_w_out, attn_q_gain, attn_k_gain,
           attn_sinks, hgrn_w_in, hgrn_w_out, hgrn_o_gain, hgrn_lb_logits, mlp_w1, mlp_w2):
    b, t, d = x.shape
    assert b == 1 and mod_w.shape[0] == 2
    xs = x.reshape(t, d)

    mod0, w_in_b = _mod_call(c, mod_w, mod_b, 0, attn_w_in, 0)

    q_dim = attn_w_out.shape[1]
    kv_dim = (attn_w_in.shape[2] - q_dim) // 2
    qk_gain = jnp.concatenate([jnp.tile(attn_q_gain[0], q_dim // HEAD_DIM),
                               jnp.tile(attn_k_gain[0], kv_dim // HEAD_DIM)]).reshape(1, q_dim + kv_dim)
    n_heads = hgrn_o_gain.shape[1]
    hpt = 2
    att, wob, w1b, w2b, hw_perm = _attn_layer_call(xs, mod0, norm_mix[0].reshape(1, d), w_in_b, qk_gain,
                                                   attn_sinks[0], attn_w_out, mlp_w1, mlp_w2, 0,
                                                   hgrn_w_in, 0, n_heads // hpt)
    xs, mod1 = _mlp_call(xs, att, mod0, norm_mlp[0].reshape(1, d), wob, w1b, w2b,
                         mod_job=(c, mod_w, mod_b, 1))

    hq, hlf, hkk, hv, hg, w1b, wob = _hgrn_inproj_call(xs, mod1, norm_mix[1].reshape(1, d), hgrn_lb_logits,
                                                       hw_perm, n_heads, mlp_w1, 1, hgrn_w_out, 0, hpt=hpt)
    ho, w2b = _hgrn_call(hq, hlf, hkk, hv, hg, hgrn_o_gain[0], mlp_w2, 1)
    xs = _mlp_call(xs, ho, mod1, norm_mlp[1].reshape(1, d), wob, w1b, w2b)
    return xs.reshape(b, t, d)
```

```python
import functools
import math

import numpy as np
import jax
import jax.numpy as jnp
from jax import lax
from jax.experimental import pallas as pl
from jax.experimental.pallas import tpu as pltpu

F32 = jnp.float32
BF16 = jnp.bfloat16

EPS = 1e-6
N_MOD = 6

HEAD_DIM = 64
GROUP = 8
WINDOW = 128
LANES = 128
LOG2E = math.log2(math.e)

HG_DK = 128
CHUNK = 64

VMEM_LIMIT = 56 * 1024 * 1024


def _cparams(*sem):
    return pltpu.CompilerParams(dimension_semantics=sem, vmem_limit_bytes=VMEM_LIMIT)


def _resident(shape):
    nd = len(shape)
    return pl.BlockSpec(shape, lambda *_: (0,) * nd, pipeline_mode=pl.Buffered(1))


def _cast_job(w, layer, n_steps, step_of):
    _, r, c = w.shape
    rb = r // n_steps
    assert rb * n_steps == r and rb % 16 == 0
    in_spec = pl.BlockSpec((None, rb, c), lambda *g: (layer, step_of(*g), 0))
    out_spec = pl.BlockSpec((rb, c), lambda *g: (step_of(*g), 0))
    return in_spec, out_spec, jax.ShapeDtypeStruct((r, c), BF16)


def _sigmoid(v):
    return 0.5 * jnp.tanh(0.5 * v) + 0.5


def _silu(v):
    return v * _sigmoid(v)


def _norm_mod(x, gain, shift, scale):
    ms = jnp.mean(x * x, axis=-1, keepdims=True)
    y = x * lax.rsqrt(ms + EPS) * gain
    return y * (1.0 + scale) + shift


def _mod_block(c_ref, w_ref, b_ref):
    cond = _silu(c_ref[...])
    return jnp.sum(cond * w_ref[...], axis=0, keepdims=True) + b_ref[...]


def _mod_specs(mod_w, layer, tn, col_of):
    _, d, n = mod_w.shape
    in_specs = [
        pl.BlockSpec((d, 1), lambda *g: (0, 0)),
        pl.BlockSpec((None, d, tn), lambda *g: (layer, 0, col_of(*g))),
        pl.BlockSpec((None, 1, tn), lambda *g: (layer, 0, col_of(*g))),
    ]
    out_spec = pl.BlockSpec((1, tn), lambda *g: (0, col_of(*g)))
    return in_specs, out_spec, jax.ShapeDtypeStruct((1, n), F32)


def _mod_kernel(c_ref, w_ref, b_ref, side_ref, o_ref, sideb_ref):
    sideb_ref[...] = side_ref[...].astype(BF16)
    o_ref[...] = _mod_block(c_ref, w_ref, b_ref)


def _mod_call(c, mod_w, mod_b, layer, side_w, side_layer, tn=768):
    depth, d, n = mod_w.shape
    nj = n // tn
    m_in, m_out, m_shape = _mod_specs(mod_w, layer, tn, lambda j: j)
    s_in, s_out, s_shape = _cast_job(side_w, side_layer, nj, lambda j: j)
    out, side_b = pl.pallas_call(
        _mod_kernel,
        grid=(nj,),
        in_specs=m_in + [s_in],
        out_specs=[m_out, s_out],
        out_shape=[m_shape, s_shape],
        compiler_params=_cparams("arbitrary"),
        name="mod_proj",
    )(c.reshape(d, 1), mod_w, mod_b.reshape(depth, 1, n), side_w)
    return out.reshape(N_MOD, d), side_b


PROJ_COLS = 256


def _qkv_finish(proj, qkg_ref, p_ref, pt_ref, q_dim, kv_dim, scale):
    qk = proj[:, :q_dim + kv_dim]
    ss = jnp.dot((qk * qk).astype(BF16), p_ref[...], preferred_element_type=F32)
    inv = lax.rsqrt(ss * (1.0 / HEAD_DIM) + EPS)
    inv_hi = inv.astype(BF16)
    inv_lo = (inv - inv_hi.astype(F32)).astype(BF16)
    inv_b = jnp.dot(jnp.concatenate([inv_hi, inv_lo], axis=1), pt_ref[...],
                    preferred_element_type=F32)
    qkn = qk * inv_b * qkg_ref[...]
    return ((qkn[:, :q_dim] * scale).astype(BF16), qkn[:, q_dim:].astype(BF16),
            proj[:, q_dim + kv_dim:].astype(BF16))


def _attn_layer_kernel(sink_ref, x_ref, mod_ref, gain_ref, wb_ref, qkg_ref, p_ref, pt_ref,
                       wo_ref, w1_ref, w2_ref, hw_ref,
                       o_ref, wob_ref, w1b_ref, w2b_ref, hwb_ref,
                       q_s, k_s, v_s, kt_s, vt_s, bias_ref, *, n_kv, n_blk, scale, hg_tiles, hg_cw):
    s = pl.program_id(0)
    wob_ref[...] = wo_ref[...].astype(BF16)
    w1b_ref[...] = w1_ref[...].astype(BF16)
    w2b_ref[...] = w2_ref[...].astype(BF16)
    for j in range(hg_tiles):
        for sg in range(4):
            dst = (j * 4 + sg) * hg_cw
            src = (sg * hg_tiles + j) * hg_cw
            hwb_ref[:, dst:dst + hg_cw] = hw_ref[:, src:src + hg_cw].astype(BF16)
    n_heads = n_kv * GROUP
    q_dim = n_heads * HEAD_DIM
    kv_dim = n_kv * HEAD_DIM
    row = lax.broadcasted_iota(jnp.int32, (WINDOW, WINDOW), 0)
    col = lax.broadcasted_iota(jnp.int32, (WINDOW, WINDOW), 1)
    in_cur = col <= row

    @pl.when(s == 0)
    def _():
        dist = jnp.where(in_cur, row - col, row - col + WINDOW).astype(F32)
        for hq in range(n_heads):
            slope = LOG2E * 2.0 ** (-8.0 * (hq + 1) / n_heads)
            bias_ref[hq] = -slope * dist
            bias_ref[n_heads + hq] = jnp.where(in_cur, -slope * dist, -jnp.inf)
        q_s[...] = jnp.zeros_like(q_s)
        k_s[...] = jnp.zeros_like(k_s)
        v_s[...] = jnp.zeros_like(v_s)
        kt_s[...] = jnp.zeros_like(kt_s)
        vt_s[...] = jnp.zeros_like(vt_s)

    wslot = s % 2
    rslot = 1 - wslot

    lane = lax.broadcasted_iota(jnp.int32, (WINDOW, LANES), 1)
    low = lane < HEAD_DIM
    one_lo = jnp.where(low, 1.0, 0.0).astype(BF16)
    one_hi = jnp.where(low, 0.0, 1.0).astype(BF16)
    den_rhs = jnp.concatenate([one_lo, one_lo, one_hi, one_hi], axis=0)
    zero = jnp.zeros((WINDOW, LANES), BF16)

    def halves(t, g):
        r = pltpu.roll(t, HEAD_DIM, axis=1)
        if g % 2 == 0:
            return jnp.where(low, t, zero), jnp.where(low, zero, r)
        return jnp.where(low, r, zero), jnp.where(low, zero, t)

    units = [(b_, g_) for b_ in range(n_blk) for g_ in range(n_kv)]
    n_pairs = GROUP // 2

    h = _norm_mod(x_ref[...], gain_ref[...], mod_ref[0:1, :], mod_ref[1:2, :]).astype(BF16)
    n_cols = (q_dim + 2 * kv_dim) // PROJ_COLS
    proj_chunks = []

    def project_next():
        if len(proj_chunks) < n_cols:
            cs = slice(len(proj_chunks) * PROJ_COLS, (len(proj_chunks) + 1) * PROJ_COLS)
            proj_chunks.append(jnp.dot(h, wb_ref[:, cs], preferred_element_type=F32))

    def logits_stage(blk, g):
        qrows = slice(blk * WINDOW, (blk + 1) * WINDOW)
        sl = slice((g // 2) * LANES, (g // 2 + 1) * LANES)
        if blk == 0:
            k_prev, v_prev = kt_s[:, sl], vt_s[:, sl]
            bias_off = jnp.where(s == 1, n_heads, 0)
        else:
            prows = slice((blk - 1) * WINDOW, blk * WINDOW)
            k_prev, v_prev = k_s[rslot, prows, sl], v_s[rslot, prows, sl]
            bias_off = 0
        kc_lo, kc_hi = halves(k_s[rslot, qrows, sl], g)
        kp_lo, kp_hi = halves(k_prev, g)
        vc_lo, vc_hi = halves(v_s[rslot, qrows, sl], g)
        vp_lo, vp_hi = halves(v_prev, g)
        k_rhs = jnp.concatenate([kc_lo, kc_hi, kp_lo, kp_hi], axis=0)
        v_rhs = jnp.concatenate([vc_lo, vp_lo, vc_hi, vp_hi], axis=0)
        q_base = g * GROUP * HEAD_DIM
        qs = jnp.concatenate(
            [q_s[rslot, qrows, q_base + t * LANES: q_base + (t + 1) * LANES] for t in range(n_pairs)], axis=0)
        logits = lax.dot_general(qs, k_rhs, (((1,), (1,)), ((), ())), preferred_element_type=F32)
        return logits, v_rhs, bias_off

    def softmax_pv_stage(blk, g, logits, v_rhs, bias_off):
        qrows = slice(blk * WINDOW, (blk + 1) * WINDOW)
        q_base = g * GROUP * HEAD_DIM
        pms, sink_terms = [], []
        for t in range(n_pairs):
            rows = slice(t * WINDOW, (t + 1) * WINDOW)
            probs, mx_e = [], []
            for e in range(2):
                hq = g * GROUP + 2 * t + e
                l_cur = logits[rows, e * WINDOW:(e + 1) * WINDOW]
                l_prev = logits[rows, (2 + e) * WINDOW:(3 + e) * WINDOW]
                sc = jnp.where(in_cur, l_cur, l_prev) + bias_ref[hq + bias_off]
                mx = jnp.max(sc, axis=1, keepdims=True)
                pe = jnp.exp2(sc - mx).astype(BF16)
                probs.append(jnp.where(in_cur, pe, zero))
                probs.append(jnp.where(in_cur, zero, pe))
                mx_e.append(mx)
            pms.append(jnp.concatenate(probs, axis=1))
            hq0 = g * GROUP + 2 * t
            sink_l = jnp.where(low[0:1], sink_ref[0, hq0] * LOG2E, sink_ref[0, hq0 + 1] * LOG2E)
            sink_terms.append(jnp.exp2(sink_l - jnp.where(low, mx_e[0], mx_e[1])))
        res = jnp.dot(jnp.concatenate(pms, axis=0), jnp.concatenate([v_rhs, den_rhs], axis=1),
                      preferred_element_type=F32)
        for t in range(n_pairs):
            rows = slice(t * WINDOW, (t + 1) * WINDOW)
            den = res[rows, LANES:] + sink_terms[t]
            o_ref[qrows, q_base + t * LANES: q_base + (t + 1) * LANES] = (res[rows, :LANES] / den).astype(BF16)

    staged = logits_stage(*units[0])
    project_next()
    project_next()
    for u, (blk, g) in enumerate(units):
        cur = staged
        project_next()
        if u + 1 < len(units):
            staged = logits_stage(*units[u + 1])
        softmax_pv_stage(blk, g, *cur)
    while len(proj_chunks) < n_cols:
        project_next()

    last = slice((n_blk - 1) * WINDOW, n_blk * WINDOW)
    kt_s[...] = k_s[rslot, last, :]
    vt_s[...] = v_s[rslot, last, :]

    qn, kn, vn = _qkv_finish(jnp.concatenate(proj_chunks, axis=1), qkg_ref, p_ref, pt_ref, q_dim, kv_dim, scale)
    q_s[wslot] = qn
    k_s[wslot] = kn
    v_s[wslot] = vn


def _attn_layer_call(x, mod_l, gain, w_in_b, qk_gain, sinks, w_out_all, mlp_w1, mlp_w2, layer,
                     hgrn_w_in, hg_layer, hg_tiles, n_blk=2):
    t, d = x.shape
    n = w_in_b.shape[1]
    kv_dim = (n - d) // 2
    q_dim = d
    n_kv = kv_dim // HEAD_DIM
    n_heads = n_kv * GROUP
    assert n_kv % 2 == 0 and q_dim == n_heads * HEAD_DIM and (q_dim + kv_dim) // HEAD_DIM <= LANES
    rows = n_blk * WINDOW
    nt = t // rows
    ind = (np.arange(q_dim + kv_dim)[:, None] // HEAD_DIM == np.arange(LANES)[None, :])
    p = jnp.asarray(ind, dtype=BF16)
    pt = jnp.asarray(np.concatenate([ind.T, ind.T], axis=0), dtype=BF16)
    proj_tile = lambda s: (jnp.minimum(s, nt - 1), 0)
    attn_tile = lambda s: (jnp.maximum(s - 1, 0), 0)
    cast_step = lambda s: jnp.minimum(s, nt - 1)
    wo_in, wo_out, wo_shape = _cast_job(w_out_all, layer, nt, cast_step)
    w1_in, w1_out, w1_shape = _cast_job(mlp_w1, layer, nt, cast_step)
    w2_in, w2_out, w2_shape = _cast_job(mlp_w2, layer, nt, cast_step)
    hw_in, hw_out, hw_shape = _cast_job(hgrn_w_in, hg_layer, nt, cast_step)
    hg_cw = hgrn_w_in.shape[2] // (4 * hg_tiles)
    return pl.pallas_call(
        functools.partial(_attn_layer_kernel, n_kv=n_kv, n_blk=n_blk, scale=LOG2E / math.sqrt(HEAD_DIM),
                          hg_tiles=hg_tiles, hg_cw=hg_cw),
        grid=(nt + 1,),
        in_specs=[
            pl.BlockSpec(memory_space=pltpu.SMEM),
            pl.BlockSpec((rows, d), proj_tile),
            _resident((N_MOD, d)),
            _resident((1, d)),
            _resident((d, n)),
            _resident((1, q_dim + kv_dim)),
            _resident((q_dim + kv_dim, LANES)),
            _resident((2 * LANES, q_dim + kv_dim)),
            wo_in, w1_in, w2_in, hw_in,
        ],
        out_specs=[pl.BlockSpec((rows, q_dim), attn_tile), wo_out, w1_out, w2_out, hw_out],
        out_shape=[jax.ShapeDtypeStruct((t, q_dim), BF16), wo_shape, w1_shape, w2_shape, hw_shape],
        scratch_shapes=[
            pltpu.VMEM((2, rows, q_dim), BF16),
            pltpu.VMEM((2, rows, kv_dim), BF16),
            pltpu.VMEM((2, rows, kv_dim), BF16),
            pltpu.VMEM((WINDOW, kv_dim), BF16),
            pltpu.VMEM((WINDOW, kv_dim), BF16),
            pltpu.VMEM((2 * n_heads, WINDOW, WINDOW), F32),
        ],
        compiler_params=_cparams("arbitrary"),
        name="swa_attention_layer",
    )(sinks.reshape(1, n_heads), x, mod_l, gain, w_in_b, qk_gain, p, pt, w_out_all, mlp_w1, mlp_w2, hgrn_w_in)


def _mlp_kernel(*refs, has_mod, n_sub):
    if has_mod:
        (x_ref, a_ref, mod_ref, gain_ref, wo_ref, w1_ref, w2_ref, c_ref, mw_ref, mb_ref,
         o_ref, mo_ref, h_ref) = refs
        step = pl.program_id(0) * pl.num_programs(1) + pl.program_id(1)
        rb = mw_ref.shape[0]

        @pl.when(step == 0)
        def _():
            mo_ref[...] = mb_ref[...]

        def side_work():
            cond = _silu(c_ref[pl.ds(pl.multiple_of(step * rb, rb), rb), :])
            mo_ref[...] += jnp.sum(cond * mw_ref[...], axis=0, keepdims=True)
    else:
        x_ref, a_ref, mod_ref, gain_ref, wo_ref, w1_ref, w2_ref, o_ref, h_ref = refs

        def side_work():
            pass
    f = pl.program_id(1)
    gate = mod_ref[5:6, :]

    def ffn(h):
        a = jnp.maximum(jnp.dot(h, w1_ref[...], preferred_element_type=F32), 0.0)
        return gate * jnp.dot((a * a).astype(BF16), w2_ref[...], preferred_element_type=F32)

    @pl.when(f == 0)
    def _():
        side_work()
        sub = x_ref.shape[0] // n_sub
        for s in range(n_sub):
            rs = slice(s * sub, (s + 1) * sub)
            x1 = x_ref[rs, :] + mod_ref[2:3, :] * jnp.dot(a_ref[rs, :], wo_ref[...], preferred_element_type=F32)
            h = _norm_mod(x1, gain_ref[...], mod_ref[3:4, :], mod_ref[4:5, :]).astype(BF16)
            h_ref[rs, :] = h
            o_ref[rs, :] = x1 + ffn(h)

    @pl.when(f > 0)
    def _():
        side_work()
        o_ref[...] += ffn(h_ref[...])


def _mlp_call(x, a, mod_l, gain, w_out, w1, w2, mod_job=None, tm=512, tf=1024, n_sub=2):
    t, d = x.shape
    kdim = a.shape[1]
    dff = w1.shape[1]
    nf = dff // tf
    in_specs = [
        pl.BlockSpec((tm, d), lambda i, f: (i, 0)),
        pl.BlockSpec((tm, kdim), lambda i, f: (i, 0)),
        _resident((N_MOD, d)),
        _resident((1, d)),
        _resident((kdim, d)),
        pl.BlockSpec((d, tf), lambda i, f: (0, f)),
        pl.BlockSpec((tf, d), lambda i, f: (f, 0)),
    ]
    out_specs = [pl.BlockSpec((tm, d), lambda i, f: (i, 0))]
    out_shape = [jax.ShapeDtypeStruct((t, d), F32)]
    args = [x, a, mod_l, gain, w_out, w1, w2]
    if mod_job is not None:
        c, mod_w, mod_b, layer = mod_job
        depth, _, n = mod_w.shape
        steps = (t // tm) * nf
        rb = d // steps
        assert rb * steps == d and rb % 8 == 0
        in_specs += [
            _resident((d, 1)),
            pl.BlockSpec((None, rb, n), lambda i, f: (layer, i * nf + f, 0)),
            pl.BlockSpec((None, 1, n), lambda i, f: (layer, 0, 0), pipeline_mode=pl.Buffered(1)),
        ]
        out_specs.append(pl.BlockSpec((1, n), lambda i, f: (0, 0)))
        out_shape.append(jax.ShapeDtypeStruct((1, n), F32))
        args += [c.reshape(d, 1), mod_w, mod_b.reshape(depth, 1, n)]
    outs = pl.pallas_call(
        functools.partial(_mlp_kernel, has_mod=mod_job is not None, n_sub=n_sub),
        grid=(t // tm, nf),
        in_specs=in_specs,
        out_specs=out_specs,
        out_shape=out_shape,
        scratch_shapes=[pltpu.VMEM((tm, d), BF16)],
        compiler_params=_cparams("arbitrary", "arbitrary"),
        name="outproj_relu2_mlp",
    )(*args)
    if mod_job is None:
        return outs[0]
    return outs[0], outs[1].reshape(N_MOD, d)


def _layer_lower_bound(lbl_ref):
    l0 = lbl_ref[0]
    l1 = lbl_ref[1]
    lm = jnp.maximum(l0, l1)
    e0 = jnp.exp(l0 - lm)
    e1 = jnp.exp(l1 - lm)
    p0 = e0 / (e0 + e1)
    p1 = e1 / (e0 + e1)
    return (p0 + p1) - p0


def _hgrn_inproj_kernel(x_ref, mod_ref, gain_ref, lbl_ref, w_ref, side_ref, side2_ref,
                        q_ref, lf_ref, kk_ref, v_ref, g_ref, sideb_ref, side2b_ref, h_ref, *, hpt, scale, n_sub):
    j = pl.program_id(1)
    lb = _layer_lower_bound(lbl_ref)
    width = hpt * HG_DK

    def project(h, rs, with_side):
        if with_side:
            sideb_ref[...] = side_ref[...].astype(BF16)
            side2b_ref[...] = side2_ref[...].astype(BF16)
        res = jnp.dot(h, w_ref[...], preferred_element_type=F32)
        for hh in range(hpt):
            seg = lambda s: res[:, s * width + hh * HG_DK: s * width + (hh + 1) * HG_DK]
            q_ref[hh, rs, :] = (_silu(seg(0)) * scale).astype(BF16)
            forget = lb[hh] + (1.0 - lb[hh]) * _sigmoid(seg(1))
            kk_ref[hh, rs, :] = 1.0 - forget
            lf_ref[hh, rs, :] = jnp.log(forget)
            v_ref[hh, rs, :] = seg(2).astype(BF16)
            g_ref[hh, rs, :] = _silu(seg(3)).astype(BF16)

    @pl.when(j == 0)
    def _():
        sub = x_ref.shape[0] // n_sub
        for s in range(n_sub):
            rs = slice(s * sub, (s + 1) * sub)
            h = _norm_mod(x_ref[rs, :], gain_ref[...], mod_ref[0:1, :], mod_ref[1:2, :]).astype(BF16)
            h_ref[rs, :] = h
            project(h, rs, s == 0)

    @pl.when(j > 0)
    def _():
        project(h_ref[...], slice(None), True)


def _hgrn_inproj_call(x, mod_l, gain, lb_logits, w_perm, n_heads, side_w, side_layer, side2_w, side2_layer,
                      tm=1024, hpt=2):
    t, d = x.shape
    depth = lb_logits.shape[0]
    assert depth == 2
    tn = 4 * hpt * HG_DK
    n_inner = n_heads // hpt
    head_major = lambda dt: jax.ShapeDtypeStruct((n_heads, t, HG_DK), dt)
    ospec = pl.BlockSpec((hpt, tm, HG_DK), lambda i, j: (j, i, 0))
    s_in, s_out, s_shape = _cast_job(side_w, side_layer, (t // tm) * n_inner, lambda i, j: i * n_inner + j)
    s2_in, s2_out, s2_shape = _cast_job(side2_w, side2_layer, (t // tm) * n_inner,
                                        lambda i, j: i * n_inner + j)
    return pl.pallas_call(
        functools.partial(_hgrn_inproj_kernel, hpt=hpt, scale=1.0 / math.sqrt(HG_DK), n_sub=4),
        grid=(t // tm, n_inner),
        in_specs=[
            pl.BlockSpec((tm, d), lambda i, j: (i, 0)),
            _resident((N_MOD, d)),
            _resident((1, d)),
            pl.BlockSpec((depth, hpt, 1, HG_DK), lambda i, j: (0, j, 0, 0)),
            pl.BlockSpec((d, tn), lambda i, j: (0, j)),
            s_in, s2_in,
        ],
        out_specs=[ospec] * 5 + [s_out, s2_out],
        out_shape=[head_major(BF16), head_major(F32), head_major(F32), head_major(BF16), head_major(BF16),
                   s_shape, s2_shape],
        scratch_shapes=[pltpu.VMEM((tm, d), BF16)],
        compiler_params=_cparams("arbitrary", "arbitrary"),
        name="hgrn_inproj",
    )(x, mod_l, gain, lb_logits.reshape(depth, n_heads, 1, HG_DK), w_perm, side_w, side2_w)


def _cum_matrix(tr):
    n_chunks = tr // CHUNK
    r = np.arange(tr)[:, None]
    c = np.arange(tr)[None, :]
    same = (r // CHUNK) == (c // CHUNK)
    incl = same & (c <= r)
    pivot = same & (c % CHUNK <= CHUNK // 2 - 1)
    rows = np.zeros((2 * n_chunks, tr), np.float32)
    for ch in range(n_chunks):
        rows[2 * ch, ch * CHUNK: ch * CHUNK + CHUNK // 2] = 1.0
        rows[2 * ch + 1, ch * CHUNK: (ch + 1) * CHUNK] = 1.0
    mats = np.concatenate([incl.astype(np.float32) - pivot.astype(np.float32), rows], axis=0)
    return jnp.asarray(mats, dtype=BF16)


def _hgrn_kernel(gain_ref, cm_ref, q_ref, lf_ref, kk_ref, v_ref, g_ref, side_ref,
                 o_ref, sideb_ref, st_ref, *, hb, tr):
    tstep = pl.program_id(1)
    sideb_ref[...] = side_ref[...].astype(BF16)

    @pl.when(tstep == 0)
    def _():
        st_ref[...] = jnp.zeros_like(st_ref)

    rowc = lax.broadcasted_iota(jnp.int32, (CHUNK, 2 * CHUNK), 0)
    colc = lax.broadcasted_iota(jnp.int32, (CHUNK, 2 * CHUNK), 1)
    causal2 = (colc % CHUNK) <= rowc
    n_chunks = tr // CHUNK
    zc = jnp.zeros((CHUNK, HG_DK), BF16)
    zs = jnp.zeros((HG_DK, HG_DK), BF16)

    def block_diag(a, b, z):
        return jnp.concatenate([jnp.concatenate([a, z], axis=1), jnp.concatenate([z, b], axis=1)], axis=0)

    n_pairs = hb // 2
    pairs = [(2 * pp, 2 * pp + 1) for pp in range(n_pairs)]
    chunk_rows = [slice(c * CHUNK, (c + 1) * CHUNK) for c in range(n_chunks)]

    cums = []
    for hh in range(hb):
        logf = lf_ref[hh]
        p_hi = logf.astype(BF16)
        p_lo = (logf - p_hi.astype(F32)).astype(BF16)
        cm = jnp.dot(cm_ref[...], jnp.concatenate([p_hi, p_lo], axis=1), preferred_element_type=F32)
        cums.append(cm[:, :HG_DK] + cm[:, HG_DK:])

    qe, ke, qb, ku, dec = [], [], [], [], []
    for hh in range(hb):
        bmp = cums[hh][:tr]
        qe_f = q_ref[hh].astype(F32) * jnp.exp(bmp)
        ke_f = kk_ref[hh] * jnp.exp(-bmp)
        qb_h, ku_h, dec_h = [], [], []
        for c, rs in enumerate(chunk_rows):
            piv = cums[hh][tr + 2 * c: tr + 2 * c + 1]
            blast = cums[hh][tr + 2 * c + 1: tr + 2 * c + 2]
            qb_h.append((qe_f[rs] * jnp.exp(piv)).astype(BF16))
            ku_h.append((ke_f[rs] * jnp.exp(blast - piv)).astype(BF16))
            dec_h.append(jnp.exp(blast))
        qe.append(qe_f.astype(BF16))
        ke.append(ke_f.astype(BF16))
        qb.append(qb_h)
        ku.append(ku_h)
        dec.append(dec_h)

    amat = [[None] * n_chunks for _ in pairs]
    upd = [[None] * n_chunks for _ in pairs]
    for c, rs in enumerate(chunk_rows):
        for pp, (h0, h1) in enumerate(pairs):
            v0 = v_ref[h0, rs, :]
            v1 = v_ref[h1, rs, :]
            a = lax.dot_general(jnp.concatenate([qe[h0][rs], qe[h1][rs]], axis=0),
                                jnp.concatenate([ke[h0][rs], ke[h1][rs]], axis=0),
                                (((1,), (1,)), ((), ())), preferred_element_type=F32)
            a = jnp.where(colc < CHUNK, a[:CHUNK], a[CHUNK:])
            amat[pp][c] = jnp.where(causal2, a, 0.0).astype(BF16)
            upd[pp][c] = lax.dot_general(jnp.concatenate([v0, v1], axis=0),
                                         block_diag(ku[h0][c], ku[h1][c], zc),
                                         (((0,), (0,)), ((), ())), preferred_element_type=F32)

    for c, rs in enumerate(chunk_rows):
        for pp, (h0, h1) in enumerate(pairs):
            st = st_ref[pp]
            st_b = st.astype(BF16)
            o_intra = jnp.dot(amat[pp][c], block_diag(v_ref[h0, rs, :], v_ref[h1, rs, :], zc),
                              preferred_element_type=F32)
            o_inter = lax.dot_general(jnp.concatenate([qb[h0][c], qb[h1][c]], axis=0),
                                      jnp.concatenate([st_b[:, :HG_DK], st_b[:, HG_DK:]], axis=0),
                                      (((1,), (1,)), ((), ())), preferred_element_type=F32)
            st_ref[pp] = st * jnp.concatenate([dec[h0][c], dec[h1][c]], axis=1) + upd[pp][c]
            for e, hh in enumerate((h0, h1)):
                oh = (o_intra[:, e * HG_DK:(e + 1) * HG_DK]
                      + o_inter[e * CHUNK:(e + 1) * CHUNK, e * HG_DK:(e + 1) * HG_DK])
                on = oh * lax.rsqrt(jnp.mean(oh * oh, axis=-1, keepdims=True) + EPS) * gain_ref[hh]
                o_ref[rs, hh * HG_DK:(hh + 1) * HG_DK] = (on * g_ref[hh, rs, :].astype(F32)).astype(BF16)


def _hgrn_call(q, lf, kk, v, g, o_gain, side_w, side_layer, hb=16, tr=256):
    n_heads, t, dk = q.shape
    assert dk == HG_DK and hb % 2 == 0
    cm = _cum_matrix(tr)
    n_inner = t // tr
    blk = pl.BlockSpec((hb, tr, dk), lambda h, s: (h, s, 0))
    s_in, s_out, s_shape = _cast_job(side_w, side_layer, (n_heads // hb) * n_inner,
                                     lambda h, s: h * n_inner + s)
    return pl.pallas_call(
        functools.partial(_hgrn_kernel, hb=hb, tr=tr),
        grid=(n_heads // hb, n_inner),
        in_specs=[
            pl.BlockSpec((hb, 1, dk), lambda h, s: (h, 0, 0)),
            _resident(cm.shape),
            blk, blk, blk, blk, blk,
            s_in,
        ],
        out_specs=[pl.BlockSpec((tr, hb * dk), lambda h, s: (s, h)), s_out],
        out_shape=[jax.ShapeDtypeStruct((t, n_heads * dk), BF16), s_shape],
        scratch_shapes=[pltpu.VMEM((hb // 2, dk, 2 * dk), F32)],
        compiler_params=_cparams("arbitrary", "arbitrary"),
        name="hgrn_recurrence",
    )(o_gain.reshape(n_heads, 1, dk), cm, q, lf, kk, v, g, side_w)


def kernel(x, c, mod_w, mod_b, norm_mix, norm_mlp, attn_w_in, attn_w_out, attn_q_gain, attn_k_gain,
           attn_sinks, hgrn_w_in, hgrn_w_out, hgrn_o_gain, hgrn_lb_logits, mlp_w1, mlp_w2):
    b, t, d = x.shape
    assert b == 1 and mod_w.shape[0] == 2
    xs = x.reshape(t, d)

    mod0, w_in_b = _mod_call(c, mod_w, mod_b, 0, attn_w_in, 0)

    q_dim = attn_w_out.shape[1]
    kv_dim = (attn_w_in.shape[2] - q_dim) // 2
    qk_gain = jnp.concatenate([jnp.tile(attn_q_gain[0], q_dim // HEAD_DIM),
                               jnp.tile(attn_k_gain[0], kv_dim // HEAD_DIM)]).reshape(1, q_dim + kv_dim)
    n_heads = hgrn_o_gain.shape[1]
    hpt = 2
    att, wob, w1b, w2b, hw_perm = _attn_layer_call(xs, mod0, norm_mix[0].reshape(1, d), w_in_b, qk_gain,
                                                   attn_sinks[0], attn_w_out, mlp_w1, mlp_w2, 0,
                                                   hgrn_w_in, 0, n_heads // hpt)
    xs, mod1 = _mlp_call(xs, att, mod0, norm_mlp[0].reshape(1, d), wob, w1b, w2b,
                         mod_job=(c, mod_w, mod_b, 1))

    hq, hlf, hkk, hv, hg, w1b, wob = _hgrn_inproj_call(xs, mod1, norm_mix[1].reshape(1, d), hgrn_lb_logits,
                                                       hw_perm, n_heads, mlp_w1, 1, hgrn_w_out, 0, hpt=hpt)
    ho, w2b = _hgrn_call(hq, hlf, hkk, hv, hg, hgrn_o_gain[0], mlp_w2, 1)
    xs = _mlp_call(xs, ho, mod1, norm_mlp[1].reshape(1, d), wob, w1b, w2b)
    return xs.reshape(b, t, d)
```

```python
import functools
import math

import numpy as np
import jax
import jax.numpy as jnp
from jax import lax
from jax.experimental import pallas as pl
from jax.experimental.pallas import tpu as pltpu

F32 = jnp.float32
BF16 = jnp.bfloat16

EPS = 1e-6
N_MOD = 6

HEAD_DIM = 64
GROUP = 8
WINDOW = 128
LANES = 128
LOG2E = math.log2(math.e)

HG_DK = 128
CHUNK = 64

VMEM_LIMIT = 56 * 1024 * 1024


def _cparams(*sem):
    return pltpu.CompilerParams(dimension_semantics=sem, vmem_limit_bytes=VMEM_LIMIT)


def _resident(shape):
    nd = len(shape)
    return pl.BlockSpec(shape, lambda *_: (0,) * nd, pipeline_mode=pl.Buffered(1))


def _cast_job(w, layer, n_steps, step_of):
    _, r, c = w.shape
    rb = r // n_steps
    assert rb * n_steps == r and rb % 16 == 0
    in_spec = pl.BlockSpec((None, rb, c), lambda *g: (layer, step_of(*g), 0))
    out_spec = pl.BlockSpec((rb, c), lambda *g: (step_of(*g), 0))
    return in_spec, out_spec, jax.ShapeDtypeStruct((r, c), BF16)


def _sigmoid(v):
    return 0.5 * jnp.tanh(0.5 * v) + 0.5


def _silu(v):
    return v * _sigmoid(v)


def _norm_mod(x, gain, shift, scale):
    ms = jnp.mean(x * x, axis=-1, keepdims=True)
    y = x * lax.rsqrt(ms + EPS) * gain
    return y * (1.0 + scale) + shift


def _mod_block(c_ref, w_ref, b_ref):
    cond = _silu(c_ref[...])
    return jnp.sum(cond * w_ref[...], axis=0, keepdims=True) + b_ref[...]


def _mod_specs(mod_w, layer, tn, col_of):
    _, d, n = mod_w.shape
    in_specs = [
        pl.BlockSpec((d, 1), lambda *g: (0, 0)),
        pl.BlockSpec((None, d, tn), lambda *g: (layer, 0, col_of(*g))),
        pl.BlockSpec((None, 1, tn), lambda *g: (layer, 0, col_of(*g))),
    ]
    out_spec = pl.BlockSpec((1, tn), lambda *g: (0, col_of(*g)))
    return in_specs, out_spec, jax.ShapeDtypeStruct((1, n), F32)


def _mod_kernel(c_ref, w_ref, b_ref, side_ref, o_ref, sideb_ref):
    sideb_ref[...] = side_ref[...].astype(BF16)
    o_ref[...] = _mod_block(c_ref, w_ref, b_ref)


def _mod_call(c, mod_w, mod_b, layer, side_w, side_layer, tn=768):
    depth, d, n = mod_w.shape
    nj = n // tn
    m_in, m_out, m_shape = _mod_specs(mod_w, layer, tn, lambda j: j)
    s_in, s_out, s_shape = _cast_job(side_w, side_layer, nj, lambda j: j)
    out, side_b = pl.pallas_call(
        _mod_kernel,
        grid=(nj,),
        in_specs=m_in + [s_in],
        out_specs=[m_out, s_out],
        out_shape=[m_shape, s_shape],
        compiler_params=_cparams("arbitrary"),
        name="mod_proj",
    )(c.reshape(d, 1), mod_w, mod_b.reshape(depth, 1, n), side_w)
    return out.reshape(N_MOD, d), side_b


PROJ_COLS = 256


def _qkv_finish(proj, qkg_ref, p_ref, pt_ref, q_dim, kv_dim, scale):
    qk = proj[:, :q_dim + kv_dim]
    ss = jnp.dot((qk * qk).astype(BF16), p_ref[...], preferred_element_type=F32)
    inv = lax.rsqrt(ss * (1.0 / HEAD_DIM) + EPS)
    inv_hi = inv.astype(BF16)
    inv_lo = (inv - inv_hi.astype(F32)).astype(BF16)
    inv_b = jnp.dot(jnp.concatenate([inv_hi, inv_lo], axis=1), pt_ref[...],
                    preferred_element_type=F32)
    qkn = qk * inv_b * qkg_ref[...]
    return ((qkn[:, :q_dim] * scale).astype(BF16), qkn[:, q_dim:].astype(BF16),
            proj[:, q_dim + kv_dim:].astype(BF16))


def _attn_layer_kernel(sink_ref, x_ref, mod_ref, gain_ref, wb_ref, qkg_ref, p_ref, pt_ref,
                       wo_ref, w1_ref, w2_ref, hw_ref,
                       o_ref, wob_ref, w1b_ref, w2b_ref, hwb_ref,
                       q_s, k_s, v_s, kt_s, vt_s, bias_ref, *, n_kv, n_blk, scale, hg_tiles, hg_cw):
    s = pl.program_id(0)
    wob_ref[...] = wo_ref[...].astype(BF16)
    w1b_ref[...] = w1_ref[...].astype(BF16)
    w2b_ref[...] = w2_ref[...].astype(BF16)
    for j in range(hg_tiles):
        for sg in range(4):
            dst = (j * 4 + sg) * hg_cw
            src = (sg * hg_tiles + j) * hg_cw
            hwb_ref[:, dst:dst + hg_cw] = hw_ref[:, src:src + hg_cw].astype(BF16)
    n_heads = n_kv * GROUP
    q_dim = n_heads * HEAD_DIM
    kv_dim = n_kv * HEAD_DIM
    row = lax.broadcasted_iota(jnp.int32, (WINDOW, WINDOW), 0)
    col = lax.broadcasted_iota(jnp.int32, (WINDOW, WINDOW), 1)
    in_cur = col <= row

    @pl.when(s == 0)
    def _():
        dist = jnp.where(in_cur, row - col, row - col + WINDOW).astype(F32)
        for hq in range(n_heads):
            slope = LOG2E * 2.0 ** (-8.0 * (hq + 1) / n_heads)
            bias_ref[hq] = -slope * dist
            bias_ref[n_heads + hq] = jnp.where(in_cur, -slope * dist, -jnp.inf)
        q_s[...] = jnp.zeros_like(q_s)
        k_s[...] = jnp.zeros_like(k_s)
        v_s[...] = jnp.zeros_like(v_s)
        kt_s[...] = jnp.zeros_like(kt_s)
        vt_s[...] = jnp.zeros_like(vt_s)

    wslot = s % 2
    rslot = 1 - wslot

    lane = lax.broadcasted_iota(jnp.int32, (WINDOW, LANES), 1)
    low = lane < HEAD_DIM
    one_lo = jnp.where(low, 1.0, 0.0).astype(BF16)
    one_hi = jnp.where(low, 0.0, 1.0).astype(BF16)
    den_rhs = jnp.concatenate([one_lo, one_lo, one_hi, one_hi], axis=0)
    zero = jnp.zeros((WINDOW, LANES), BF16)

    def halves(t, g):
        r = pltpu.roll(t, HEAD_DIM, axis=1)
        if g % 2 == 0:
            return jnp.where(low, t, zero), jnp.where(low, zero, r)
        return jnp.where(low, r, zero), jnp.where(low, zero, t)

    units = [(b_, g_) for b_ in range(n_blk) for g_ in range(n_kv)]
    n_pairs = GROUP // 2

    h = _norm_mod(x_ref[...], gain_ref[...], mod_ref[0:1, :], mod_ref[1:2, :]).astype(BF16)
    n_cols = (q_dim + 2 * kv_dim) // PROJ_COLS
    proj_chunks = []

    def project_next():
        if len(proj_chunks) < n_cols:
            cs = slice(len(proj_chunks) * PROJ_COLS, (len(proj_chunks) + 1) * PROJ_COLS)
            proj_chunks.append(jnp.dot(h, wb_ref[:, cs], preferred_element_type=F32))

    def logits_stage(blk, g):
        qrows = slice(blk * WINDOW, (blk + 1) * WINDOW)
        sl = slice((g // 2) * LANES, (g // 2 + 1) * LANES)
        if blk == 0:
            k_prev, v_prev = kt_s[:, sl], vt_s[:, sl]
            bias_off = jnp.where(s == 1, n_heads, 0)
        else:
            prows = slice((blk - 1) * WINDOW, blk * WINDOW)
            k_prev, v_prev = k_s[rslot, prows, sl], v_s[rslot, prows, sl]
            bias_off = 0
        kc_lo, kc_hi = halves(k_s[rslot, qrows, sl], g)
        kp_lo, kp_hi = halves(k_prev, g)
        vc_lo, vc_hi = halves(v_s[rslot, qrows, sl], g)
        vp_lo, vp_hi = halves(v_prev, g)
        k_rhs = jnp.concatenate([kc_lo, kc_hi, kp_lo, kp_hi], axis=0)
        v_rhs = jnp.concatenate([vc_lo, vp_lo, vc_hi, vp_hi], axis=0)
        q_base = g * GROUP * HEAD_DIM
        qs = jnp.concatenate(
            [q_s[rslot, qrows, q_base + t * LANES: q_base + (t + 1) * LANES] for t in range(n_pairs)], axis=0)
        logits = lax.dot_general(qs, k_rhs, (((1,), (1,)), ((), ())), preferred_element_type=F32)
        return logits, v_rhs, bias_off

    def softmax_pv_stage(blk, g, logits, v_rhs, bias_off):
        qrows = slice(blk * WINDOW, (blk + 1) * WINDOW)
        q_base = g * GROUP * HEAD_DIM
        pms, sink_terms = [], []
        for t in range(n_pairs):
            rows = slice(t * WINDOW, (t + 1) * WINDOW)
            probs, mx_e = [], []
            for e in range(2):
                hq = g * GROUP + 2 * t + e
                l_cur = logits[rows, e * WINDOW:(e + 1) * WINDOW]
                l_prev = logits[rows, (2 + e) * WINDOW:(3 + e) * WINDOW]
                sc = jnp.where(in_cur, l_cur, l_prev) + bias_ref[hq + bias_off]
                mx = jnp.max(sc, axis=1, keepdims=True)
                pe = jnp.exp2(sc - mx).astype(BF16)
                probs.append(jnp.where(in_cur, pe, zero))
                probs.append(jnp.where(in_cur, zero, pe))
                mx_e.append(mx)
            pms.append(jnp.concatenate(probs, axis=1))
            hq0 = g * GROUP + 2 * t
            sink_l = jnp.where(low[0:1], sink_ref[0, hq0] * LOG2E, sink_ref[0, hq0 + 1] * LOG2E)
            sink_terms.append(jnp.exp2(sink_l - jnp.where(low, mx_e[0], mx_e[1])))
        res = jnp.dot(jnp.concatenate(pms, axis=0), jnp.concatenate([v_rhs, den_rhs], axis=1),
                      preferred_element_type=F32)
        for t in range(n_pairs):
            rows = slice(t * WINDOW, (t + 1) * WINDOW)
            den = res[rows, LANES:] + sink_terms[t]
            o_ref[qrows, q_base + t * LANES: q_base + (t + 1) * LANES] = (res[rows, :LANES] / den).astype(BF16)

    staged = logits_stage(*units[0])
    project_next()
    project_next()
    for u, (blk, g) in enumerate(units):
        cur = staged
        project_next()
        if u + 1 < len(units):
            staged = logits_stage(*units[u + 1])
        softmax_pv_stage(blk, g, *cur)
    while len(proj_chunks) < n_cols:
        project_next()

    last = slice((n_blk - 1) * WINDOW, n_blk * WINDOW)
    kt_s[...] = k_s[rslot, last, :]
    vt_s[...] = v_s[rslot, last, :]

    qn, kn, vn = _qkv_finish(jnp.concatenate(proj_chunks, axis=1), qkg_ref, p_ref, pt_ref, q_dim, kv_dim, scale)
    q_s[wslot] = qn
    k_s[wslot] = kn
    v_s[wslot] = vn


def _attn_layer_call(x, mod_l, gain, w_in_b, qk_gain, sinks, w_out_all, mlp_w1, mlp_w2, layer,
                     hgrn_w_in, hg_layer, hg_tiles, n_blk=2):
    t, d = x.shape
    n = w_in_b.shape[1]
    kv_dim = (n - d) // 2
    q_dim = d
    n_kv = kv_dim // HEAD_DIM
    n_heads = n_kv * GROUP
    assert n_kv % 2 == 0 and q_dim == n_heads * HEAD_DIM and (q_dim + kv_dim) // HEAD_DIM <= LANES
    rows = n_blk * WINDOW
    nt = t // rows
    ind = (np.arange(q_dim + kv_dim)[:, None] // HEAD_DIM == np.arange(LANES)[None, :])
    p = jnp.asarray(ind, dtype=BF16)
    pt = jnp.asarray(np.concatenate([ind.T, ind.T], axis=0), dtype=BF16)
    proj_tile = lambda s: (jnp.minimum(s, nt - 1), 0)
    attn_tile = lambda s: (jnp.maximum(s - 1, 0), 0)
    cast_step = lambda s: jnp.minimum(s, nt - 1)
    wo_in, wo_out, wo_shape = _cast_job(w_out_all, layer, nt, cast_step)
    w1_in, w1_out, w1_shape = _cast_job(mlp_w1, layer, nt, cast_step)
    w2_in, w2_out, w2_shape = _cast_job(mlp_w2, layer, nt, cast_step)
    hw_in, hw_out, hw_shape = _cast_job(hgrn_w_in, hg_layer, nt, cast_step)
    hg_cw = hgrn_w_in.shape[2] // (4 * hg_tiles)
    return pl.pallas_call(
        functools.partial(_attn_layer_kernel, n_kv=n_kv, n_blk=n_blk, scale=LOG2E / math.sqrt(HEAD_DIM),
                          hg_tiles=hg_tiles, hg_cw=hg_cw),
        grid=(nt + 1,),
        in_specs=[
            pl.BlockSpec(memory_space=pltpu.SMEM),
            pl.BlockSpec((rows, d), proj_tile),
            _resident((N_MOD, d)),
            _resident((1, d)),
            _resident((d, n)),
            _resident((1, q_dim + kv_dim)),
            _resident((q_dim + kv_dim, LANES)),
            _resident((2 * LANES, q_dim + kv_dim)),
            wo_in, w1_in, w2_in, hw_in,
        ],
        out_specs=[pl.BlockSpec((rows, q_dim), attn_tile), wo_out, w1_out, w2_out, hw_out],
        out_shape=[jax.ShapeDtypeStruct((t, q_dim), BF16), wo_shape, w1_shape, w2_shape, hw_shape],
        scratch_shapes=[
            pltpu.VMEM((2, rows, q_dim), BF16),
            pltpu.VMEM((2, rows, kv_dim), BF16),
            pltpu.VMEM((2, rows, kv_dim), BF16),
            pltpu.VMEM((WINDOW, kv_dim), BF16),
            pltpu.VMEM((WINDOW, kv_dim), BF16),
            pltpu.VMEM((2 * n_heads, WINDOW, WINDOW), F32),
        ],
        compiler_params=_cparams("arbitrary"),
        name="swa_attention_layer",
    )(sinks.reshape(1, n_heads), x, mod_l, gain, w_in_b, qk_gain, p, pt, w_out_all, mlp_w1, mlp_w2, hgrn_w_in)


def _mlp_kernel(*refs, has_mod, n_sub):
    if has_mod:
        (x_ref, a_ref, mod_ref, gain_ref, wo_ref, w1_ref, w2_ref, c_ref, mw_ref, mb_ref,
         o_ref, mo_ref, h_ref) = refs
        step = pl.program_id(0) * pl.num_programs(1) + pl.program_id(1)
        rb = mw_ref.shape[0]

        @pl.when(step == 0)
        def _():
            mo_ref[...] = mb_ref[...]

        def side_work():
            cond = _silu(c_ref[pl.ds(pl.multiple_of(step * rb, rb), rb), :])
            mo_ref[...] += jnp.sum(cond * mw_ref[...], axis=0, keepdims=True)
    else:
        x_ref, a_ref, mod_ref, gain_ref, wo_ref, w1_ref, w2_ref, o_ref, h_ref = refs

        def side_work():
            pass
    f = pl.program_id(1)
    gate = mod_ref[5:6, :]

    def ffn(h):
        a = jnp.maximum(jnp.dot(h, w1_ref[...], preferred_element_type=F32), 0.0)
        return gate * jnp.dot((a * a).astype(BF16), w2_ref[...], preferred_element_type=F32)

    @pl.when(f == 0)
    def _():
        side_work()
        sub = x_ref.shape[0] // n_sub
        for s in range(n_sub):
            rs = slice(s * sub, (s + 1) * sub)
            x1 = x_ref[rs, :] + mod_ref[2:3, :] * jnp.dot(a_ref[rs, :], wo_ref[...], preferred_element_type=F32)
            h = _norm_mod(x1, gain_ref[...], mod_ref[3:4, :], mod_ref[4:5, :]).astype(BF16)
            h_ref[rs, :] = h
            o_ref[rs, :] = x1 + ffn(h)

    @pl.when(f > 0)
    def _():
        side_work()
        o_ref[...] += ffn(h_ref[...])


def _mlp_call(x, a, mod_l, gain, w_out, w1, w2, mod_job=None, tm=512, tf=1024, n_sub=1):
    t, d = x.shape
    kdim = a.shape[1]
    dff = w1.shape[1]
    nf = dff // tf
    in_specs = [
        pl.BlockSpec((tm, d), lambda i, f: (i, 0)),
        pl.BlockSpec((tm, kdim), lambda i, f: (i, 0)),
        _resident((N_MOD, d)),
        _resident((1, d)),
        _resident((kdim, d)),
        pl.BlockSpec((d, tf), lambda i, f: (0, f)),
        pl.BlockSpec((tf, d), lambda i, f: (f, 0)),
    ]
    out_specs = [pl.BlockSpec((tm, d), lambda i, f: (i, 0))]
    out_shape = [jax.ShapeDtypeStruct((t, d), F32)]
    args = [x, a, mod_l, gain, w_out, w1, w2]
    if mod_job is not None:
        c, mod_w, mod_b, layer = mod_job
        depth, _, n = mod_w.shape
        steps = (t // tm) * nf
        rb = d // steps
        assert rb * steps == d and rb % 8 == 0
        in_specs += [
            _resident((d, 1)),
            pl.BlockSpec((None, rb, n), lambda i, f: (layer, i * nf + f, 0)),
            pl.BlockSpec((None, 1, n), lambda i, f: (layer, 0, 0), pipeline_mode=pl.Buffered(1)),
        ]
        out_specs.append(pl.BlockSpec((1, n), lambda i, f: (0, 0)))
        out_shape.append(jax.ShapeDtypeStruct((1, n), F32))
        args += [c.reshape(d, 1), mod_w, mod_b.reshape(depth, 1, n)]
    outs = pl.pallas_call(
        functools.partial(_mlp_kernel, has_mod=mod_job is not None, n_sub=n_sub),
        grid=(t // tm, nf),
        in_specs=in_specs,
        out_specs=out_specs,
        out_shape=out_shape,
        scratch_shapes=[pltpu.VMEM((tm, d), BF16)],
        compiler_params=_cparams("arbitrary", "arbitrary"),
        name="outproj_relu2_mlp",
    )(*args)
    if mod_job is None:
        return outs[0]
    return outs[0], outs[1].reshape(N_MOD, d)


def _layer_lower_bound(lbl_ref):
    l0 = lbl_ref[0]
    l1 = lbl_ref[1]
    lm = jnp.maximum(l0, l1)
    e0 = jnp.exp(l0 - lm)
    e1 = jnp.exp(l1 - lm)
    p0 = e0 / (e0 + e1)
    p1 = e1 / (e0 + e1)
    return (p0 + p1) - p0


def _hgrn_inproj_kernel(x_ref, mod_ref, gain_ref, lbl_ref, w_ref, side_ref, side2_ref,
                        q_ref, lf_ref, kk_ref, v_ref, g_ref, sideb_ref, side2b_ref, h_ref, *, hpt, scale, n_sub):
    j = pl.program_id(1)
    lb = _layer_lower_bound(lbl_ref)
    width = hpt * HG_DK

    def project(h, rs, with_side):
        if with_side:
            sideb_ref[...] = side_ref[...].astype(BF16)
            side2b_ref[...] = side2_ref[...].astype(BF16)
        res = jnp.dot(h, w_ref[...], preferred_element_type=F32)
        for hh in range(hpt):
            seg = lambda s: res[:, s * width + hh * HG_DK: s * width + (hh + 1) * HG_DK]
            q_ref[hh, rs, :] = (_silu(seg(0)) * scale).astype(BF16)
            forget = lb[hh] + (1.0 - lb[hh]) * _sigmoid(seg(1))
            kk_ref[hh, rs, :] = 1.0 - forget
            lf_ref[hh, rs, :] = jnp.log(forget)
            v_ref[hh, rs, :] = seg(2).astype(BF16)
            g_ref[hh, rs, :] = _silu(seg(3)).astype(BF16)

    @pl.when(j == 0)
    def _():
        sub = x_ref.shape[0] // n_sub
        for s in range(n_sub):
            rs = slice(s * sub, (s + 1) * sub)
            h = _norm_mod(x_ref[rs, :], gain_ref[...], mod_ref[0:1, :], mod_ref[1:2, :]).astype(BF16)
            h_ref[rs, :] = h
            project(h, rs, s == 0)

    @pl.when(j > 0)
    def _():
        project(h_ref[...], slice(None), True)


def _hgrn_inproj_call(x, mod_l, gain, lb_logits, w_perm, n_heads, side_w, side_layer, side2_w, side2_layer,
                      tm=1024, hpt=2):
    t, d = x.shape
    depth = lb_logits.shape[0]
    assert depth == 2
    tn = 4 * hpt * HG_DK
    n_inner = n_heads // hpt
    head_major = lambda dt: jax.ShapeDtypeStruct((n_heads, t, HG_DK), dt)
    ospec = pl.BlockSpec((hpt, tm, HG_DK), lambda i, j: (j, i, 0))
    s_in, s_out, s_shape = _cast_job(side_w, side_layer, (t // tm) * n_inner, lambda i, j: i * n_inner + j)
    s2_in, s2_out, s2_shape = _cast_job(side2_w, side2_layer, (t // tm) * n_inner,
                                        lambda i, j: i * n_inner + j)
    return pl.pallas_call(
        functools.partial(_hgrn_inproj_kernel, hpt=hpt, scale=1.0 / math.sqrt(HG_DK), n_sub=4),
        grid=(t // tm, n_inner),
        in_specs=[
            pl.BlockSpec((tm, d), lambda i, j: (i, 0)),
            _resident((N_MOD, d)),
            _resident((1, d)),
            pl.BlockSpec((depth, hpt, 1, HG_DK), lambda i, j: (0, j, 0, 0)),
            pl.BlockSpec((d, tn), lambda i, j: (0, j)),
            s_in, s2_in,
        ],
        out_specs=[ospec] * 5 + [s_out, s2_out],
        out_shape=[head_major(BF16), head_major(F32), head_major(F32), head_major(BF16), head_major(BF16),
                   s_shape, s2_shape],
        scratch_shapes=[pltpu.VMEM((tm, d), BF16)],
        compiler_params=_cparams("arbitrary", "arbitrary"),
        name="hgrn_inproj",
    )(x, mod_l, gain, lb_logits.reshape(depth, n_heads, 1, HG_DK), w_perm, side_w, side2_w)


def _cum_matrix(tr):
    n_chunks = tr // CHUNK
    r = np.arange(tr)[:, None]
    c = np.arange(tr)[None, :]
    same = (r // CHUNK) == (c // CHUNK)
    incl = same & (c <= r)
    pivot = same & (c % CHUNK <= CHUNK // 2 - 1)
    rows = np.zeros((2 * n_chunks, tr), np.float32)
    for ch in range(n_chunks):
        rows[2 * ch, ch * CHUNK: ch * CHUNK + CHUNK // 2] = 1.0
        rows[2 * ch + 1, ch * CHUNK: (ch + 1) * CHUNK] = 1.0
    mats = np.concatenate([incl.astype(np.float32) - pivot.astype(np.float32), rows], axis=0)
    return jnp.asarray(mats, dtype=BF16)


def _hgrn_kernel(gain_ref, cm_ref, q_ref, lf_ref, kk_ref, v_ref, g_ref, side_ref,
                 o_ref, sideb_ref, st_ref, *, hb, tr):
    tstep = pl.program_id(1)
    sideb_ref[...] = side_ref[...].astype(BF16)

    @pl.when(tstep == 0)
    def _():
        st_ref[...] = jnp.zeros_like(st_ref)

    rowc = lax.broadcasted_iota(jnp.int32, (CHUNK, 2 * CHUNK), 0)
    colc = lax.broadcasted_iota(jnp.int32, (CHUNK, 2 * CHUNK), 1)
    causal2 = (colc % CHUNK) <= rowc
    n_chunks = tr // CHUNK
    zc = jnp.zeros((CHUNK, HG_DK), BF16)
    zs = jnp.zeros((HG_DK, HG_DK), BF16)

    def block_diag(a, b, z):
        return jnp.concatenate([jnp.concatenate([a, z], axis=1), jnp.concatenate([z, b], axis=1)], axis=0)

    n_pairs = hb // 2
    pairs = [(2 * pp, 2 * pp + 1) for pp in range(n_pairs)]
    chunk_rows = [slice(c * CHUNK, (c + 1) * CHUNK) for c in range(n_chunks)]

    cums = []
    for hh in range(hb):
        logf = lf_ref[hh]
        p_hi = logf.astype(BF16)
        p_lo = (logf - p_hi.astype(F32)).astype(BF16)
        cm = jnp.dot(cm_ref[...], jnp.concatenate([p_hi, p_lo], axis=1), preferred_element_type=F32)
        cums.append(cm[:, :HG_DK] + cm[:, HG_DK:])

    qe, ke, qb, ku, dec = [], [], [], [], []
    for hh in range(hb):
        bmp = cums[hh][:tr]
        qe_f = q_ref[hh].astype(F32) * jnp.exp(bmp)
        ke_f = kk_ref[hh] * jnp.exp(-bmp)
        qb_h, ku_h, dec_h = [], [], []
        for c, rs in enumerate(chunk_rows):
            piv = cums[hh][tr + 2 * c: tr + 2 * c + 1]
            blast = cums[hh][tr + 2 * c + 1: tr + 2 * c + 2]
            qb_h.append((qe_f[rs] * jnp.exp(piv)).astype(BF16))
            ku_h.append((ke_f[rs] * jnp.exp(blast - piv)).astype(BF16))
            dec_h.append(jnp.exp(blast))
        qe.append(qe_f.astype(BF16))
        ke.append(ke_f.astype(BF16))
        qb.append(qb_h)
        ku.append(ku_h)
        dec.append(dec_h)

    amat = [[None] * n_chunks for _ in pairs]
    upd = [[None] * n_chunks for _ in pairs]
    for c, rs in enumerate(chunk_rows):
        for pp, (h0, h1) in enumerate(pairs):
            v0 = v_ref[h0, rs, :]
            v1 = v_ref[h1, rs, :]
            a = lax.dot_general(jnp.concatenate([qe[h0][rs], qe[h1][rs]], axis=0),
                                jnp.concatenate([ke[h0][rs], ke[h1][rs]], axis=0),
                                (((1,), (1,)), ((), ())), preferred_element_type=F32)
            a = jnp.where(colc < CHUNK, a[:CHUNK], a[CHUNK:])
            amat[pp][c] = jnp.where(causal2, a, 0.0).astype(BF16)
            upd[pp][c] = lax.dot_general(jnp.concatenate([v0, v1], axis=0),
                                         block_diag(ku[h0][c], ku[h1][c], zc),
                                         (((0,), (0,)), ((), ())), preferred_element_type=F32)

    for c, rs in enumerate(chunk_rows):
        for pp, (h0, h1) in enumerate(pairs):
            st = st_ref[pp]
            st_b = st.astype(BF16)
            o_intra = jnp.dot(amat[pp][c], block_diag(v_ref[h0, rs, :], v_ref[h1, rs, :], zc),
                              preferred_element_type=F32)
            o_inter = lax.dot_general(jnp.concatenate([qb[h0][c], qb[h1][c]], axis=0),
                                      jnp.concatenate([st_b[:, :HG_DK], st_b[:, HG_DK:]], axis=0),
                                      (((1,), (1,)), ((), ())), preferred_element_type=F32)
            st_ref[pp] = st * jnp.concatenate([dec[h0][c], dec[h1][c]], axis=1) + upd[pp][c]
            for e, hh in enumerate((h0, h1)):
                oh = (o_intra[:, e * HG_DK:(e + 1) * HG_DK]
                      + o_inter[e * CHUNK:(e + 1) * CHUNK, e * HG_DK:(e + 1) * HG_DK])
                on = oh * lax.rsqrt(jnp.mean(oh * oh, axis=-1, keepdims=True) + EPS) * gain_ref[hh]
                o_ref[rs, hh * HG_DK:(hh + 1) * HG_DK] = (on * g_ref[hh, rs, :].astype(F32)).astype(BF16)


def _hgrn_call(q, lf, kk, v, g, o_gain, side_w, side_layer, hb=16, tr=256):
    n_heads, t, dk = q.shape
    assert dk == HG_DK and hb % 2 == 0
    cm = _cum_matrix(tr)
    n_inner = t // tr
    blk = pl.BlockSpec((hb, tr, dk), lambda h, s: (h, s, 0))
    s_in, s_out, s_shape = _cast_job(side_w, side_layer, (n_heads // hb) * n_inner,
                                     lambda h, s: h * n_inner + s)
    return pl.pallas_call(
        functools.partial(_hgrn_kernel, hb=hb, tr=tr),
        grid=(n_heads // hb, n_inner),
        in_specs=[
            pl.BlockSpec((hb, 1, dk), lambda h, s: (h, 0, 0)),
            _resident(cm.shape),
            blk, blk, blk, blk, blk,
            s_in,
        ],
        out_specs=[pl.BlockSpec((tr, hb * dk), lambda h, s: (s, h)), s_out],
        out_shape=[jax.ShapeDtypeStruct((t, n_heads * dk), BF16), s_shape],
        scratch_shapes=[pltpu.VMEM((hb // 2, dk, 2 * dk), F32)],
        compiler_params=_cparams("arbitrary", "arbitrary"),
        name="hgrn_recurrence",
    )(o_gain.reshape(n_heads, 1, dk), cm, q, lf, kk, v, g, side_w)


def kernel(x, c, mod_w, mod_b, norm_mix, norm_mlp, attn_w_in, attn_w_out, attn_q_gain, attn_k_gain,
           attn_sinks, hgrn_w_in, hgrn_w_out, hgrn_o_gain, hgrn_lb_logits, mlp_w1, mlp_w2):
    b, t, d = x.shape
    assert b == 1 and mod_w.shape[0] == 2
    xs = x.reshape(t, d)

    mod0, w_in_b = _mod_call(c, mod_w, mod_b, 0, attn_w_in, 0)

    q_dim = attn_w_out.shape[1]
    kv_dim = (attn_w_in.shape[2] - q_dim) // 2
    qk_gain = jnp.concatenate([jnp.tile(attn_q_gain[0], q_dim // HEAD_DIM),
                               jnp.tile(attn_k_gain[0], kv_dim // HEAD_DIM)]).reshape(1, q_dim + kv_dim)
    n_heads = hgrn_o_gain.shape[1]
    hpt = 2
    att, wob, w1b, w2b, hw_perm = _attn_layer_call(xs, mod0, norm_mix[0].reshape(1, d), w_in_b, qk_gain,
                                                   attn_sinks[0], attn_w_out, mlp_w1, mlp_w2, 0,
                                                   hgrn_w_in, 0, n_heads // hpt)
    xs, mod1 = _mlp_call(xs, att, mod0, norm_mlp[0].reshape(1, d), wob, w1b, w2b,
                         mod_job=(c, mod_w, mod_b, 1))

    hq, hlf, hkk, hv, hg, w1b, wob = _hgrn_inproj_call(xs, mod1, norm_mix[1].reshape(1, d), hgrn_lb_logits,
                                                       hw_perm, n_heads, mlp_w1, 1, hgrn_w_out, 0, hpt=hpt)
    ho, w2b = _hgrn_call(hq, hlf, hkk, hv, hg, hgrn_o_gain[0], mlp_w2, 1)
    xs = _mlp_call(xs, ho, mod1, norm_mlp[1].reshape(1, d), wob, w1b, w2b)
    return xs.reshape(b, t, d)
```

```python
import functools
import math

import numpy as np
import jax
import jax.numpy as jnp
from jax import lax
from jax.experimental import pallas as pl
from jax.experimental.pallas import tpu as pltpu

F32 = jnp.float32
BF16 = jnp.bfloat16

EPS = 1e-6
N_MOD = 6

HEAD_DIM = 64
GROUP = 8
WINDOW = 128
LANES = 128
LOG2E = math.log2(math.e)

HG_DK = 128
CHUNK = 64

VMEM_LIMIT = 56 * 1024 * 1024


def _cparams(*sem):
    return pltpu.CompilerParams(dimension_semantics=sem, vmem_limit_bytes=VMEM_LIMIT)


def _resident(shape):
    nd = len(shape)
    return pl.BlockSpec(shape, lambda *_: (0,) * nd, pipeline_mode=pl.Buffered(1))


def _cast_job(w, layer, n_steps, step_of):
    _, r, c = w.shape
    rb = r // n_steps
    assert rb * n_steps == r and rb % 16 == 0
    in_spec = pl.BlockSpec((None, rb, c), lambda *g: (layer, step_of(*g), 0))
    out_spec = pl.BlockSpec((rb, c), lambda *g: (step_of(*g), 0))
    return in_spec, out_spec, jax.ShapeDtypeStruct((r, c), BF16)


def _sigmoid(v):
    return 0.5 * jnp.tanh(0.5 * v) + 0.5


def _silu(v):
    return v * _sigmoid(v)


def _norm_mod(x, gain, shift, scale):
    ms = jnp.mean(x * x, axis=-1, keepdims=True)
    y = x * lax.rsqrt(ms + EPS) * gain
    return y * (1.0 + scale) + shift


def _mod_block(c_ref, w_ref, b_ref):
    cond = _silu(c_ref[...])
    return jnp.sum(cond * w_ref[...], axis=0, keepdims=True) + b_ref[...]


def _mod_specs(mod_w, layer, tn, col_of):
    _, d, n = mod_w.shape
    in_specs = [
        pl.BlockSpec((d, 1), lambda *g: (0, 0)),
        pl.BlockSpec((None, d, tn), lambda *g: (layer, 0, col_of(*g))),
        pl.BlockSpec((None, 1, tn), lambda *g: (layer, 0, col_of(*g))),
    ]
    out_spec = pl.BlockSpec((1, tn), lambda *g: (0, col_of(*g)))
    return in_specs, out_spec, jax.ShapeDtypeStruct((1, n), F32)


def _mod_kernel(c_ref, w_ref, b_ref, side_ref, o_ref, sideb_ref):
    sideb_ref[...] = side_ref[...].astype(BF16)
    o_ref[...] = _mod_block(c_ref, w_ref, b_ref)


def _mod_call(c, mod_w, mod_b, layer, side_w, side_layer, tn=768):
    depth, d, n = mod_w.shape
    nj = n // tn
    m_in, m_out, m_shape = _mod_specs(mod_w, layer, tn, lambda j: j)
    s_in, s_out, s_shape = _cast_job(side_w, side_layer, nj, lambda j: j)
    out, side_b = pl.pallas_call(
        _mod_kernel,
        grid=(nj,),
        in_specs=m_in + [s_in],
        out_specs=[m_out, s_out],
        out_shape=[m_shape, s_shape],
        compiler_params=_cparams("arbitrary"),
        name="mod_proj",
    )(c.reshape(d, 1), mod_w, mod_b.reshape(depth, 1, n), side_w)
    return out.reshape(N_MOD, d), side_b


PROJ_COLS = 256


def _qkv_finish(proj, qkg_ref, p_ref, pt_ref, q_dim, kv_dim, scale):
    qk = proj[:, :q_dim + kv_dim]
    ss = jnp.dot((qk * qk).astype(BF16), p_ref[...], preferred_element_type=F32)
    inv = lax.rsqrt(ss * (1.0 / HEAD_DIM) + EPS)
    inv_hi = inv.astype(BF16)
    inv_lo = (inv - inv_hi.astype(F32)).astype(BF16)
    inv_b = jnp.dot(jnp.concatenate([inv_hi, inv_lo], axis=1), pt_ref[...],
                    preferred_element_type=F32)
    qkn = qk * inv_b * qkg_ref[...]
    return ((qkn[:, :q_dim] * scale).astype(BF16), qkn[:, q_dim:].astype(BF16),
            proj[:, q_dim + kv_dim:].astype(BF16))


def _attn_layer_kernel(sink_ref, x_ref, mod_ref, gain_ref, wb_ref, qkg_ref, p_ref, pt_ref,
                       wo_ref, w1_ref, w2_ref, hw_ref,
                       o_ref, wob_ref, w1b_ref, w2b_ref, hwb_ref,
                       q_s, k_s, v_s, kt_s, vt_s, bias_ref, *, n_kv, n_blk, scale, hg_tiles, hg_cw):
    s = pl.program_id(0)
    wob_ref[...] = wo_ref[...].astype(BF16)
    w1b_ref[...] = w1_ref[...].astype(BF16)
    w2b_ref[...] = w2_ref[...].astype(BF16)
    for j in range(hg_tiles):
        for sg in range(4):
            dst = (j * 4 + sg) * hg_cw
            src = (sg * hg_tiles + j) * hg_cw
            hwb_ref[:, dst:dst + hg_cw] = hw_ref[:, src:src + hg_cw].astype(BF16)
    n_heads = n_kv * GROUP
    q_dim = n_heads * HEAD_DIM
    kv_dim = n_kv * HEAD_DIM
    row = lax.broadcasted_iota(jnp.int32, (WINDOW, WINDOW), 0)
    col = lax.broadcasted_iota(jnp.int32, (WINDOW, WINDOW), 1)
    in_cur = col <= row

    @pl.when(s == 0)
    def _():
        dist = jnp.where(in_cur, row - col, row - col + WINDOW).astype(F32)
        for hq in range(n_heads):
            slope = LOG2E * 2.0 ** (-8.0 * (hq + 1) / n_heads)
            bias_ref[hq] = -slope * dist
            bias_ref[n_heads + hq] = jnp.where(in_cur, -slope * dist, -jnp.inf)
        q_s[...] = jnp.zeros_like(q_s)
        k_s[...] = jnp.zeros_like(k_s)
        v_s[...] = jnp.zeros_like(v_s)
        kt_s[...] = jnp.zeros_like(kt_s)
        vt_s[...] = jnp.zeros_like(vt_s)

    wslot = s % 2
    rslot = 1 - wslot

    lane = lax.broadcasted_iota(jnp.int32, (WINDOW, LANES), 1)
    low = lane < HEAD_DIM
    one_lo = jnp.where(low, 1.0, 0.0).astype(BF16)
    one_hi = jnp.where(low, 0.0, 1.0).astype(BF16)
    den_rhs = jnp.concatenate([one_lo, one_lo, one_hi, one_hi], axis=0)
    zero = jnp.zeros((WINDOW, LANES), BF16)

    def halves(t, g):
        r = pltpu.roll(t, HEAD_DIM, axis=1)
        if g % 2 == 0:
            return jnp.where(low, t, zero), jnp.where(low, zero, r)
        return jnp.where(low, r, zero), jnp.where(low, zero, t)

    units = [(b_, g_) for b_ in range(n_blk) for g_ in range(n_kv)]
    n_pairs = GROUP // 2

    h = _norm_mod(x_ref[...], gain_ref[...], mod_ref[0:1, :], mod_ref[1:2, :]).astype(BF16)
    n_cols = (q_dim + 2 * kv_dim) // PROJ_COLS
    proj_chunks = []

    def project_next():
        if len(proj_chunks) < n_cols:
            cs = slice(len(proj_chunks) * PROJ_COLS, (len(proj_chunks) + 1) * PROJ_COLS)
            proj_chunks.append(jnp.dot(h, wb_ref[:, cs], preferred_element_type=F32))

    def logits_stage(blk, g):
        qrows = slice(blk * WINDOW, (blk + 1) * WINDOW)
        sl = slice((g // 2) * LANES, (g // 2 + 1) * LANES)
        if blk == 0:
            k_prev, v_prev = kt_s[:, sl], vt_s[:, sl]
            bias_off = jnp.where(s == 1, n_heads, 0)
        else:
            prows = slice((blk - 1) * WINDOW, blk * WINDOW)
            k_prev, v_prev = k_s[rslot, prows, sl], v_s[rslot, prows, sl]
            bias_off = 0
        kc_lo, kc_hi = halves(k_s[rslot, qrows, sl], g)
        kp_lo, kp_hi = halves(k_prev, g)
        vc_lo, vc_hi = halves(v_s[rslot, qrows, sl], g)
        vp_lo, vp_hi = halves(v_prev, g)
        k_rhs = jnp.concatenate([kc_lo, kc_hi, kp_lo, kp_hi], axis=0)
        v_rhs = jnp.concatenate([vc_lo, vp_lo, vc_hi, vp_hi], axis=0)
        q_base = g * GROUP * HEAD_DIM
        qs = jnp.concatenate(
            [q_s[rslot, qrows, q_base + t * LANES: q_base + (t + 1) * LANES] for t in range(n_pairs)], axis=0)
        logits = lax.dot_general(qs, k_rhs, (((1,), (1,)), ((), ())), preferred_element_type=F32)
        return logits, v_rhs, bias_off

    def softmax_pv_stage(blk, g, logits, v_rhs, bias_off):
        qrows = slice(blk * WINDOW, (blk + 1) * WINDOW)
        q_base = g * GROUP * HEAD_DIM
        pms, sink_terms = [], []
        for t in range(n_pairs):
            rows = slice(t * WINDOW, (t + 1) * WINDOW)
            probs, mx_e = [], []
            for e in range(2):
                hq = g * GROUP + 2 * t + e
                l_cur = logits[rows, e * WINDOW:(e + 1) * WINDOW]
                l_prev = logits[rows, (2 + e) * WINDOW:(3 + e) * WINDOW]
                sc = jnp.where(in_cur, l_cur, l_prev) + bias_ref[hq + bias_off]
                mx = jnp.max(sc, axis=1, keepdims=True)
                pe = jnp.exp2(sc - mx).astype(BF16)
                probs.append(jnp.where(in_cur, pe, zero))
                probs.append(jnp.where(in_cur, zero, pe))
                mx_e.append(mx)
            pms.append(jnp.concatenate(probs, axis=1))
            hq0 = g * GROUP + 2 * t
            sink_l = jnp.where(low[0:1], sink_ref[0, hq0] * LOG2E, sink_ref[0, hq0 + 1] * LOG2E)
            sink_terms.append(jnp.exp2(sink_l - jnp.where(low, mx_e[0], mx_e[1])))
        res = jnp.dot(jnp.concatenate(pms, axis=0), jnp.concatenate([v_rhs, den_rhs], axis=1),
                      preferred_element_type=F32)
        for t in range(n_pairs):
            rows = slice(t * WINDOW, (t + 1) * WINDOW)
            den = res[rows, LANES:] + sink_terms[t]
            o_ref[qrows, q_base + t * LANES: q_base + (t + 1) * LANES] = (res[rows, :LANES] / den).astype(BF16)

    staged = logits_stage(*units[0])
    project_next()
    project_next()
    for u, (blk, g) in enumerate(units):
        cur = staged
        project_next()
        if u + 1 < len(units):
            staged = logits_stage(*units[u + 1])
        softmax_pv_stage(blk, g, *cur)
    while len(proj_chunks) < n_cols:
        project_next()

    last = slice((n_blk - 1) * WINDOW, n_blk * WINDOW)
    kt_s[...] = k_s[rslot, last, :]
    vt_s[...] = v_s[rslot, last, :]

    qn, kn, vn = _qkv_finish(jnp.concatenate(proj_chunks, axis=1), qkg_ref, p_ref, pt_ref, q_dim, kv_dim, scale)
    q_s[wslot] = qn
    k_s[wslot] = kn
    v_s[wslot] = vn


def _attn_layer_call(x, mod_l, gain, w_in_b, qk_gain, sinks, w_out_all, mlp_w1, mlp_w2, layer,
                     hgrn_w_in, hg_layer, hg_tiles, n_blk=2):
    t, d = x.shape
    n = w_in_b.shape[1]
    kv_dim = (n - d) // 2
    q_dim = d
    n_kv = kv_dim // HEAD_DIM
    n_heads = n_kv * GROUP
    assert n_kv % 2 == 0 and q_dim == n_heads * HEAD_DIM and (q_dim + kv_dim) // HEAD_DIM <= LANES
    rows = n_blk * WINDOW
    nt = t // rows
    ind = (np.arange(q_dim + kv_dim)[:, None] // HEAD_DIM == np.arange(LANES)[None, :])
    p = jnp.asarray(ind, dtype=BF16)
    pt = jnp.asarray(np.concatenate([ind.T, ind.T], axis=0), dtype=BF16)
    proj_tile = lambda s: (jnp.minimum(s, nt - 1), 0)
    attn_tile = lambda s: (jnp.maximum(s - 1, 0), 0)
    cast_step = lambda s: jnp.minimum(s, nt - 1)
    wo_in, wo_out, wo_shape = _cast_job(w_out_all, layer, nt, cast_step)
    w1_in, w1_out, w1_shape = _cast_job(mlp_w1, layer, nt, cast_step)
    w2_in, w2_out, w2_shape = _cast_job(mlp_w2, layer, nt, cast_step)
    hw_in, hw_out, hw_shape = _cast_job(hgrn_w_in, hg_layer, nt, cast_step)
    hg_cw = hgrn_w_in.shape[2] // (4 * hg_tiles)
    return pl.pallas_call(
        functools.partial(_attn_layer_kernel, n_kv=n_kv, n_blk=n_blk, scale=LOG2E / math.sqrt(HEAD_DIM),
                          hg_tiles=hg_tiles, hg_cw=hg_cw),
        grid=(nt + 1,),
        in_specs=[
            pl.BlockSpec(memory_space=pltpu.SMEM),
            pl.BlockSpec((rows, d), proj_tile),
            _resident((N_MOD, d)),
            _resident((1, d)),
            _resident((d, n)),
            _resident((1, q_dim + kv_dim)),
            _resident((q_dim + kv_dim, LANES)),
            _resident((2 * LANES, q_dim + kv_dim)),
            wo_in, w1_in, w2_in, hw_in,
        ],
        out_specs=[pl.BlockSpec((rows, q_dim), attn_tile), wo_out, w1_out, w2_out, hw_out],
        out_shape=[jax.ShapeDtypeStruct((t, q_dim), BF16), wo_shape, w1_shape, w2_shape, hw_shape],
        scratch_shapes=[
            pltpu.VMEM((2, rows, q_dim), BF16),
            pltpu.VMEM((2, rows, kv_dim), BF16),
            pltpu.VMEM((2, rows, kv_dim), BF16),
            pltpu.VMEM((WINDOW, kv_dim), BF16),
            pltpu.VMEM((WINDOW, kv_dim), BF16),
            pltpu.VMEM((2 * n_heads, WINDOW, WINDOW), F32),
        ],
        compiler_params=_cparams("arbitrary"),
        name="swa_attention_layer",
    )(sinks.reshape(1, n_heads), x, mod_l, gain, w_in_b, qk_gain, p, pt, w_out_all, mlp_w1, mlp_w2, hgrn_w_in)


def _mlp_kernel(*refs, has_mod, n_sub):
    if has_mod:
        (x_ref, a_ref, mod_ref, gain_ref, wo_ref, w1_ref, w2_ref, c_ref, mw_ref, mb_ref,
         o_ref, mo_ref, h_ref, acc_ref) = refs
        step = pl.program_id(0) * pl.num_programs(1) + pl.program_id(1)
        rb = mw_ref.shape[0]

        @pl.when(step == 0)
        def _():
            acc_ref[...] = jnp.zeros_like(acc_ref)

        def side_work():
            cond = _silu(c_ref[pl.ds(pl.multiple_of(step * rb, rb), rb), :])
            part = cond[0:8] * mw_ref[0:8, :]
            for r in range(8, rb, 8):
                part = part + cond[r:r + 8] * mw_ref[r:r + 8, :]
            acc_ref[...] += part
    else:
        x_ref, a_ref, mod_ref, gain_ref, wo_ref, w1_ref, w2_ref, o_ref, h_ref = refs

        def side_work():
            pass
    f = pl.program_id(1)
    gate = mod_ref[5:6, :]

    def ffn(h):
        a = jnp.maximum(jnp.dot(h, w1_ref[...], preferred_element_type=F32), 0.0)
        return gate * jnp.dot((a * a).astype(BF16), w2_ref[...], preferred_element_type=F32)

    @pl.when(f == 0)
    def _():
        side_work()
        sub = x_ref.shape[0] // n_sub
        for s in range(n_sub):
            rs = slice(s * sub, (s + 1) * sub)
            x1 = x_ref[rs, :] + mod_ref[2:3, :] * jnp.dot(a_ref[rs, :], wo_ref[...], preferred_element_type=F32)
            h = _norm_mod(x1, gain_ref[...], mod_ref[3:4, :], mod_ref[4:5, :]).astype(BF16)
            h_ref[rs, :] = h
            o_ref[rs, :] = x1 + ffn(h)

    @pl.when(f > 0)
    def _():
        side_work()
        o_ref[...] += ffn(h_ref[...])

    if has_mod:
        @pl.when(step == pl.num_programs(0) * pl.num_programs(1) - 1)
        def _():
            mo_ref[...] = jnp.sum(acc_ref[...], axis=0, keepdims=True) + mb_ref[...]


def _mlp_call(x, a, mod_l, gain, w_out, w1, w2, mod_job=None, tm=512, tf=1024, n_sub=1):
    t, d = x.shape
    kdim = a.shape[1]
    dff = w1.shape[1]
    nf = dff // tf
    in_specs = [
        pl.BlockSpec((tm, d), lambda i, f: (i, 0)),
        pl.BlockSpec((tm, kdim), lambda i, f: (i, 0)),
        _resident((N_MOD, d)),
        _resident((1, d)),
        _resident((kdim, d)),
        pl.BlockSpec((d, tf), lambda i, f: (0, f)),
        pl.BlockSpec((tf, d), lambda i, f: (f, 0)),
    ]
    out_specs = [pl.BlockSpec((tm, d), lambda i, f: (i, 0))]
    out_shape = [jax.ShapeDtypeStruct((t, d), F32)]
    args = [x, a, mod_l, gain, w_out, w1, w2]
    scratch = [pltpu.VMEM((tm, d), BF16)]
    if mod_job is not None:
        c, mod_w, mod_b, layer = mod_job
        depth, _, n = mod_w.shape
        steps = (t // tm) * nf
        rb = d // steps
        assert rb * steps == d and rb % 8 == 0
        in_specs += [
            _resident((d, 1)),
            pl.BlockSpec((None, rb, n), lambda i, f: (layer, i * nf + f, 0)),
            pl.BlockSpec((None, 1, n), lambda i, f: (layer, 0, 0), pipeline_mode=pl.Buffered(1)),
        ]
        out_specs.append(pl.BlockSpec((1, n), lambda i, f: (0, 0)))
        out_shape.append(jax.ShapeDtypeStruct((1, n), F32))
        args += [c.reshape(d, 1), mod_w, mod_b.reshape(depth, 1, n)]
        scratch.append(pltpu.VMEM((8, n), F32))
    outs = pl.pallas_call(
        functools.partial(_mlp_kernel, has_mod=mod_job is not None, n_sub=n_sub),
        grid=(t // tm, nf),
        in_specs=in_specs,
        out_specs=out_specs,
        out_shape=out_shape,
        scratch_shapes=scratch,
        compiler_params=_cparams("arbitrary", "arbitrary"),
        name="outproj_relu2_mlp",
    )(*args)
    if mod_job is None:
        return outs[0]
    return outs[0], outs[1].reshape(N_MOD, d)


def _layer_lower_bound(lbl_ref):
    l0 = lbl_ref[0]
    l1 = lbl_ref[1]
    lm = jnp.maximum(l0, l1)
    e0 = jnp.exp(l0 - lm)
    e1 = jnp.exp(l1 - lm)
    p0 = e0 / (e0 + e1)
    p1 = e1 / (e0 + e1)
    return (p0 + p1) - p0


def _hgrn_inproj_kernel(x_ref, mod_ref, gain_ref, lbl_ref, w_ref, side_ref, side2_ref,
                        q_ref, lf_ref, kk_ref, v_ref, g_ref, sideb_ref, side2b_ref, h_ref, *, hpt, scale, n_sub):
    j = pl.program_id(1)
    lb = _layer_lower_bound(lbl_ref)
    width = hpt * HG_DK

    def project(h, rs, with_side):
        if with_side:
            sideb_ref[...] = side_ref[...].astype(BF16)
            side2b_ref[...] = side2_ref[...].astype(BF16)
        res = jnp.dot(h, w_ref[...], preferred_element_type=F32)
        for hh in range(hpt):
            seg = lambda s: res[:, s * width + hh * HG_DK: s * width + (hh + 1) * HG_DK]
            q_ref[hh, rs, :] = (_silu(seg(0)) * scale).astype(BF16)
            forget = lb[hh] + (1.0 - lb[hh]) * _sigmoid(seg(1))
            kk_ref[hh, rs, :] = 1.0 - forget
            lf_ref[hh, rs, :] = jnp.log(forget)
            v_ref[hh, rs, :] = seg(2).astype(BF16)
            g_ref[hh, rs, :] = _silu(seg(3)).astype(BF16)

    @pl.when(j == 0)
    def _():
        sub = x_ref.shape[0] // n_sub
        for s in range(n_sub):
            rs = slice(s * sub, (s + 1) * sub)
            h = _norm_mod(x_ref[rs, :], gain_ref[...], mod_ref[0:1, :], mod_ref[1:2, :]).astype(BF16)
            h_ref[rs, :] = h
            project(h, rs, s == 0)

    @pl.when(j > 0)
    def _():
        project(h_ref[...], slice(None), True)


def _hgrn_inproj_call(x, mod_l, gain, lb_logits, w_perm, n_heads, side_w, side_layer, side2_w, side2_layer,
                      tm=1024, hpt=2):
    t, d = x.shape
    depth = lb_logits.shape[0]
    assert depth == 2
    tn = 4 * hpt * HG_DK
    n_inner = n_heads // hpt
    head_major = lambda dt: jax.ShapeDtypeStruct((n_heads, t, HG_DK), dt)
    ospec = pl.BlockSpec((hpt, tm, HG_DK), lambda i, j: (j, i, 0))
    s_in, s_out, s_shape = _cast_job(side_w, side_layer, (t // tm) * n_inner, lambda i, j: i * n_inner + j)
    s2_in, s2_out, s2_shape = _cast_job(side2_w, side2_layer, (t // tm) * n_inner,
                                        lambda i, j: i * n_inner + j)
    return pl.pallas_call(
        functools.partial(_hgrn_inproj_kernel, hpt=hpt, scale=1.0 / math.sqrt(HG_DK), n_sub=4),
        grid=(t // tm, n_inner),
        in_specs=[
            pl.BlockSpec((tm, d), lambda i, j: (i, 0)),
            _resident((N_MOD, d)),
            _resident((1, d)),
            pl.BlockSpec((depth, hpt, 1, HG_DK), lambda i, j: (0, j, 0, 0)),
            pl.BlockSpec((d, tn), lambda i, j: (0, j)),
            s_in, s2_in,
        ],
        out_specs=[ospec] * 5 + [s_out, s2_out],
        out_shape=[head_major(BF16), head_major(F32), head_major(F32), head_major(BF16), head_major(BF16),
                   s_shape, s2_shape],
        scratch_shapes=[pltpu.VMEM((tm, d), BF16)],
        compiler_params=_cparams("arbitrary", "arbitrary"),
        name="hgrn_inproj",
    )(x, mod_l, gain, lb_logits.reshape(depth, n_heads, 1, HG_DK), w_perm, side_w, side2_w)


def _cum_matrix(tr):
    n_chunks = tr // CHUNK
    r = np.arange(tr)[:, None]
    c = np.arange(tr)[None, :]
    same = (r // CHUNK) == (c // CHUNK)
    incl = same & (c <= r)
    pivot = same & (c % CHUNK <= CHUNK // 2 - 1)
    rows = np.zeros((2 * n_chunks, tr), np.float32)
    for ch in range(n_chunks):
        rows[2 * ch, ch * CHUNK: ch * CHUNK + CHUNK // 2] = 1.0
        rows[2 * ch + 1, ch * CHUNK: (ch + 1) * CHUNK] = 1.0
    mats = np.concatenate([incl.astype(np.float32) - pivot.astype(np.float32), rows], axis=0)
    return jnp.asarray(mats, dtype=BF16)


def _hgrn_kernel(gain_ref, cm_ref, q_ref, lf_ref, kk_ref, v_ref, g_ref, side_ref,
                 o_ref, sideb_ref, st_ref, *, hb, tr):
    tstep = pl.program_id(1)
    sideb_ref[...] = side_ref[...].astype(BF16)

    @pl.when(tstep == 0)
    def _():
        st_ref[...] = jnp.zeros_like(st_ref)

    rowc = lax.broadcasted_iota(jnp.int32, (CHUNK, 2 * CHUNK), 0)
    colc = lax.broadcasted_iota(jnp.int32, (CHUNK, 2 * CHUNK), 1)
    causal2 = (colc % CHUNK) <= rowc
    n_chunks = tr // CHUNK
    zc = jnp.zeros((CHUNK, HG_DK), BF16)
    zs = jnp.zeros((HG_DK, HG_DK), BF16)

    def block_diag(a, b, z):
        return jnp.concatenate([jnp.concatenate([a, z], axis=1), jnp.concatenate([z, b], axis=1)], axis=0)

    n_pairs = hb // 2
    pairs = [(2 * pp, 2 * pp + 1) for pp in range(n_pairs)]
    chunk_rows = [slice(c * CHUNK, (c + 1) * CHUNK) for c in range(n_chunks)]

    cums = []
    for hh in range(hb):
        logf = lf_ref[hh]
        p_hi = logf.astype(BF16)
        p_lo = (logf - p_hi.astype(F32)).astype(BF16)
        cm = jnp.dot(cm_ref[...], jnp.concatenate([p_hi, p_lo], axis=1), preferred_element_type=F32)
        cums.append(cm[:, :HG_DK] + cm[:, HG_DK:])

    qe, ke, qb, ku, dec = [], [], [], [], []
    for hh in range(hb):
        bmp = cums[hh][:tr]
        qe_f = q_ref[hh].astype(F32) * jnp.exp(bmp)
        ke_f = kk_ref[hh] * jnp.exp(-bmp)
        qb_h, ku_h, dec_h = [], [], []
        for c, rs in enumerate(chunk_rows):
            piv = cums[hh][tr + 2 * c: tr + 2 * c + 1]
            blast = cums[hh][tr + 2 * c + 1: tr + 2 * c + 2]
            qb_h.append((qe_f[rs] * jnp.exp(piv)).astype(BF16))
            ku_h.append((ke_f[rs] * jnp.exp(blast - piv)).astype(BF16))
            dec_h.append(jnp.exp(blast))
        qe.append(qe_f.astype(BF16))
        ke.append(ke_f.astype(BF16))
        qb.append(qb_h)
        ku.append(ku_h)
        dec.append(dec_h)

    amat = [[None] * n_chunks for _ in pairs]
    upd = [[None] * n_chunks for _ in pairs]
    for c, rs in enumerate(chunk_rows):
        for pp, (h0, h1) in enumerate(pairs):
            v0 = v_ref[h0, rs, :]
            v1 = v_ref[h1, rs, :]
            a = lax.dot_general(jnp.concatenate([qe[h0][rs], qe[h1][rs]], axis=0),
                                jnp.concatenate([ke[h0][rs], ke[h1][rs]], axis=0),
                                (((1,), (1,)), ((), ())), preferred_element_type=F32)
            a = jnp.where(colc < CHUNK, a[:CHUNK], a[CHUNK:])
            amat[pp][c] = jnp.where(causal2, a, 0.0).astype(BF16)
            upd[pp][c] = lax.dot_general(jnp.concatenate([v0, v1], axis=0),
                                         block_diag(ku[h0][c], ku[h1][c], zc),
                                         (((0,), (0,)), ((), ())), preferred_element_type=F32)

    for c, rs in enumerate(chunk_rows):
        for pp, (h0, h1) in enumerate(pairs):
            st = st_ref[pp]
            st_b = st.astype(BF16)
            o_intra = jnp.dot(amat[pp][c], block_diag(v_ref[h0, rs, :], v_ref[h1, rs, :], zc),
                              preferred_element_type=F32)
            o_inter = lax.dot_general(jnp.concatenate([qb[h0][c], qb[h1][c]], axis=0),
                                      jnp.concatenate([st_b[:, :HG_DK], st_b[:, HG_DK:]], axis=0),
                                      (((1,), (1,)), ((), ())), preferred_element_type=F32)
            st_ref[pp] = st * jnp.concatenate([dec[h0][c], dec[h1][c]], axis=1) + upd[pp][c]
            for e, hh in enumerate((h0, h1)):
                oh = (o_intra[:, e * HG_DK:(e + 1) * HG_DK]
                      + o_inter[e * CHUNK:(e + 1) * CHUNK, e * HG_DK:(e + 1) * HG_DK])
                on = oh * lax.rsqrt(jnp.mean(oh * oh, axis=-1, keepdims=True) + EPS) * gain_ref[hh]
                o_ref[rs, hh * HG_DK:(hh + 1) * HG_DK] = (on * g_ref[hh, rs, :].astype(F32)).astype(BF16)


def _hgrn_call(q, lf, kk, v, g, o_gain, side_w, side_layer, hb=16, tr=256):
    n_heads, t, dk = q.shape
    assert dk == HG_DK and hb % 2 == 0
    cm = _cum_matrix(tr)
    n_inner = t // tr
    blk = pl.BlockSpec((hb, tr, dk), lambda h, s: (h, s, 0))
    s_in, s_out, s_shape = _cast_job(side_w, side_layer, (n_heads // hb) * n_inner,
                                     lambda h, s: h * n_inner + s)
    return pl.pallas_call(
        functools.partial(_hgrn_kernel, hb=hb, tr=tr),
        grid=(n_heads // hb, n_inner),
        in_specs=[
            pl.BlockSpec((hb, 1, dk), lambda h, s: (h, 0, 0)),
            _resident(cm.shape),
            blk, blk, blk, blk, blk,
            s_in,
        ],
        out_specs=[pl.BlockSpec((tr, hb * dk), lambda h, s: (s, h)), s_out],
        out_shape=[jax.ShapeDtypeStruct((t, n_heads * dk), BF16), s_shape],
        scratch_shapes=[pltpu.VMEM((hb // 2, dk, 2 * dk), F32)],
        compiler_params=_cparams("arbitrary", "arbitrary"),
        name="hgrn_recurrence",
    )(o_gain.reshape(n_heads, 1, dk), cm, q, lf, kk, v, g, side_w)


def kernel(x, c, mod_w, mod_b, norm_mix, norm_mlp, attn_w_in, attn_w_out, attn_q_gain, attn_k_gain,
           attn_sinks, hgrn_w_in, hgrn_w_out, hgrn_o_gain, hgrn_lb_logits, mlp_w1, mlp_w2):
    b, t, d = x.shape
    assert b == 1 and mod_w.shape[0] == 2
    xs = x.reshape(t, d)

    mod0, w_in_b = _mod_call(c, mod_w, mod_b, 0, attn_w_in, 0)

    q_dim = attn_w_out.shape[1]
    kv_dim = (attn_w_in.shape[2] - q_dim) // 2
    qk_gain = jnp.concatenate([jnp.tile(attn_q_gain[0], q_dim // HEAD_DIM),
                               jnp.tile(attn_k_gain[0], kv_dim // HEAD_DIM)]).reshape(1, q_dim + kv_dim)
    n_heads = hgrn_o_gain.shape[1]
    hpt = 2
    att, wob, w1b, w2b, hw_perm = _attn_layer_call(xs, mod0, norm_mix[0].reshape(1, d), w_in_b, qk_gain,
                                                   attn_sinks[0], attn_w_out, mlp_w1, mlp_w2, 0,
                                                   hgrn_w_in, 0, n_heads // hpt)
    xs, mod1 = _mlp_call(xs, att, mod0, norm_mlp[0].reshape(1, d), wob, w1b, w2b,
                         mod_job=(c, mod_w, mod_b, 1))

    hq, hlf, hkk, hv, hg, w1b, wob = _hgrn_inproj_call(xs, mod1, norm_mix[1].reshape(1, d), hgrn_lb_logits,
                                                       hw_perm, n_heads, mlp_w1, 1, hgrn_w_out, 0, hpt=hpt)
    ho, w2b = _hgrn_call(hq, hlf, hkk, hv, hg, hgrn_o_gain[0], mlp_w2, 1)
    xs = _mlp_call(xs, ho, mod1, norm_mlp[1].reshape(1, d), wob, w1b, w2b)
    return xs.reshape(b, t, d)
```

```python
import functools
import math

import numpy as np
import jax
import jax.numpy as jnp
from jax import lax
from jax.experimental import pallas as pl
from jax.experimental.pallas import tpu as pltpu

F32 = jnp.float32
BF16 = jnp.bfloat16

EPS = 1e-6
N_MOD = 6

HEAD_DIM = 64
GROUP = 8
WINDOW = 128
LANES = 128
LOG2E = math.log2(math.e)

HG_DK = 128
CHUNK = 64

VMEM_LIMIT = 56 * 1024 * 1024


def _cparams(*sem):
    return pltpu.CompilerParams(dimension_semantics=sem, vmem_limit_bytes=VMEM_LIMIT)


def _resident(shape):
    nd = len(shape)
    return pl.BlockSpec(shape, lambda *_: (0,) * nd, pipeline_mode=pl.Buffered(1))


def _cast_job(w, layer, n_steps, step_of):
    _, r, c = w.shape
    rb = r // n_steps
    assert rb * n_steps == r and rb % 16 == 0
    in_spec = pl.BlockSpec((None, rb, c), lambda *g: (layer, step_of(*g), 0))
    out_spec = pl.BlockSpec((rb, c), lambda *g: (step_of(*g), 0))
    return in_spec, out_spec, jax.ShapeDtypeStruct((r, c), BF16)


def _sigmoid(v):
    return 0.5 * jnp.tanh(0.5 * v) + 0.5


def _silu(v):
    return v * _sigmoid(v)


def _norm_mod(x, gain, shift, scale):
    ms = jnp.mean(x * x, axis=-1, keepdims=True)
    y = x * lax.rsqrt(ms + EPS) * gain
    return y * (1.0 + scale) + shift


def _mod_block(c_ref, w_ref, b_ref):
    cond = _silu(c_ref[...])
    return jnp.sum(cond * w_ref[...], axis=0, keepdims=True) + b_ref[...]


def _mod_specs(mod_w, layer, tn, col_of):
    _, d, n = mod_w.shape
    in_specs = [
        pl.BlockSpec((d, 1), lambda *g: (0, 0)),
        pl.BlockSpec((None, d, tn), lambda *g: (layer, 0, col_of(*g))),
        pl.BlockSpec((None, 1, tn), lambda *g: (layer, 0, col_of(*g))),
    ]
    out_spec = pl.BlockSpec((1, tn), lambda *g: (0, col_of(*g)))
    return in_specs, out_spec, jax.ShapeDtypeStruct((1, n), F32)


def _mod_kernel(c_ref, w_ref, b_ref, side_ref, o_ref, sideb_ref):
    sideb_ref[...] = side_ref[...].astype(BF16)
    o_ref[...] = _mod_block(c_ref, w_ref, b_ref)


def _mod_call(c, mod_w, mod_b, layer, side_w, side_layer, tn=768):
    depth, d, n = mod_w.shape
    nj = n // tn
    m_in, m_out, m_shape = _mod_specs(mod_w, layer, tn, lambda j: j)
    s_in, s_out, s_shape = _cast_job(side_w, side_layer, nj, lambda j: j)
    out, side_b = pl.pallas_call(
        _mod_kernel,
        grid=(nj,),
        in_specs=m_in + [s_in],
        out_specs=[m_out, s_out],
        out_shape=[m_shape, s_shape],
        compiler_params=_cparams("arbitrary"),
        name="mod_proj",
    )(c.reshape(d, 1), mod_w, mod_b.reshape(depth, 1, n), side_w)
    return out.reshape(N_MOD, d), side_b


PROJ_COLS = 256


def _qkv_finish(proj, qkg_ref, p_ref, pt_ref, q_dim, kv_dim, scale):
    qk = proj[:, :q_dim + kv_dim]
    ss = jnp.dot((qk * qk).astype(BF16), p_ref[...], preferred_element_type=F32)
    inv = lax.rsqrt(ss * (1.0 / HEAD_DIM) + EPS)
    inv_hi = inv.astype(BF16)
    inv_lo = (inv - inv_hi.astype(F32)).astype(BF16)
    inv_b = jnp.dot(jnp.concatenate([inv_hi, inv_lo], axis=1), pt_ref[...],
                    preferred_element_type=F32)
    qkn = qk * inv_b * qkg_ref[...]
    return ((qkn[:, :q_dim] * scale).astype(BF16), qkn[:, q_dim:].astype(BF16),
            proj[:, q_dim + kv_dim:].astype(BF16))


def _attn_layer_kernel(sink_ref, x_ref, mod_ref, gain_ref, wb_ref, qkg_ref, p_ref, pt_ref,
                       wo_ref, w1_ref, w2_ref, hw_ref,
                       o_ref, wob_ref, w1b_ref, w2b_ref, hwb_ref,
                       q_s, k_s, v_s, kt_s, vt_s, bias_ref, *, n_kv, n_blk, scale, hg_tiles, hg_cw):
    s = pl.program_id(0)
    wob_ref[...] = wo_ref[...].astype(BF16)
    w1b_ref[...] = w1_ref[...].astype(BF16)
    w2b_ref[...] = w2_ref[...].astype(BF16)
    for j in range(hg_tiles):
        for sg in range(4):
            dst = (j * 4 + sg) * hg_cw
            src = (sg * hg_tiles + j) * hg_cw
            hwb_ref[:, dst:dst + hg_cw] = hw_ref[:, src:src + hg_cw].astype(BF16)
    n_heads = n_kv * GROUP
    q_dim = n_heads * HEAD_DIM
    kv_dim = n_kv * HEAD_DIM
    row = lax.broadcasted_iota(jnp.int32, (WINDOW, WINDOW), 0)
    col = lax.broadcasted_iota(jnp.int32, (WINDOW, WINDOW), 1)
    in_cur = col <= row

    @pl.when(s == 0)
    def _():
        dist = jnp.where(in_cur, row - col, row - col + WINDOW).astype(F32)
        for hq in range(n_heads):
            slope = LOG2E * 2.0 ** (-8.0 * (hq + 1) / n_heads)
            bias_ref[hq] = -slope * dist
            bias_ref[n_heads + hq] = jnp.where(in_cur, -slope * dist, -jnp.inf)
        q_s[...] = jnp.zeros_like(q_s)
        k_s[...] = jnp.zeros_like(k_s)
        v_s[...] = jnp.zeros_like(v_s)
        kt_s[...] = jnp.zeros_like(kt_s)
        vt_s[...] = jnp.zeros_like(vt_s)

    wslot = s % 2
    rslot = 1 - wslot

    lane = lax.broadcasted_iota(jnp.int32, (WINDOW, LANES), 1)
    low = lane < HEAD_DIM
    one_lo = jnp.where(low, 1.0, 0.0).astype(BF16)
    one_hi = jnp.where(low, 0.0, 1.0).astype(BF16)
    den_rhs = jnp.concatenate([one_lo, one_lo, one_hi, one_hi], axis=0)
    zero = jnp.zeros((WINDOW, LANES), BF16)

    def halves(t, g):
        r = pltpu.roll(t, HEAD_DIM, axis=1)
        if g % 2 == 0:
            return jnp.where(low, t, zero), jnp.where(low, zero, r)
        return jnp.where(low, r, zero), jnp.where(low, zero, t)

    units = [(b_, g_) for b_ in range(n_blk) for g_ in range(n_kv)]
    n_pairs = GROUP // 2

    h = _norm_mod(x_ref[...], gain_ref[...], mod_ref[0:1, :], mod_ref[1:2, :]).astype(BF16)
    n_cols = (q_dim + 2 * kv_dim) // PROJ_COLS
    proj_chunks = []

    def project_next():
        if len(proj_chunks) < n_cols:
            cs = slice(len(proj_chunks) * PROJ_COLS, (len(proj_chunks) + 1) * PROJ_COLS)
            proj_chunks.append(jnp.dot(h, wb_ref[:, cs], preferred_element_type=F32))

    def logits_stage(blk, g):
        qrows = slice(blk * WINDOW, (blk + 1) * WINDOW)
        sl = slice((g // 2) * LANES, (g // 2 + 1) * LANES)
        if blk == 0:
            k_prev, v_prev = kt_s[:, sl], vt_s[:, sl]
            bias_off = jnp.where(s == 1, n_heads, 0)
        else:
            prows = slice((blk - 1) * WINDOW, blk * WINDOW)
            k_prev, v_prev = k_s[rslot, prows, sl], v_s[rslot, prows, sl]
            bias_off = 0
        kc_lo, kc_hi = halves(k_s[rslot, qrows, sl], g)
        kp_lo, kp_hi = halves(k_prev, g)
        vc_lo, vc_hi = halves(v_s[rslot, qrows, sl], g)
        vp_lo, vp_hi = halves(v_prev, g)
        k_rhs = jnp.concatenate([kc_lo, kc_hi, kp_lo, kp_hi], axis=0)
        v_rhs = jnp.concatenate([vc_lo, vp_lo, vc_hi, vp_hi], axis=0)
        q_base = g * GROUP * HEAD_DIM
        qs = jnp.concatenate(
            [q_s[rslot, qrows, q_base + t * LANES: q_base + (t + 1) * LANES] for t in range(n_pairs)], axis=0)
        logits = lax.dot_general(qs, k_rhs, (((1,), (1,)), ((), ())), preferred_element_type=F32)
        return logits, v_rhs, bias_off

    def softmax_pv_stage(blk, g, logits, v_rhs, bias_off):
        qrows = slice(blk * WINDOW, (blk + 1) * WINDOW)
        q_base = g * GROUP * HEAD_DIM
        pms, sink_terms = [], []
        for t in range(n_pairs):
            rows = slice(t * WINDOW, (t + 1) * WINDOW)
            probs, mx_e = [], []
            for e in range(2):
                hq = g * GROUP + 2 * t + e
                l_cur = logits[rows, e * WINDOW:(e + 1) * WINDOW]
                l_prev = logits[rows, (2 + e) * WINDOW:(3 + e) * WINDOW]
                sc = jnp.where(in_cur, l_cur, l_prev) + bias_ref[hq + bias_off]
                mx = jnp.max(sc, axis=1, keepdims=True)
                pe = jnp.exp2(sc - mx).astype(BF16)
                probs.append(jnp.where(in_cur, pe, zero))
                probs.append(jnp.where(in_cur, zero, pe))
                mx_e.append(mx)
            pms.append(jnp.concatenate(probs, axis=1))
            hq0 = g * GROUP + 2 * t
            sink_l = jnp.where(low[0:1], sink_ref[0, hq0] * LOG2E, sink_ref[0, hq0 + 1] * LOG2E)
            sink_terms.append(jnp.exp2(sink_l - jnp.where(low, mx_e[0], mx_e[1])))
        res = jnp.dot(jnp.concatenate(pms, axis=0), jnp.concatenate([v_rhs, den_rhs], axis=1),
                      preferred_element_type=F32)
        for t in range(n_pairs):
            rows = slice(t * WINDOW, (t + 1) * WINDOW)
            den = res[rows, LANES:] + sink_terms[t]
            o_ref[qrows, q_base + t * LANES: q_base + (t + 1) * LANES] = (res[rows, :LANES] / den).astype(BF16)

    staged = logits_stage(*units[0])
    project_next()
    project_next()
    for u, (blk, g) in enumerate(units):
        cur = staged
        project_next()
        if u + 1 < len(units):
            staged = logits_stage(*units[u + 1])
        softmax_pv_stage(blk, g, *cur)
    while len(proj_chunks) < n_cols:
        project_next()

    last = slice((n_blk - 1) * WINDOW, n_blk * WINDOW)
    kt_s[...] = k_s[rslot, last, :]
    vt_s[...] = v_s[rslot, last, :]

    qn, kn, vn = _qkv_finish(jnp.concatenate(proj_chunks, axis=1), qkg_ref, p_ref, pt_ref, q_dim, kv_dim, scale)
    q_s[wslot] = qn
    k_s[wslot] = kn
    v_s[wslot] = vn


def _attn_layer_call(x, mod_l, gain, w_in_b, qk_gain, sinks, w_out_all, mlp_w1, mlp_w2, layer,
                     hgrn_w_in, hg_layer, hg_tiles, n_blk=2):
    t, d = x.shape
    n = w_in_b.shape[1]
    kv_dim = (n - d) // 2
    q_dim = d
    n_kv = kv_dim // HEAD_DIM
    n_heads = n_kv * GROUP
    assert n_kv % 2 == 0 and q_dim == n_heads * HEAD_DIM and (q_dim + kv_dim) // HEAD_DIM <= LANES
    rows = n_blk * WINDOW
    nt = t // rows
    ind = (np.arange(q_dim + kv_dim)[:, None] // HEAD_DIM == np.arange(LANES)[None, :])
    p = jnp.asarray(ind, dtype=BF16)
    pt = jnp.asarray(np.concatenate([ind.T, ind.T], axis=0), dtype=BF16)
    proj_tile = lambda s: (jnp.minimum(s, nt - 1), 0)
    attn_tile = lambda s: (jnp.maximum(s - 1, 0), 0)
    cast_step = lambda s: jnp.minimum(s, nt - 1)
    wo_in, wo_out, wo_shape = _cast_job(w_out_all, layer, nt, cast_step)
    w1_in, w1_out, w1_shape = _cast_job(mlp_w1, layer, nt, cast_step)
    w2_in, w2_out, w2_shape = _cast_job(mlp_w2, layer, nt, cast_step)
    hw_in, hw_out, hw_shape = _cast_job(hgrn_w_in, hg_layer, nt, cast_step)
    hg_cw = hgrn_w_in.shape[2] // (4 * hg_tiles)
    return pl.pallas_call(
        functools.partial(_attn_layer_kernel, n_kv=n_kv, n_blk=n_blk, scale=LOG2E / math.sqrt(HEAD_DIM),
                          hg_tiles=hg_tiles, hg_cw=hg_cw),
        grid=(nt + 1,),
        in_specs=[
            pl.BlockSpec(memory_space=pltpu.SMEM),
            pl.BlockSpec((rows, d), proj_tile),
            _resident((N_MOD, d)),
            _resident((1, d)),
            _resident((d, n)),
            _resident((1, q_dim + kv_dim)),
            _resident((q_dim + kv_dim, LANES)),
            _resident((2 * LANES, q_dim + kv_dim)),
            wo_in, w1_in, w2_in, hw_in,
        ],
        out_specs=[pl.BlockSpec((rows, q_dim), attn_tile), wo_out, w1_out, w2_out, hw_out],
        out_shape=[jax.ShapeDtypeStruct((t, q_dim), BF16), wo_shape, w1_shape, w2_shape, hw_shape],
        scratch_shapes=[
            pltpu.VMEM((2, rows, q_dim), BF16),
            pltpu.VMEM((2, rows, kv_dim), BF16),
            pltpu.VMEM((2, rows, kv_dim), BF16),
            pltpu.VMEM((WINDOW, kv_dim), BF16),
            pltpu.VMEM((WINDOW, kv_dim), BF16),
            pltpu.VMEM((2 * n_heads, WINDOW, WINDOW), F32),
        ],
        compiler_params=_cparams("arbitrary"),
        name="swa_attention_layer",
    )(sinks.reshape(1, n_heads), x, mod_l, gain, w_in_b, qk_gain, p, pt, w_out_all, mlp_w1, mlp_w2, hgrn_w_in)


def _mlp_kernel(*refs, has_mod, n_sub):
    if has_mod:
        (x_ref, a_ref, mod_ref, gain_ref, wo_ref, w1_ref, w2_ref, c_ref, mw_ref, mb_ref,
         o_ref, mo_ref, h_ref, acc_ref) = refs
        step = pl.program_id(0) * pl.num_programs(1) + pl.program_id(1)
        rb = mw_ref.shape[0]

        @pl.when(step == 0)
        def _():
            acc_ref[...] = jnp.zeros_like(acc_ref)

        def side_work():
            cond = _silu(c_ref[pl.ds(pl.multiple_of(step * rb, rb), rb), :])
            part = cond[0:8] * mw_ref[0:8, :]
            for r in range(8, rb, 8):
                part = part + cond[r:r + 8] * mw_ref[r:r + 8, :]
            acc_ref[...] += part
    else:
        x_ref, a_ref, mod_ref, gain_ref, wo_ref, w1_ref, w2_ref, o_ref, h_ref = refs

        def side_work():
            pass
    f = pl.program_id(1)
    gate = mod_ref[5:6, :]

    def ffn(h):
        a = jnp.maximum(jnp.dot(h, w1_ref[...], preferred_element_type=F32), 0.0)
        return gate * jnp.dot((a * a).astype(BF16), w2_ref[...], preferred_element_type=F32)

    @pl.when(f == 0)
    def _():
        side_work()
        sub = x_ref.shape[0] // n_sub
        for s in range(n_sub):
            rs = slice(s * sub, (s + 1) * sub)
            x1 = x_ref[rs, :] + mod_ref[2:3, :] * jnp.dot(a_ref[rs, :], wo_ref[...], preferred_element_type=F32)
            h = _norm_mod(x1, gain_ref[...], mod_ref[3:4, :], mod_ref[4:5, :]).astype(BF16)
            h_ref[rs, :] = h
            o_ref[rs, :] = x1 + ffn(h)

    @pl.when(f > 0)
    def _():
        side_work()
        o_ref[...] += ffn(h_ref[...])

    if has_mod:
        @pl.when(step == pl.num_programs(0) * pl.num_programs(1) - 1)
        def _():
            mo_ref[...] = jnp.sum(acc_ref[...], axis=0, keepdims=True) + mb_ref[...]


def _mlp_call(x, a, mod_l, gain, w_out, w1, w2, mod_job=None, tm=512, tf=1024, n_sub=1):
    t, d = x.shape
    kdim = a.shape[1]
    dff = w1.shape[1]
    nf = dff // tf
    in_specs = [
        pl.BlockSpec((tm, d), lambda i, f: (i, 0)),
        pl.BlockSpec((tm, kdim), lambda i, f: (i, 0)),
        _resident((N_MOD, d)),
        _resident((1, d)),
        _resident((kdim, d)),
        pl.BlockSpec((d, tf), lambda i, f: (0, f)),
        pl.BlockSpec((tf, d), lambda i, f: (f, 0)),
    ]
    out_specs = [pl.BlockSpec((tm, d), lambda i, f: (i, 0))]
    out_shape = [jax.ShapeDtypeStruct((t, d), F32)]
    args = [x, a, mod_l, gain, w_out, w1, w2]
    scratch = [pltpu.VMEM((tm, d), BF16)]
    if mod_job is not None:
        c, mod_w, mod_b, layer = mod_job
        depth, _, n = mod_w.shape
        steps = (t // tm) * nf
        rb = d // steps
        assert rb * steps == d and rb % 8 == 0
        in_specs += [
            _resident((d, 1)),
            pl.BlockSpec((None, rb, n), lambda i, f: (layer, i * nf + f, 0)),
            pl.BlockSpec((None, 1, n), lambda i, f: (layer, 0, 0), pipeline_mode=pl.Buffered(1)),
        ]
        out_specs.append(pl.BlockSpec((1, n), lambda i, f: (0, 0)))
        out_shape.append(jax.ShapeDtypeStruct((1, n), F32))
        args += [c.reshape(d, 1), mod_w, mod_b.reshape(depth, 1, n)]
        scratch.append(pltpu.VMEM((8, n), F32))
    outs = pl.pallas_call(
        functools.partial(_mlp_kernel, has_mod=mod_job is not None, n_sub=n_sub),
        grid=(t // tm, nf),
        in_specs=in_specs,
        out_specs=out_specs,
        out_shape=out_shape,
        scratch_shapes=scratch,
        compiler_params=_cparams("arbitrary", "arbitrary"),
        name="outproj_relu2_mlp",
    )(*args)
    if mod_job is None:
        return outs[0]
    return outs[0], outs[1].reshape(N_MOD, d)


def _layer_lower_bound(lbl_ref):
    l0 = lbl_ref[0]
    l1 = lbl_ref[1]
    lm = jnp.maximum(l0, l1)
    e0 = jnp.exp(l0 - lm)
    e1 = jnp.exp(l1 - lm)
    p0 = e0 / (e0 + e1)
    p1 = e1 / (e0 + e1)
    return (p0 + p1) - p0


def _hgrn_inproj_kernel(x_ref, mod_ref, gain_ref, lbl_ref, w_ref, side_ref, side2_ref,
                        q_ref, lf_ref, v_ref, g_ref, sideb_ref, side2b_ref, h_ref, *, hpt, scale, n_sub):
    j = pl.program_id(1)
    lb = _layer_lower_bound(lbl_ref)
    width = hpt * HG_DK

    def project(h, rs, with_side):
        if with_side:
            sideb_ref[...] = side_ref[...].astype(BF16)
            side2b_ref[...] = side2_ref[...].astype(BF16)
        res = jnp.dot(h, w_ref[...], preferred_element_type=F32)
        for hh in range(hpt):
            seg = lambda s: res[:, s * width + hh * HG_DK: s * width + (hh + 1) * HG_DK]
            q_ref[hh, rs, :] = (_silu(seg(0)) * scale).astype(BF16)
            forget = lb[hh] + (1.0 - lb[hh]) * _sigmoid(seg(1))
            lf_ref[hh, rs, :] = jnp.log(forget)
            v_ref[hh, rs, :] = seg(2).astype(BF16)
            g_ref[hh, rs, :] = _silu(seg(3)).astype(BF16)

    @pl.when(j == 0)
    def _():
        sub = x_ref.shape[0] // n_sub
        for s in range(n_sub):
            rs = slice(s * sub, (s + 1) * sub)
            h = _norm_mod(x_ref[rs, :], gain_ref[...], mod_ref[0:1, :], mod_ref[1:2, :]).astype(BF16)
            h_ref[rs, :] = h
            project(h, rs, s == 0)

    @pl.when(j > 0)
    def _():
        project(h_ref[...], slice(None), True)


def _hgrn_inproj_call(x, mod_l, gain, lb_logits, w_perm, n_heads, side_w, side_layer, side2_w, side2_layer,
                      tm=1024, hpt=2):
    t, d = x.shape
    depth = lb_logits.shape[0]
    assert depth == 2
    tn = 4 * hpt * HG_DK
    n_inner = n_heads // hpt
    head_major = lambda dt: jax.ShapeDtypeStruct((n_heads, t, HG_DK), dt)
    ospec = pl.BlockSpec((hpt, tm, HG_DK), lambda i, j: (j, i, 0))
    s_in, s_out, s_shape = _cast_job(side_w, side_layer, (t // tm) * n_inner, lambda i, j: i * n_inner + j)
    s2_in, s2_out, s2_shape = _cast_job(side2_w, side2_layer, (t // tm) * n_inner,
                                        lambda i, j: i * n_inner + j)
    return pl.pallas_call(
        functools.partial(_hgrn_inproj_kernel, hpt=hpt, scale=1.0 / math.sqrt(HG_DK), n_sub=4),
        grid=(t // tm, n_inner),
        in_specs=[
            pl.BlockSpec((tm, d), lambda i, j: (i, 0)),
            _resident((N_MOD, d)),
            _resident((1, d)),
            pl.BlockSpec((depth, hpt, 1, HG_DK), lambda i, j: (0, j, 0, 0)),
            pl.BlockSpec((d, tn), lambda i, j: (0, j)),
            s_in, s2_in,
        ],
        out_specs=[ospec] * 4 + [s_out, s2_out],
        out_shape=[head_major(BF16), head_major(F32), head_major(BF16), head_major(BF16), s_shape, s2_shape],
        scratch_shapes=[pltpu.VMEM((tm, d), BF16)],
        compiler_params=_cparams("arbitrary", "arbitrary"),
        name="hgrn_inproj",
    )(x, mod_l, gain, lb_logits.reshape(depth, n_heads, 1, HG_DK), w_perm, side_w, side2_w)


def _cum_matrix(tr):
    n_chunks = tr // CHUNK
    r = np.arange(tr)[:, None]
    c = np.arange(tr)[None, :]
    same = (r // CHUNK) == (c // CHUNK)
    incl = same & (c <= r)
    pivot = same & (c % CHUNK <= CHUNK // 2 - 1)
    rows = np.zeros((2 * n_chunks, tr), np.float32)
    for ch in range(n_chunks):
        rows[2 * ch, ch * CHUNK: ch * CHUNK + CHUNK // 2] = 1.0
        rows[2 * ch + 1, ch * CHUNK: (ch + 1) * CHUNK] = 1.0
    mats = np.concatenate([incl.astype(np.float32) - pivot.astype(np.float32), rows], axis=0)
    return jnp.asarray(mats, dtype=BF16)


def _hgrn_kernel(gain_ref, cm_ref, q_ref, lf_ref, v_ref, g_ref, side_ref,
                 o_ref, sideb_ref, st_ref, *, hb, tr):
    tstep = pl.program_id(1)
    sideb_ref[...] = side_ref[...].astype(BF16)

    @pl.when(tstep == 0)
    def _():
        st_ref[...] = jnp.zeros_like(st_ref)

    rowc = lax.broadcasted_iota(jnp.int32, (CHUNK, 2 * CHUNK), 0)
    colc = lax.broadcasted_iota(jnp.int32, (CHUNK, 2 * CHUNK), 1)
    causal2 = (colc % CHUNK) <= rowc
    n_chunks = tr // CHUNK
    zc = jnp.zeros((CHUNK, HG_DK), BF16)
    zs = jnp.zeros((HG_DK, HG_DK), BF16)

    def block_diag(a, b, z):
        return jnp.concatenate([jnp.concatenate([a, z], axis=1), jnp.concatenate([z, b], axis=1)], axis=0)

    n_pairs = hb // 2
    pairs = [(2 * pp, 2 * pp + 1) for pp in range(n_pairs)]
    chunk_rows = [slice(c * CHUNK, (c + 1) * CHUNK) for c in range(n_chunks)]

    cums = []
    for hh in range(hb):
        logf = lf_ref[hh]
        p_hi = logf.astype(BF16)
        p_lo = (logf - p_hi.astype(F32)).astype(BF16)
        cm = jnp.dot(cm_ref[...], jnp.concatenate([p_hi, p_lo], axis=1), preferred_element_type=F32)
        cums.append(cm[:, :HG_DK] + cm[:, HG_DK:])

    qe, ke, qb, ku, dec = [], [], [], [], []
    for hh in range(hb):
        bmp = cums[hh][:tr]
        qe_f = q_ref[hh].astype(F32) * jnp.exp(bmp)
        ke_f = (1.0 - jnp.exp(lf_ref[hh])) * jnp.exp(-bmp)
        qb_h, ku_h, dec_h = [], [], []
        for c, rs in enumerate(chunk_rows):
            piv = cums[hh][tr + 2 * c: tr + 2 * c + 1]
            blast = cums[hh][tr + 2 * c + 1: tr + 2 * c + 2]
            qb_h.append((qe_f[rs] * jnp.exp(piv)).astype(BF16))
            ku_h.append((ke_f[rs] * jnp.exp(blast - piv)).astype(BF16))
            dec_h.append(jnp.exp(blast))
        qe.append(qe_f.astype(BF16))
        ke.append(ke_f.astype(BF16))
        qb.append(qb_h)
        ku.append(ku_h)
        dec.append(dec_h)

    amat = [[None] * n_chunks for _ in pairs]
    upd = [[None] * n_chunks for _ in pairs]
    for c, rs in enumerate(chunk_rows):
        for pp, (h0, h1) in enumerate(pairs):
            v0 = v_ref[h0, rs, :]
            v1 = v_ref[h1, rs, :]
            a = lax.dot_general(jnp.concatenate([qe[h0][rs], qe[h1][rs]], axis=0),
                                jnp.concatenate([ke[h0][rs], ke[h1][rs]], axis=0),
                                (((1,), (1,)), ((), ())), preferred_element_type=F32)
            a = jnp.where(colc < CHUNK, a[:CHUNK], a[CHUNK:])
            amat[pp][c] = jnp.where(causal2, a, 0.0).astype(BF16)
            upd[pp][c] = lax.dot_general(jnp.concatenate([v0, v1], axis=0),
                                         block_diag(ku[h0][c], ku[h1][c], zc),
                                         (((0,), (0,)), ((), ())), preferred_element_type=F32)

    for c, rs in enumerate(chunk_rows):
        for pp, (h0, h1) in enumerate(pairs):
            st = st_ref[pp]
            st_b = st.astype(BF16)
            o_intra = jnp.dot(amat[pp][c], block_diag(v_ref[h0, rs, :], v_ref[h1, rs, :], zc),
                              preferred_element_type=F32)
            o_inter = lax.dot_general(jnp.concatenate([qb[h0][c], qb[h1][c]], axis=0),
                                      jnp.concatenate([st_b[:, :HG_DK], st_b[:, HG_DK:]], axis=0),
                                      (((1,), (1,)), ((), ())), preferred_element_type=F32)
            st_ref[pp] = st * jnp.concatenate([dec[h0][c], dec[h1][c]], axis=1) + upd[pp][c]
            for e, hh in enumerate((h0, h1)):
                oh = (o_intra[:, e * HG_DK:(e + 1) * HG_DK]
                      + o_inter[e * CHUNK:(e + 1) * CHUNK, e * HG_DK:(e + 1) * HG_DK])
                on = oh * lax.rsqrt(jnp.mean(oh * oh, axis=-1, keepdims=True) + EPS) * gain_ref[hh]
                o_ref[rs, hh * HG_DK:(hh + 1) * HG_DK] = (on * g_ref[hh, rs, :].astype(F32)).astype(BF16)


def _hgrn_call(q, lf, v, g, o_gain, side_w, side_layer, hb=16, tr=256):
    n_heads, t, dk = q.shape
    assert dk == HG_DK and hb % 2 == 0
    cm = _cum_matrix(tr)
    n_inner = t // tr
    blk = pl.BlockSpec((hb, tr, dk), lambda h, s: (h, s, 0))
    s_in, s_out, s_shape = _cast_job(side_w, side_layer, (n_heads // hb) * n_inner,
                                     lambda h, s: h * n_inner + s)
    return pl.pallas_call(
        functools.partial(_hgrn_kernel, hb=hb, tr=tr),
        grid=(n_heads // hb, n_inner),
        in_specs=[
            pl.BlockSpec((hb, 1, dk), lambda h, s: (h, 0, 0)),
            _resident(cm.shape),
            blk, blk, blk, blk,
            s_in,
        ],
        out_specs=[pl.BlockSpec((tr, hb * dk), lambda h, s: (s, h)), s_out],
        out_shape=[jax.ShapeDtypeStruct((t, n_heads * dk), BF16), s_shape],
        scratch_shapes=[pltpu.VMEM((hb // 2, dk, 2 * dk), F32)],
        compiler_params=_cparams("arbitrary", "arbitrary"),
        name="hgrn_recurrence",
    )(o_gain.reshape(n_heads, 1, dk), cm, q, lf, v, g, side_w)


def kernel(x, c, mod_w, mod_b, norm_mix, norm_mlp, attn_w_in, attn_w_out, attn_q_gain, attn_k_gain,
           attn_sinks, hgrn_w_in, hgrn_w_out, hgrn_o_gain, hgrn_lb_logits, mlp_w1, mlp_w2):
    b, t, d = x.shape
    assert b == 1 and mod_w.shape[0] == 2
    xs = x.reshape(t, d)

    mod0, w_in_b = _mod_call(c, mod_w, mod_b, 0, attn_w_in, 0)

    q_dim = attn_w_out.shape[1]
    kv_dim = (attn_w_in.shape[2] - q_dim) // 2
    qk_gain = jnp.concatenate([jnp.tile(attn_q_gain[0], q_dim // HEAD_DIM),
                               jnp.tile(attn_k_gain[0], kv_dim // HEAD_DIM)]).reshape(1, q_dim + kv_dim)
    n_heads = hgrn_o_gain.shape[1]
    hpt = 2
    att, wob, w1b, w2b, hw_perm = _attn_layer_call(xs, mod0, norm_mix[0].reshape(1, d), w_in_b, qk_gain,
                                                   attn_sinks[0], attn_w_out, mlp_w1, mlp_w2, 0,
                                                   hgrn_w_in, 0, n_heads // hpt)
    xs, mod1 = _mlp_call(xs, att, mod0, norm_mlp[0].reshape(1, d), wob, w1b, w2b,
                         mod_job=(c, mod_w, mod_b, 1))

    hq, hlf, hv, hg, w1b, wob = _hgrn_inproj_call(xs, mod1, norm_mix[1].reshape(1, d), hgrn_lb_logits,
                                                  hw_perm, n_heads, mlp_w1, 1, hgrn_w_out, 0, hpt=hpt)
    ho, w2b = _hgrn_call(hq, hlf, hv, hg, hgrn_o_gain[0], mlp_w2, 1)
    xs = _mlp_call(xs, ho, mod1, norm_mlp[1].reshape(1, d), wob, w1b, w2b)
    return xs.reshape(b, t, d)
```

```python
import functools
import math

import numpy as np
import jax
import jax.numpy as jnp
from jax import lax
from jax.experimental import pallas as pl
from jax.experimental.pallas import tpu as pltpu

F32 = jnp.float32
BF16 = jnp.bfloat16

EPS = 1e-6
N_MOD = 6

HEAD_DIM = 64
GROUP = 8
WINDOW = 128
LANES = 128
LOG2E = math.log2(math.e)

HG_DK = 128
CHUNK = 64

VMEM_LIMIT = 56 * 1024 * 1024


def _cparams(*sem):
    return pltpu.CompilerParams(dimension_semantics=sem, vmem_limit_bytes=VMEM_LIMIT)


def _resident(shape):
    nd = len(shape)
    return pl.BlockSpec(shape, lambda *_: (0,) * nd, pipeline_mode=pl.Buffered(1))


def _cast_job(w, layer, n_steps, step_of):
    _, r, c = w.shape
    rb = r // n_steps
    assert rb * n_steps == r and rb % 16 == 0
    in_spec = pl.BlockSpec((None, rb, c), lambda *g: (layer, step_of(*g), 0))
    out_spec = pl.BlockSpec((rb, c), lambda *g: (step_of(*g), 0))
    return in_spec, out_spec, jax.ShapeDtypeStruct((r, c), BF16)


def _sigmoid(v):
    return 0.5 * jnp.tanh(0.5 * v) + 0.5


def _silu(v):
    return v * _sigmoid(v)


def _norm_mod(x, gain, shift, scale):
    ms = jnp.mean(x * x, axis=-1, keepdims=True)
    y = x * lax.rsqrt(ms + EPS) * gain
    return y * (1.0 + scale) + shift


def _mod_block(c_ref, w_ref, b_ref):
    cond = _silu(c_ref[...])
    return jnp.sum(cond * w_ref[...], axis=0, keepdims=True) + b_ref[...]


def _mod_specs(mod_w, layer, tn, col_of):
    _, d, n = mod_w.shape
    in_specs = [
        pl.BlockSpec((d, 1), lambda *g: (0, 0)),
        pl.BlockSpec((None, d, tn), lambda *g: (layer, 0, col_of(*g))),
        pl.BlockSpec((None, 1, tn), lambda *g: (layer, 0, col_of(*g))),
    ]
    out_spec = pl.BlockSpec((1, tn), lambda *g: (0, col_of(*g)))
    return in_specs, out_spec, jax.ShapeDtypeStruct((1, n), F32)


def _mod_kernel(c_ref, w_ref, b_ref, side_ref, o_ref, sideb_ref):
    sideb_ref[...] = side_ref[...].astype(BF16)
    o_ref[...] = _mod_block(c_ref, w_ref, b_ref)


def _mod_call(c, mod_w, mod_b, layer, side_w, side_layer, tn=768):
    depth, d, n = mod_w.shape
    nj = n // tn
    m_in, m_out, m_shape = _mod_specs(mod_w, layer, tn, lambda j: j)
    s_in, s_out, s_shape = _cast_job(side_w, side_layer, nj, lambda j: j)
    out, side_b = pl.pallas_call(
        _mod_kernel,
        grid=(nj,),
        in_specs=m_in + [s_in],
        out_specs=[m_out, s_out],
        out_shape=[m_shape, s_shape],
        compiler_params=_cparams("arbitrary"),
        name="mod_proj",
    )(c.reshape(d, 1), mod_w, mod_b.reshape(depth, 1, n), side_w)
    return out.reshape(N_MOD, d), side_b


PROJ_COLS = 256


def _qkv_finish(proj, qkg_ref, p_ref, pt_ref, q_dim, kv_dim, scale):
    qk = proj[:, :q_dim + kv_dim]
    ss = jnp.dot((qk * qk).astype(BF16), p_ref[...], preferred_element_type=F32)
    inv = lax.rsqrt(ss * (1.0 / HEAD_DIM) + EPS)
    inv_hi = inv.astype(BF16)
    inv_lo = (inv - inv_hi.astype(F32)).astype(BF16)
    inv_b = jnp.dot(jnp.concatenate([inv_hi, inv_lo], axis=1), pt_ref[...],
                    preferred_element_type=F32)
    qkn = qk * inv_b * qkg_ref[...]
    return ((qkn[:, :q_dim] * scale).astype(BF16), qkn[:, q_dim:].astype(BF16),
            proj[:, q_dim + kv_dim:].astype(BF16))


def _attn_layer_kernel(sink_ref, x_ref, mod_ref, gain_ref, wb_ref, qkg_ref, p_ref, pt_ref,
                       wo_ref, w1_ref, w2_ref, hw_ref,
                       o_ref, wob_ref, w1b_ref, w2b_ref, hwb_ref,
                       q_s, k_s, v_s, kt_s, vt_s, bias_ref, *, n_kv, n_blk, scale, hg_tiles, hg_cw):
    s = pl.program_id(0)
    wob_ref[...] = wo_ref[...].astype(BF16)
    w1b_ref[...] = w1_ref[...].astype(BF16)
    w2b_ref[...] = w2_ref[...].astype(BF16)
    for j in range(hg_tiles):
        for sg in range(4):
            dst = (j * 4 + sg) * hg_cw
            src = (sg * hg_tiles + j) * hg_cw
            hwb_ref[:, dst:dst + hg_cw] = hw_ref[:, src:src + hg_cw].astype(BF16)
    n_heads = n_kv * GROUP
    q_dim = n_heads * HEAD_DIM
    kv_dim = n_kv * HEAD_DIM
    row = lax.broadcasted_iota(jnp.int32, (WINDOW, WINDOW), 0)
    col = lax.broadcasted_iota(jnp.int32, (WINDOW, WINDOW), 1)
    in_cur = col <= row

    @pl.when(s == 0)
    def _():
        dist = jnp.where(in_cur, row - col, row - col + WINDOW).astype(F32)
        for hq in range(n_heads):
            slope = LOG2E * 2.0 ** (-8.0 * (hq + 1) / n_heads)
            bias_ref[hq] = -slope * dist
            bias_ref[n_heads + hq] = jnp.where(in_cur, -slope * dist, -jnp.inf)
        q_s[...] = jnp.zeros_like(q_s)
        k_s[...] = jnp.zeros_like(k_s)
        v_s[...] = jnp.zeros_like(v_s)
        kt_s[...] = jnp.zeros_like(kt_s)
        vt_s[...] = jnp.zeros_like(vt_s)

    wslot = s % 2
    rslot = 1 - wslot

    lane = lax.broadcasted_iota(jnp.int32, (WINDOW, LANES), 1)
    low = lane < HEAD_DIM
    one_lo = jnp.where(low, 1.0, 0.0).astype(BF16)
    one_hi = jnp.where(low, 0.0, 1.0).astype(BF16)
    den_rhs = jnp.concatenate([one_lo, one_lo, one_hi, one_hi], axis=0)
    zero = jnp.zeros((WINDOW, LANES), BF16)

    def halves(t, g):
        r = pltpu.roll(t, HEAD_DIM, axis=1)
        if g % 2 == 0:
            return jnp.where(low, t, zero), jnp.where(low, zero, r)
        return jnp.where(low, r, zero), jnp.where(low, zero, t)

    units = [(b_, g_) for b_ in range(n_blk) for g_ in range(n_kv)]
    n_pairs = GROUP // 2

    h = _norm_mod(x_ref[...], gain_ref[...], mod_ref[0:1, :], mod_ref[1:2, :]).astype(BF16)
    n_cols = (q_dim + 2 * kv_dim) // PROJ_COLS
    proj_chunks = []

    def project_next():
        if len(proj_chunks) < n_cols:
            cs = slice(len(proj_chunks) * PROJ_COLS, (len(proj_chunks) + 1) * PROJ_COLS)
            proj_chunks.append(jnp.dot(h, wb_ref[:, cs], preferred_element_type=F32))

    def logits_stage(blk, g):
        qrows = slice(blk * WINDOW, (blk + 1) * WINDOW)
        sl = slice((g // 2) * LANES, (g // 2 + 1) * LANES)
        if blk == 0:
            k_prev, v_prev = kt_s[:, sl], vt_s[:, sl]
            bias_off = jnp.where(s == 1, n_heads, 0)
        else:
            prows = slice((blk - 1) * WINDOW, blk * WINDOW)
            k_prev, v_prev = k_s[rslot, prows, sl], v_s[rslot, prows, sl]
            bias_off = 0
        kc_lo, kc_hi = halves(k_s[rslot, qrows, sl], g)
        kp_lo, kp_hi = halves(k_prev, g)
        vc_lo, vc_hi = halves(v_s[rslot, qrows, sl], g)
        vp_lo, vp_hi = halves(v_prev, g)
        k_rhs = jnp.concatenate([kc_lo, kc_hi, kp_lo, kp_hi], axis=0)
        v_rhs = jnp.concatenate([vc_lo, vp_lo, vc_hi, vp_hi], axis=0)
        q_base = g * GROUP * HEAD_DIM
        qs = jnp.concatenate(
            [q_s[rslot, qrows, q_base + t * LANES: q_base + (t + 1) * LANES] for t in range(n_pairs)], axis=0)
        logits = lax.dot_general(qs, k_rhs, (((1,), (1,)), ((), ())), preferred_element_type=F32)
        return logits, v_rhs, bias_off

    def softmax_pv_stage(blk, g, logits, v_rhs, bias_off):
        qrows = slice(blk * WINDOW, (blk + 1) * WINDOW)
        q_base = g * GROUP * HEAD_DIM
        pms, sink_terms = [], []
        for t in range(n_pairs):
            rows = slice(t * WINDOW, (t + 1) * WINDOW)
            probs, mx_e = [], []
            for e in range(2):
                hq = g * GROUP + 2 * t + e
                l_cur = logits[rows, e * WINDOW:(e + 1) * WINDOW]
                l_prev = logits[rows, (2 + e) * WINDOW:(3 + e) * WINDOW]
                sc = jnp.where(in_cur, l_cur, l_prev) + bias_ref[hq + bias_off]
                mx = jnp.max(sc, axis=1, keepdims=True)
                pe = jnp.exp2(sc - mx).astype(BF16)
                probs.append(jnp.where(in_cur, pe, zero))
                probs.append(jnp.where(in_cur, zero, pe))
                mx_e.append(mx)
            pms.append(jnp.concatenate(probs, axis=1))
            hq0 = g * GROUP + 2 * t
            sink_l = jnp.where(low[0:1], sink_ref[0, hq0] * LOG2E, sink_ref[0, hq0 + 1] * LOG2E)
            sink_terms.append(jnp.exp2(sink_l - jnp.where(low, mx_e[0], mx_e[1])))
        res = jnp.dot(jnp.concatenate(pms, axis=0), jnp.concatenate([v_rhs, den_rhs], axis=1),
                      preferred_element_type=F32)
        for t in range(n_pairs):
            rows = slice(t * WINDOW, (t + 1) * WINDOW)
            den = res[rows, LANES:] + sink_terms[t]
            o_ref[qrows, q_base + t * LANES: q_base + (t + 1) * LANES] = (res[rows, :LANES] / den).astype(BF16)

    staged = logits_stage(*units[0])
    project_next()
    project_next()
    for u, (blk, g) in enumerate(units):
        cur = staged
        project_next()
        if u + 1 < len(units):
            staged = logits_stage(*units[u + 1])
        softmax_pv_stage(blk, g, *cur)
    while len(proj_chunks) < n_cols:
        project_next()

    last = slice((n_blk - 1) * WINDOW, n_blk * WINDOW)
    kt_s[...] = k_s[rslot, last, :]
    vt_s[...] = v_s[rslot, last, :]

    qn, kn, vn = _qkv_finish(jnp.concatenate(proj_chunks, axis=1), qkg_ref, p_ref, pt_ref, q_dim, kv_dim, scale)
    q_s[wslot] = qn
    k_s[wslot] = kn
    v_s[wslot] = vn


def _attn_layer_call(x, mod_l, gain, w_in_b, qk_gain, sinks, w_out_all, mlp_w1, mlp_w2, layer,
                     hgrn_w_in, hg_layer, hg_tiles, n_blk=2):
    t, d = x.shape
    n = w_in_b.shape[1]
    kv_dim = (n - d) // 2
    q_dim = d
    n_kv = kv_dim // HEAD_DIM
    n_heads = n_kv * GROUP
    assert n_kv % 2 == 0 and q_dim == n_heads * HEAD_DIM and (q_dim + kv_dim) // HEAD_DIM <= LANES
    rows = n_blk * WINDOW
    nt = t // rows
    ind = (np.arange(q_dim + kv_dim)[:, None] // HEAD_DIM == np.arange(LANES)[None, :])
    p = jnp.asarray(ind, dtype=BF16)
    pt = jnp.asarray(np.concatenate([ind.T, ind.T], axis=0), dtype=BF16)
    proj_tile = lambda s: (jnp.minimum(s, nt - 1), 0)
    attn_tile = lambda s: (jnp.maximum(s - 1, 0), 0)
    cast_step = lambda s: jnp.minimum(s, nt - 1)
    wo_in, wo_out, wo_shape = _cast_job(w_out_all, layer, nt, cast_step)
    w1_in, w1_out, w1_shape = _cast_job(mlp_w1, layer, nt, cast_step)
    w2_in, w2_out, w2_shape = _cast_job(mlp_w2, layer, nt, cast_step)
    hw_in, hw_out, hw_shape = _cast_job(hgrn_w_in, hg_layer, nt, cast_step)
    hg_cw = hgrn_w_in.shape[2] // (4 * hg_tiles)
    return pl.pallas_call(
        functools.partial(_attn_layer_kernel, n_kv=n_kv, n_blk=n_blk, scale=LOG2E / math.sqrt(HEAD_DIM),
                          hg_tiles=hg_tiles, hg_cw=hg_cw),
        grid=(nt + 1,),
        in_specs=[
            pl.BlockSpec(memory_space=pltpu.SMEM),
            pl.BlockSpec((rows, d), proj_tile),
            _resident((N_MOD, d)),
            _resident((1, d)),
            _resident((d, n)),
            _resident((1, q_dim + kv_dim)),
            _resident((q_dim + kv_dim, LANES)),
            _resident((2 * LANES, q_dim + kv_dim)),
            wo_in, w1_in, w2_in, hw_in,
        ],
        out_specs=[pl.BlockSpec((rows, q_dim), attn_tile), wo_out, w1_out, w2_out, hw_out],
        out_shape=[jax.ShapeDtypeStruct((t, q_dim), BF16), wo_shape, w1_shape, w2_shape, hw_shape],
        scratch_shapes=[
            pltpu.VMEM((2, rows, q_dim), BF16),
            pltpu.VMEM((2, rows, kv_dim), BF16),
            pltpu.VMEM((2, rows, kv_dim), BF16),
            pltpu.VMEM((WINDOW, kv_dim), BF16),
            pltpu.VMEM((WINDOW, kv_dim), BF16),
            pltpu.VMEM((2 * n_heads, WINDOW, WINDOW), F32),
        ],
        compiler_params=_cparams("arbitrary"),
        name="swa_attention_layer",
    )(sinks.reshape(1, n_heads), x, mod_l, gain, w_in_b, qk_gain, p, pt, w_out_all, mlp_w1, mlp_w2, hgrn_w_in)


def _mlp_kernel(*refs, has_mod, n_sub):
    if has_mod:
        (x_ref, a_ref, mod_ref, gain_ref, wo_ref, w1_ref, w2_ref, c_ref, mw_ref, mb_ref,
         o_ref, mo_ref, h_ref, acc_ref) = refs
        step = pl.program_id(0) * pl.num_programs(1) + pl.program_id(1)
        rb = mw_ref.shape[0]

        @pl.when(step == 0)
        def _():
            acc_ref[...] = jnp.zeros_like(acc_ref)

        def side_work():
            cond = _silu(c_ref[pl.ds(pl.multiple_of(step * rb, rb), rb), :])
            part = cond[0:8] * mw_ref[0:8, :]
            for r in range(8, rb, 8):
                part = part + cond[r:r + 8] * mw_ref[r:r + 8, :]
            acc_ref[...] += part
    else:
        x_ref, a_ref, mod_ref, gain_ref, wo_ref, w1_ref, w2_ref, o_ref, h_ref = refs

        def side_work():
            pass
    f = pl.program_id(1)
    gate = mod_ref[5:6, :]

    def ffn(h):
        a = jnp.maximum(jnp.dot(h, w1_ref[...], preferred_element_type=F32), 0.0)
        return gate * jnp.dot((a * a).astype(BF16), w2_ref[...], preferred_element_type=F32)

    @pl.when(f == 0)
    def _():
        side_work()
        sub = x_ref.shape[0] // n_sub
        for s in range(n_sub):
            rs = slice(s * sub, (s + 1) * sub)
            x1 = x_ref[rs, :] + mod_ref[2:3, :] * jnp.dot(a_ref[rs, :], wo_ref[...], preferred_element_type=F32)
            h = _norm_mod(x1, gain_ref[...], mod_ref[3:4, :], mod_ref[4:5, :]).astype(BF16)
            h_ref[rs, :] = h
            o_ref[rs, :] = x1 + ffn(h)

    @pl.when(f > 0)
    def _():
        side_work()
        o_ref[...] += ffn(h_ref[...])

    if has_mod:
        @pl.when(step == pl.num_programs(0) * pl.num_programs(1) - 1)
        def _():
            mo_ref[...] = jnp.sum(acc_ref[...], axis=0, keepdims=True) + mb_ref[...]


def _mlp_call(x, a, mod_l, gain, w_out, w1, w2, mod_job=None, tm=512, tf=1024, n_sub=1):
    t, d = x.shape
    kdim = a.shape[1]
    dff = w1.shape[1]
    nf = dff // tf
    in_specs = [
        pl.BlockSpec((tm, d), lambda i, f: (i, 0)),
        pl.BlockSpec((tm, kdim), lambda i, f: (i, 0)),
        _resident((N_MOD, d)),
        _resident((1, d)),
        _resident((kdim, d)),
        pl.BlockSpec((d, tf), lambda i, f: (0, f)),
        pl.BlockSpec((tf, d), lambda i, f: (f, 0)),
    ]
    out_specs = [pl.BlockSpec((tm, d), lambda i, f: (i, 0))]
    out_shape = [jax.ShapeDtypeStruct((t, d), F32)]
    args = [x, a, mod_l, gain, w_out, w1, w2]
    scratch = [pltpu.VMEM((tm, d), BF16)]
    if mod_job is not None:
        c, mod_w, mod_b, layer = mod_job
        depth, _, n = mod_w.shape
        steps = (t // tm) * nf
        rb = d // steps
        assert rb * steps == d and rb % 8 == 0
        in_specs += [
            _resident((d, 1)),
            pl.BlockSpec((None, rb, n), lambda i, f: (layer, i * nf + f, 0)),
            pl.BlockSpec((None, 1, n), lambda i, f: (layer, 0, 0), pipeline_mode=pl.Buffered(1)),
        ]
        out_specs.append(pl.BlockSpec((1, n), lambda i, f: (0, 0)))
        out_shape.append(jax.ShapeDtypeStruct((1, n), F32))
        args += [c.reshape(d, 1), mod_w, mod_b.reshape(depth, 1, n)]
        scratch.append(pltpu.VMEM((8, n), F32))
    outs = pl.pallas_call(
        functools.partial(_mlp_kernel, has_mod=mod_job is not None, n_sub=n_sub),
        grid=(t // tm, nf),
        in_specs=in_specs,
        out_specs=out_specs,
        out_shape=out_shape,
        scratch_shapes=scratch,
        compiler_params=_cparams("arbitrary", "arbitrary"),
        name="outproj_relu2_mlp",
    )(*args)
    if mod_job is None:
        return outs[0]
    return outs[0], outs[1].reshape(N_MOD, d)


def _layer_lower_bound(lbl_ref):
    l0 = lbl_ref[0]
    l1 = lbl_ref[1]
    lm = jnp.maximum(l0, l1)
    e0 = jnp.exp(l0 - lm)
    e1 = jnp.exp(l1 - lm)
    p0 = e0 / (e0 + e1)
    p1 = e1 / (e0 + e1)
    return (p0 + p1) - p0


def _hgrn_inproj_kernel(x_ref, mod_ref, gain_ref, lbl_ref, w_ref, side_ref, side2_ref,
                        q_ref, lf_ref, v_ref, g_ref, sideb_ref, side2b_ref, h_ref, *, hpt, scale, n_sub):
    j = pl.program_id(1)
    lb = _layer_lower_bound(lbl_ref)
    width = hpt * HG_DK

    def project(h, rs, with_side):
        if with_side:
            sideb_ref[...] = side_ref[...].astype(BF16)
            side2b_ref[...] = side2_ref[...].astype(BF16)
        res = jnp.dot(h, w_ref[...], preferred_element_type=F32)
        for hh in range(hpt):
            seg = lambda s: res[:, s * width + hh * HG_DK: s * width + (hh + 1) * HG_DK]
            q_ref[hh, rs, :] = (_silu(seg(0)) * scale).astype(BF16)
            forget = lb[hh] + (1.0 - lb[hh]) * _sigmoid(seg(1))
            lf_ref[hh, rs, :] = jnp.log(forget)
            v_ref[hh, rs, :] = seg(2).astype(BF16)
            g_ref[hh, rs, :] = _silu(seg(3)).astype(BF16)

    @pl.when(j == 0)
    def _():
        sub = x_ref.shape[0] // n_sub
        for s in range(n_sub):
            rs = slice(s * sub, (s + 1) * sub)
            h = _norm_mod(x_ref[rs, :], gain_ref[...], mod_ref[0:1, :], mod_ref[1:2, :]).astype(BF16)
            h_ref[rs, :] = h
            project(h, rs, s == 0)

    @pl.when(j > 0)
    def _():
        project(h_ref[...], slice(None), True)


def _hgrn_inproj_call(x, mod_l, gain, lb_logits, w_perm, n_heads, side_w, side_layer, side2_w, side2_layer,
                      tm=1024, hpt=2):
    t, d = x.shape
    depth = lb_logits.shape[0]
    assert depth == 2
    tn = 4 * hpt * HG_DK
    n_inner = n_heads // hpt
    head_major = lambda dt: jax.ShapeDtypeStruct((n_heads, t, HG_DK), dt)
    ospec = pl.BlockSpec((hpt, tm, HG_DK), lambda i, j: (j, i, 0))
    s_in, s_out, s_shape = _cast_job(side_w, side_layer, (t // tm) * n_inner, lambda i, j: i * n_inner + j)
    s2_in, s2_out, s2_shape = _cast_job(side2_w, side2_layer, (t // tm) * n_inner,
                                        lambda i, j: i * n_inner + j)
    return pl.pallas_call(
        functools.partial(_hgrn_inproj_kernel, hpt=hpt, scale=1.0 / math.sqrt(HG_DK), n_sub=4),
        grid=(t // tm, n_inner),
        in_specs=[
            pl.BlockSpec((tm, d), lambda i, j: (i, 0)),
            _resident((N_MOD, d)),
            _resident((1, d)),
            pl.BlockSpec((depth, hpt, 1, HG_DK), lambda i, j: (0, j, 0, 0)),
            pl.BlockSpec((d, tn), lambda i, j: (0, j)),
            s_in, s2_in,
        ],
        out_specs=[ospec] * 4 + [s_out, s2_out],
        out_shape=[head_major(BF16), head_major(F32), head_major(BF16), head_major(BF16), s_shape, s2_shape],
        scratch_shapes=[pltpu.VMEM((tm, d), BF16)],
        compiler_params=_cparams("arbitrary", "arbitrary"),
        name="hgrn_inproj",
    )(x, mod_l, gain, lb_logits.reshape(depth, n_heads, 1, HG_DK), w_perm, side_w, side2_w)


def _cum_matrix(tr):
    n_chunks = tr // CHUNK
    r = np.arange(tr)[:, None]
    c = np.arange(tr)[None, :]
    same = (r // CHUNK) == (c // CHUNK)
    incl = same & (c <= r)
    pivot = same & (c % CHUNK <= CHUNK // 2 - 1)
    rows = np.zeros((2 * n_chunks, tr), np.float32)
    for ch in range(n_chunks):
        rows[2 * ch, ch * CHUNK: ch * CHUNK + CHUNK // 2] = 1.0
        rows[2 * ch + 1, ch * CHUNK: (ch + 1) * CHUNK] = 1.0
    mats = np.concatenate([incl.astype(np.float32) - pivot.astype(np.float32), rows], axis=0)
    return jnp.asarray(mats, dtype=BF16)


RING = 3


def _hgrn_kernel(gain_ref, cm_ref, q_hbm, lf_hbm, v_hbm, g_hbm, side_ref,
                 o_ref, sideb_ref, st_ref, q_buf, lf_buf, v_buf, g_buf, sem, *, hb, tr):
    tstep = pl.program_id(1)
    n_steps = pl.num_programs(1)
    sideb_ref[...] = side_ref[...].astype(BF16)

    def copies(step):
        slot = step % RING
        rows = pl.ds(pl.multiple_of(step * tr, tr), tr)
        return [pltpu.make_async_copy(src.at[:, rows, :], buf.at[slot], sem.at[k, slot])
                for k, (src, buf) in enumerate(((q_hbm, q_buf), (lf_hbm, lf_buf), (v_hbm, v_buf), (g_hbm, g_buf)))]

    @pl.when(tstep == 0)
    def _():
        st_ref[...] = jnp.zeros_like(st_ref)
        for ahead in range(RING - 1):
            @pl.when(ahead < n_steps)
            def _():
                for cp in copies(tstep + ahead):
                    cp.start()

    @pl.when(tstep + RING - 1 < n_steps)
    def _():
        for cp in copies(tstep + RING - 1):
            cp.start()

    for cp in copies(tstep):
        cp.wait()
    slot = tstep % RING
    q_ref, lf_ref, v_ref, g_ref = q_buf.at[slot], lf_buf.at[slot], v_buf.at[slot], g_buf.at[slot]

    rowc = lax.broadcasted_iota(jnp.int32, (CHUNK, 2 * CHUNK), 0)
    colc = lax.broadcasted_iota(jnp.int32, (CHUNK, 2 * CHUNK), 1)
    causal2 = (colc % CHUNK) <= rowc
    n_chunks = tr // CHUNK
    zc = jnp.zeros((CHUNK, HG_DK), BF16)
    zs = jnp.zeros((HG_DK, HG_DK), BF16)

    def block_diag(a, b, z):
        return jnp.concatenate([jnp.concatenate([a, z], axis=1), jnp.concatenate([z, b], axis=1)], axis=0)

    n_pairs = hb // 2
    pairs = [(2 * pp, 2 * pp + 1) for pp in range(n_pairs)]
    chunk_rows = [slice(c * CHUNK, (c + 1) * CHUNK) for c in range(n_chunks)]

    cums = []
    for hh in range(hb):
        logf = lf_ref[hh]
        p_hi = logf.astype(BF16)
        p_lo = (logf - p_hi.astype(F32)).astype(BF16)
        cm = jnp.dot(cm_ref[...], jnp.concatenate([p_hi, p_lo], axis=1), preferred_element_type=F32)
        cums.append(cm[:, :HG_DK] + cm[:, HG_DK:])

    qe, ke, qb, ku, dec = [], [], [], [], []
    for hh in range(hb):
        bmp = cums[hh][:tr]
        qe_f = q_ref[hh].astype(F32) * jnp.exp(bmp)
        ke_f = (1.0 - jnp.exp(lf_ref[hh])) * jnp.exp(-bmp)
        qb_h, ku_h, dec_h = [], [], []
        for c, rs in enumerate(chunk_rows):
            piv = cums[hh][tr + 2 * c: tr + 2 * c + 1]
            blast = cums[hh][tr + 2 * c + 1: tr + 2 * c + 2]
            qb_h.append((qe_f[rs] * jnp.exp(piv)).astype(BF16))
            ku_h.append((ke_f[rs] * jnp.exp(blast - piv)).astype(BF16))
            dec_h.append(jnp.exp(blast))
        qe.append(qe_f.astype(BF16))
        ke.append(ke_f.astype(BF16))
        qb.append(qb_h)
        ku.append(ku_h)
        dec.append(dec_h)

    amat = [[None] * n_chunks for _ in pairs]
    upd = [[None] * n_chunks for _ in pairs]
    for c, rs in enumerate(chunk_rows):
        for pp, (h0, h1) in enumerate(pairs):
            v0 = v_ref[h0, rs, :]
            v1 = v_ref[h1, rs, :]
            a = lax.dot_general(jnp.concatenate([qe[h0][rs], qe[h1][rs]], axis=0),
                                jnp.concatenate([ke[h0][rs], ke[h1][rs]], axis=0),
                                (((1,), (1,)), ((), ())), preferred_element_type=F32)
            a = jnp.where(colc < CHUNK, a[:CHUNK], a[CHUNK:])
            amat[pp][c] = jnp.where(causal2, a, 0.0).astype(BF16)
            upd[pp][c] = lax.dot_general(jnp.concatenate([v0, v1], axis=0),
                                         block_diag(ku[h0][c], ku[h1][c], zc),
                                         (((0,), (0,)), ((), ())), preferred_element_type=F32)

    for c, rs in enumerate(chunk_rows):
        for pp, (h0, h1) in enumerate(pairs):
            st = st_ref[pp]
            st_b = st.astype(BF16)
            o_intra = jnp.dot(amat[pp][c], block_diag(v_ref[h0, rs, :], v_ref[h1, rs, :], zc),
                              preferred_element_type=F32)
            o_inter = lax.dot_general(jnp.concatenate([qb[h0][c], qb[h1][c]], axis=0),
                                      jnp.concatenate([st_b[:, :HG_DK], st_b[:, HG_DK:]], axis=0),
                                      (((1,), (1,)), ((), ())), preferred_element_type=F32)
            st_ref[pp] = st * jnp.concatenate([dec[h0][c], dec[h1][c]], axis=1) + upd[pp][c]
            for e, hh in enumerate((h0, h1)):
                oh = (o_intra[:, e * HG_DK:(e + 1) * HG_DK]
                      + o_inter[e * CHUNK:(e + 1) * CHUNK, e * HG_DK:(e + 1) * HG_DK])
                on = oh * lax.rsqrt(jnp.mean(oh * oh, axis=-1, keepdims=True) + EPS) * gain_ref[hh]
                o_ref[rs, hh * HG_DK:(hh + 1) * HG_DK] = (on * g_ref[hh, rs, :].astype(F32)).astype(BF16)


def _hgrn_call(q, lf, v, g, o_gain, side_w, side_layer, hb=16, tr=256):
    n_heads, t, dk = q.shape
    assert dk == HG_DK and hb == n_heads and hb % 2 == 0
    cm = _cum_matrix(tr)
    n_inner = t // tr
    blk = pl.BlockSpec(memory_space=pl.ANY)
    ring = lambda dt: pltpu.VMEM((RING, hb, tr, dk), dt)
    s_in, s_out, s_shape = _cast_job(side_w, side_layer, (n_heads // hb) * n_inner,
                                     lambda h, s: h * n_inner + s)
    return pl.pallas_call(
        functools.partial(_hgrn_kernel, hb=hb, tr=tr),
        grid=(n_heads // hb, n_inner),
        in_specs=[
            pl.BlockSpec((hb, 1, dk), lambda h, s: (h, 0, 0)),
            _resident(cm.shape),
            blk, blk, blk, blk,
            s_in,
        ],
        out_specs=[pl.BlockSpec((tr, hb * dk), lambda h, s: (s, h)), s_out],
        out_shape=[jax.ShapeDtypeStruct((t, n_heads * dk), BF16), s_shape],
        scratch_shapes=[pltpu.VMEM((hb // 2, dk, 2 * dk), F32),
                        ring(BF16), ring(F32), ring(BF16), ring(BF16),
                        pltpu.SemaphoreType.DMA((4, RING))],
        compiler_params=_cparams("arbitrary", "arbitrary"),
        name="hgrn_recurrence",
    )(o_gain.reshape(n_heads, 1, dk), cm, q, lf, v, g, side_w)


def kernel(x, c, mod_w, mod_b, norm_mix, norm_mlp, attn_w_in, attn_w_out, attn_q_gain, attn_k_gain,
           attn_sinks, hgrn_w_in, hgrn_w_out, hgrn_o_gain, hgrn_lb_logits, mlp_w1, mlp_w2):
    b, t, d = x.shape
    assert b == 1 and mod_w.shape[0] == 2
    xs = x.reshape(t, d)

    mod0, w_in_b = _mod_call(c, mod_w, mod_b, 0, attn_w_in, 0)

    q_dim = attn_w_out.shape[1]
    kv_dim = (attn_w_in.shape[2] - q_dim) // 2
    qk_gain = jnp.concatenate([jnp.tile(attn_q_gain[0], q_dim // HEAD_DIM),
                               jnp.tile(attn_k_gain[0], kv_dim // HEAD_DIM)]).reshape(1, q_dim + kv_dim)
    n_heads = hgrn_o_gain.shape[1]
    hpt = 2
    att, wob, w1b, w2b, hw_perm = _attn_layer_call(xs, mod0, norm_mix[0].reshape(1, d), w_in_b, qk_gain,
                                                   attn_sinks[0], attn_w_out, mlp_w1, mlp_w2, 0,
                                                   hgrn_w_in, 0, n_heads // hpt)
    xs, mod1 = _mlp_call(xs, att, mod0, norm_mlp[0].reshape(1, d), wob, w1b, w2b,
                         mod_job=(c, mod_w, mod_b, 1))

    hq, hlf, hv, hg, w1b, wob = _hgrn_inproj_call(xs, mod1, norm_mix[1].reshape(1, d), hgrn_lb_logits,
                                                  hw_perm, n_heads, mlp_w1, 1, hgrn_w_out, 0, hpt=hpt)
    ho, w2b = _hgrn_call(hq, hlf, hv, hg, hgrn_o_gain[0], mlp_w2, 1)
    xs = _mlp_call(xs, ho, mod1, norm_mlp[1].reshape(1, d), wob, w1b, w2b)
    return xs.reshape(b, t, d)
```
